```python
import math
import jax
import jax.numpy as jnp
from jax import lax
import numpy as np


D_MODEL = 2048
BATCH = 8
SEQ = 4096
DEPTH = 2

D_MIX = D_MODEL
N_MIXERS = 4
GROUP_WIDTH = D_MIX // N_MIXERS
HEAD_DIM = 128
N_HEADS = GROUP_WIDTH // HEAD_DIM
CHUNK = 64
SHORT_CONV = 4
MIX_CHUNK = 128
CONV_WIDTH = 31
D_FF = 4 * D_MODEL
D_IN_PROJ = 12 * GROUP_WIDTH + 2 * N_HEADS
EPS = 1e-6
NEG_BIG = -1e30
TINY = 1e-30

kernel_name = 'hybrid_parallel_group_trunk'


def rms_norm(x, w):
    xf = x.astype(jnp.float32)
    y = xf * lax.rsqrt(jnp.mean(xf * xf, axis=-1, keepdims=True) + EPS)
    return (y * w.astype(jnp.float32)).astype(x.dtype)


def layer_norm(x, w, b):
    xf = x.astype(jnp.float32)
    mu = jnp.mean(xf, axis=-1, keepdims=True)
    var = jnp.mean(jnp.square(xf - mu), axis=-1, keepdims=True)
    y = (xf - mu) * lax.rsqrt(var + EPS)
    return (y * w.astype(jnp.float32) + b.astype(jnp.float32)).astype(x.dtype)


def l2_norm(x):
    return x * lax.rsqrt(jnp.sum(x * x, axis=-1, keepdims=True) + EPS)


def causal_dwconv(x, w):
    width, ch = w.shape
    return lax.conv_general_dilated(
        x, w[:, None, :].astype(x.dtype), window_strides=(1,), padding=[(width - 1, 0)],
        dimension_numbers=('NWC', 'WIO', 'NWC'), feature_group_count=ch)


def split_heads(t):
    return t.reshape(t.shape[0], t.shape[1], N_HEADS, HEAD_DIM)


def split_in_proj(p):
    gw = GROUP_WIDTH
    sizes = [gw] * 4 + [gw] * 4 + [N_HEADS, N_HEADS] + [gw] * 2 + [gw] * 2
    points, acc = [], 0
    for s in sizes[:-1]:
        acc += s
        points.append(acc)
    return jnp.split(p, points, axis=-1)


def hgrn2_recurrence(q, k, v, log_f):
    bsz, seq, nh, dk = q.shape
    dv = v.shape[-1]
    n = seq // CHUNK

    def chunks(t):
        return t.reshape(bsz, n, CHUNK, nh, t.shape[-1]).transpose(1, 0, 3, 2, 4)

    qc, kc, vc = chunks(q), chunks(k), chunks(v)
    bc = jnp.cumsum(chunks(log_f), axis=3)
    causal = jnp.tril(jnp.ones((CHUNK, CHUNK), dtype=bool))[:, :, None]

    def step(state, inp):
        q_c, k_c, v_c, b_c = inp
        rel = b_c[:, :, :, None, :] - b_c[:, :, None, :, :]
        decay = jnp.exp(jnp.where(causal, rel, NEG_BIG))
        scores = jnp.einsum('bhijd,bhjd->bhij', decay * q_c[:, :, :, None, :], k_c)
        b_end = b_c[:, :, -1:, :]
        out = (jnp.einsum('bhij,bhje->bhie', scores, v_c)
               + jnp.einsum('bhid,bhde->bhie', q_c * jnp.exp(b_c), state))
        state = (state * jnp.exp(b_end)[:, :, 0, :, None]
                 + jnp.einsum('bhjd,bhje->bhde', k_c * jnp.exp(b_end - b_c), v_c))
        return state, out

    state0 = jnp.zeros((bsz, nh, dk, dv), jnp.float32)
    _, out = lax.scan(step, state0, (qc, kc, vc, bc))
    return out.transpose(1, 0, 3, 2, 4).reshape(bsz, seq, nh, dv)


def gated_delta_rule(q, k, v, beta, g):
    bsz, seq, nh, dk = q.shape
    dv = v.shape[-1]
    n = seq // CHUNK

    def chunks(t):
        t = t.reshape((bsz, n, CHUNK, nh) + t.shape[3:])
        return jnp.moveaxis(t, 3, 1)

    qc, kc, vc = chunks(q), chunks(k), chunks(v)
    bc = chunks(beta)
    gc = jnp.cumsum(chunks(g), axis=-1)
    idx = jnp.arange(CHUNK)
    incl = idx[:, None] >= idx[None, :]
    strict = idx[:, None] > idx[None, :]
    gamma = jnp.exp(jnp.where(incl, gc[..., :, None] - gc[..., None, :], NEG_BIG))
    k_beta = kc * bc[..., None]
    m = jnp.where(strict, jnp.einsum('bhnid,bhnjd->bhnij', k_beta, kc) * gamma, 0.0)
    t_mat = m + jnp.eye(CHUNK, dtype=m.dtype)

    def solve(rhs):
        return lax.linalg.triangular_solve(t_mat, rhs, left_side=True, lower=True, unit_diagonal=True)

    u = solve(vc * bc[..., None])
    w = solve(k_beta * jnp.exp(gc)[..., None])
    qk = jnp.einsum('bhnid,bhnjd->bhnij', qc, kc) * gamma
    q_dec = qc * jnp.exp(gc)[..., None]
    k_dec = kc * jnp.exp(gc[..., -1:] - gc)[..., None]
    g_end = jnp.exp(gc[..., -1])

    def step(state, inp):
        u_c, w_c, qk_c, qd_c, kd_c, ge_c = inp
        v_new = u_c - jnp.einsum('bhik,bhkv->bhiv', w_c, state)
        out = (jnp.einsum('bhik,bhkv->bhiv', qd_c, state)
               + jnp.einsum('bhij,bhjv->bhiv', qk_c, v_new))
        state = state * ge_c[..., None, None] + jnp.einsum('bhjk,bhjv->bhkv', kd_c, v_new)
        return state, out

    xs = tuple(jnp.moveaxis(t, 2, 0) for t in (u, w, qk, q_dec, k_dec, g_end))
    _, out = lax.scan(step, jnp.zeros((bsz, nh, dk, dv), jnp.float32), xs)
    return out.transpose(1, 0, 3, 2, 4).reshape(bsz, seq, nh, dv)


def spatial_gating(u, v, ln_w, ln_b, w_s, b_s):
    bsz, seq, _ = v.shape
    n = seq // MIX_CHUNK
    v = layer_norm(v, ln_w, ln_b).reshape(bsz, n, MIX_CHUNK, N_HEADS, HEAD_DIM)
    w_causal = jnp.where(jnp.tril(jnp.ones((MIX_CHUNK, MIX_CHUNK), dtype=bool)), w_s, 0.0)
    mixed = jnp.einsum('hij,bnjhd->bnihd', w_causal, v) + b_s.T[:, :, None]
    return u * mixed.reshape(bsz, seq, GROUP_WIDTH)


def conformer_conv(a, gate, dw_w, dw_b, ln_w, ln_b):
    y = a * jax.nn.sigmoid(gate)
    y = causal_dwconv(y, dw_w) + dw_b
    return jax.nn.silu(layer_norm(y, ln_w, ln_b))


def _fwd_setup_inputs(seed: int = 0) -> dict:
    key = jax.random.key(seed)
    ks = jax.random.split(key, 24)
    f32 = jnp.float32

    def normal(k, shape, scale):
        return jax.random.normal(k, shape, f32) * scale

    def gain(k, shape):
        return 1.0 + 0.02 * jax.random.normal(k, shape, f32)

    dt = jnp.exp(jax.random.uniform(ks[10], (DEPTH, N_HEADS), f32, math.log(1e-3), math.log(1e-1)))
    return {
        'x': normal(ks[0], (BATCH, SEQ, D_MODEL), 1.0),
        'lower_bounds': normal(ks[1], (DEPTH, GROUP_WIDTH), 0.1),
        'norm_mix_pre': gain(ks[2], (DEPTH, D_MODEL)),
        'norm_mix_post': gain(ks[3], (DEPTH, D_MODEL)),
        'norm_ff_pre': gain(ks[4], (DEPTH, D_MODEL)),
        'norm_ff_post': gain(ks[5], (DEPTH, D_MODEL)),
        'w_in': normal(ks[6], (DEPTH, D_MODEL, D_IN_PROJ), D_MODEL ** -0.5),
        'w_out': normal(ks[7], (DEPTH, D_MIX, D_MODEL), D_MIX ** -0.5),
        'hgrn_norm_w': gain(ks[8], (DEPTH, HEAD_DIM)),
        'gdn_conv_w': normal(ks[9], (DEPTH, SHORT_CONV, 3 * GROUP_WIDTH), SHORT_CONV ** -0.5),
        'gdn_a_log': jnp.log(jax.random.uniform(ks[11], (DEPTH, N_HEADS), f32, 1.0, 16.0)),
        'gdn_dt_bias': dt + jnp.log(-jnp.expm1(-dt)),
        'gdn_norm_w': gain(ks[12], (DEPTH, HEAD_DIM)),
        'gmlp_ln_w': gain(ks[13], (DEPTH, GROUP_WIDTH)),
        'gmlp_ln_b': normal(ks[14], (DEPTH, GROUP_WIDTH), 0.02),
        'gmlp_w_s': normal(ks[15], (DEPTH, N_HEADS, MIX_CHUNK, MIX_CHUNK), MIX_CHUNK ** -0.5),
        'gmlp_b_s': gain(ks[16], (DEPTH, N_HEADS, MIX_CHUNK)),
        'conv_dw_w': normal(ks[17], (DEPTH, CONV_WIDTH, GROUP_WIDTH), CONV_WIDTH ** -0.5),
        'conv_dw_b': normal(ks[18], (DEPTH, GROUP_WIDTH), 0.02),
        'conv_ln_w': gain(ks[19], (DEPTH, GROUP_WIDTH)),
        'conv_ln_b': normal(ks[20], (DEPTH, GROUP_WIDTH), 0.02),
        'w_ff1': normal(ks[21], (DEPTH, D_MODEL, D_FF), D_MODEL ** -0.5),
        'w_ff2': normal(ks[22], (DEPTH, D_FF, D_MODEL), D_FF ** -0.5),
    }


def _fwd_reference(x, lower_bounds, norm_mix_pre, norm_mix_post, norm_ff_pre, norm_ff_post,
              w_in, w_out, hgrn_norm_w, gdn_conv_w, gdn_a_log, gdn_dt_bias, gdn_norm_w,
              gmlp_ln_w, gmlp_ln_b, gmlp_w_s, gmlp_b_s, conv_dw_w, conv_dw_b, conv_ln_w,
              conv_ln_b, w_ff1, w_ff2):
    f32 = jnp.float32
    bsz, seq, _ = x.shape
    lb_soft = jax.nn.softmax(lower_bounds.astype(f32), axis=0)
    lb_all = jnp.cumsum(lb_soft, axis=0) - lb_soft[0]

    for l in range(DEPTH):
        h = rms_norm(x, norm_mix_pre[l])
        proj = (h @ w_in[l]).astype(f32)
        (a_q, a_f, a_i, a_g, b_q, b_k, b_v, b_z, b_beta, b_a,
         c_u, c_v, d_a, d_gate) = split_in_proj(proj)

        lb = lb_all[l]
        f_a = lb + (1.0 - lb) * jax.nn.sigmoid(a_f)
        log_f = jnp.log(jnp.maximum(f_a, TINY))
        k_a = (1.0 - lb) * jax.nn.sigmoid(-a_f)
        q_a = jax.nn.silu(a_q)
        o_a = hgrn2_recurrence(split_heads(q_a), split_heads(k_a), split_heads(a_i), split_heads(log_f))
        o_a = rms_norm(o_a, hgrn_norm_w[l]).reshape(bsz, seq, GROUP_WIDTH) * jax.nn.silu(a_g)

        qkv = jax.nn.silu(causal_dwconv(jnp.concatenate([b_q, b_k, b_v], axis=-1), gdn_conv_w[l]))
        q_b, k_b, v_b = jnp.split(qkv, 3, axis=-1)
        q_b = l2_norm(split_heads(q_b)) * (HEAD_DIM ** -0.5)
        k_b = l2_norm(split_heads(k_b))
        beta = jax.nn.sigmoid(b_beta)
        g_b = -jnp.exp(gdn_a_log[l].astype(f32)) * jax.nn.softplus(b_a + gdn_dt_bias[l].astype(f32))
        o_b = gated_delta_rule(q_b, k_b, split_heads(v_b), beta, g_b)
        o_b = rms_norm(o_b, gdn_norm_w[l]).reshape(bsz, seq, GROUP_WIDTH) * jax.nn.silu(b_z)

        o_c = spatial_gating(jax.nn.gelu(c_u, approximate=False), jax.nn.gelu(c_v, approximate=False),
                             gmlp_ln_w[l], gmlp_ln_b[l], gmlp_w_s[l].astype(f32), gmlp_b_s[l].astype(f32))

        o_d = conformer_conv(d_a, d_gate, conv_dw_w[l], conv_dw_b[l], conv_ln_w[l], conv_ln_b[l])

        mix = jnp.concatenate([o_a, o_b, o_c, o_d], axis=-1).astype(x.dtype)
        x = x + rms_norm(mix @ w_out[l], norm_mix_post[l])

        h = rms_norm(x, norm_ff_pre[l])
        y = jnp.square(jax.nn.relu(h @ w_ff1[l])) @ w_ff2[l]
        x = x + rms_norm(y, norm_ff_post[l])
    return x


import jax as _jax
import jax.numpy as _jnp

TWIN_FORMAT = 'train_step'
FWD_PARAMS = ['x', 'lower_bounds', 'norm_mix_pre', 'norm_mix_post', 'norm_ff_pre', 'norm_ff_post', 'w_in', 'w_out', 'hgrn_norm_w', 'gdn_conv_w', 'gdn_a_log', 'gdn_dt_bias', 'gdn_norm_w', 'gmlp_ln_w', 'gmlp_ln_b', 'gmlp_w_s', 'gmlp_b_s', 'conv_dw_w', 'conv_dw_b', 'conv_ln_w', 'conv_ln_b', 'w_ff1', 'w_ff2']
TWIN_WEIGHTS = ['lower_bounds', 'norm_mix_pre', 'norm_mix_post', 'norm_ff_pre', 'norm_ff_post', 'w_in', 'w_out', 'hgrn_norm_w', 'gdn_conv_w', 'gdn_a_log', 'gdn_dt_bias', 'gdn_norm_w', 'gmlp_ln_w', 'gmlp_ln_b', 'gmlp_w_s', 'gmlp_b_s', 'conv_dw_w', 'conv_dw_b', 'conv_ln_w', 'conv_ln_b', 'w_ff1', 'w_ff2']
TWIN_DIFF_INPUT = 'x'
TWIN_INPUTS = ['x', 'lower_bounds', 'norm_mix_pre', 'norm_mix_post', 'norm_ff_pre', 'norm_ff_post', 'w_in', 'w_out', 'hgrn_norm_w', 'gdn_conv_w', 'gdn_a_log', 'gdn_dt_bias', 'gdn_norm_w', 'gmlp_ln_w', 'gmlp_ln_b', 'gmlp_w_s', 'gmlp_b_s', 'conv_dw_w', 'conv_dw_b', 'conv_ln_w', 'conv_ln_b', 'w_ff1', 'w_ff2', 'loss_target', 'm_lower_bounds', 'm_norm_mix_pre', 'm_norm_mix_post', 'm_norm_ff_pre', 'm_norm_ff_post', 'm_w_in', 'm_w_out', 'm_hgrn_norm_w', 'm_gdn_conv_w', 'm_gdn_a_log', 'm_gdn_dt_bias', 'm_gdn_norm_w', 'm_gmlp_ln_w', 'm_gmlp_ln_b', 'm_gmlp_w_s', 'm_gmlp_b_s', 'm_conv_dw_w', 'm_conv_dw_b', 'm_conv_ln_w', 'm_conv_ln_b', 'm_w_ff1', 'm_w_ff2', 'v_lower_bounds', 'v_norm_mix_pre', 'v_norm_mix_post', 'v_norm_ff_pre', 'v_norm_ff_post', 'v_w_in', 'v_w_out', 'v_hgrn_norm_w', 'v_gdn_conv_w', 'v_gdn_a_log', 'v_gdn_dt_bias', 'v_gdn_norm_w', 'v_gmlp_ln_w', 'v_gmlp_ln_b', 'v_gmlp_w_s', 'v_gmlp_b_s', 'v_conv_dw_w', 'v_conv_dw_b', 'v_conv_ln_w', 'v_conv_ln_b', 'v_w_ff1', 'v_w_ff2']
TWIN_OUTPUTS = ['loss', 'grad_x', 'grad_lower_bounds', 'grad_norm_mix_pre', 'grad_norm_mix_post', 'grad_norm_ff_pre', 'grad_norm_ff_post', 'grad_w_in', 'grad_w_out', 'grad_hgrn_norm_w', 'grad_gdn_conv_w', 'grad_gdn_a_log', 'grad_gdn_dt_bias', 'grad_gdn_norm_w', 'grad_gmlp_ln_w', 'grad_gmlp_ln_b', 'grad_gmlp_w_s', 'grad_gmlp_b_s', 'grad_conv_dw_w', 'grad_conv_dw_b', 'grad_conv_ln_w', 'grad_conv_ln_b', 'grad_w_ff1', 'grad_w_ff2', 'delta_lower_bounds', 'delta_norm_mix_pre', 'delta_norm_mix_post', 'delta_norm_ff_pre', 'delta_norm_ff_post', 'delta_w_in', 'delta_w_out', 'delta_hgrn_norm_w', 'delta_gdn_conv_w', 'delta_gdn_a_log', 'delta_gdn_dt_bias', 'delta_gdn_norm_w', 'delta_gmlp_ln_w', 'delta_gmlp_ln_b', 'delta_gmlp_w_s', 'delta_gmlp_b_s', 'delta_conv_dw_w', 'delta_conv_dw_b', 'delta_conv_ln_w', 'delta_conv_ln_b', 'delta_w_ff1', 'delta_w_ff2', 'new_m_lower_bounds', 'new_m_norm_mix_pre', 'new_m_norm_mix_post', 'new_m_norm_ff_pre', 'new_m_norm_ff_post', 'new_m_w_in', 'new_m_w_out', 'new_m_hgrn_norm_w', 'new_m_gdn_conv_w', 'new_m_gdn_a_log', 'new_m_gdn_dt_bias', 'new_m_gdn_norm_w', 'new_m_gmlp_ln_w', 'new_m_gmlp_ln_b', 'new_m_gmlp_w_s', 'new_m_gmlp_b_s', 'new_m_conv_dw_w', 'new_m_conv_dw_b', 'new_m_conv_ln_w', 'new_m_conv_ln_b', 'new_m_w_ff1', 'new_m_w_ff2', 'new_v_lower_bounds', 'new_v_norm_mix_pre', 'new_v_norm_mix_post', 'new_v_norm_ff_pre', 'new_v_norm_ff_post', 'new_v_w_in', 'new_v_w_out', 'new_v_hgrn_norm_w', 'new_v_gdn_conv_w', 'new_v_gdn_a_log', 'new_v_gdn_dt_bias', 'new_v_gdn_norm_w', 'new_v_gmlp_ln_w', 'new_v_gmlp_ln_b', 'new_v_gmlp_w_s', 'new_v_gmlp_b_s', 'new_v_conv_dw_w', 'new_v_conv_dw_b', 'new_v_conv_ln_w', 'new_v_conv_ln_b', 'new_v_w_ff1', 'new_v_w_ff2']
TWIN_LEAF_KINDS = {'loss': 'loss', 'grad_x': 'grad_x', 'grad_lower_bounds': 'grad_w', 'grad_norm_mix_pre': 'grad_w', 'grad_norm_mix_post': 'grad_w', 'grad_norm_ff_pre': 'grad_w', 'grad_norm_ff_post': 'grad_w', 'grad_w_in': 'grad_w', 'grad_w_out': 'grad_w', 'grad_hgrn_norm_w': 'grad_w', 'grad_gdn_conv_w': 'grad_w', 'grad_gdn_a_log': 'grad_w', 'grad_gdn_dt_bias': 'grad_w', 'grad_gdn_norm_w': 'grad_w', 'grad_gmlp_ln_w': 'grad_w', 'grad_gmlp_ln_b': 'grad_w', 'grad_gmlp_w_s': 'grad_w', 'grad_gmlp_b_s': 'grad_w', 'grad_conv_dw_w': 'grad_w', 'grad_conv_dw_b': 'grad_w', 'grad_conv_ln_w': 'grad_w', 'grad_conv_ln_b': 'grad_w', 'grad_w_ff1': 'grad_w', 'grad_w_ff2': 'grad_w', 'delta_lower_bounds': 'delta_w', 'delta_norm_mix_pre': 'delta_w', 'delta_norm_mix_post': 'delta_w', 'delta_norm_ff_pre': 'delta_w', 'delta_norm_ff_post': 'delta_w', 'delta_w_in': 'delta_w', 'delta_w_out': 'delta_w', 'delta_hgrn_norm_w': 'delta_w', 'delta_gdn_conv_w': 'delta_w', 'delta_gdn_a_log': 'delta_w', 'delta_gdn_dt_bias': 'delta_w', 'delta_gdn_norm_w': 'delta_w', 'delta_gmlp_ln_w': 'delta_w', 'delta_gmlp_ln_b': 'delta_w', 'delta_gmlp_w_s': 'delta_w', 'delta_gmlp_b_s': 'delta_w', 'delta_conv_dw_w': 'delta_w', 'delta_conv_dw_b': 'delta_w', 'delta_conv_ln_w': 'delta_w', 'delta_conv_ln_b': 'delta_w', 'delta_w_ff1': 'delta_w', 'delta_w_ff2': 'delta_w', 'new_m_lower_bounds': 'new_m', 'new_m_norm_mix_pre': 'new_m', 'new_m_norm_mix_post': 'new_m', 'new_m_norm_ff_pre': 'new_m', 'new_m_norm_ff_post': 'new_m', 'new_m_w_in': 'new_m', 'new_m_w_out': 'new_m', 'new_m_hgrn_norm_w': 'new_m', 'new_m_gdn_conv_w': 'new_m', 'new_m_gdn_a_log': 'new_m', 'new_m_gdn_dt_bias': 'new_m', 'new_m_gdn_norm_w': 'new_m', 'new_m_gmlp_ln_w': 'new_m', 'new_m_gmlp_ln_b': 'new_m', 'new_m_gmlp_w_s': 'new_m', 'new_m_gmlp_b_s': 'new_m', 'new_m_conv_dw_w': 'new_m', 'new_m_conv_dw_b': 'new_m', 'new_m_conv_ln_w': 'new_m', 'new_m_conv_ln_b': 'new_m', 'new_m_w_ff1': 'new_m', 'new_m_w_ff2': 'new_m', 'new_v_lower_bounds': 'new_v', 'new_v_norm_mix_pre': 'new_v', 'new_v_norm_mix_post': 'new_v', 'new_v_norm_ff_pre': 'new_v', 'new_v_norm_ff_post': 'new_v', 'new_v_w_in': 'new_v', 'new_v_w_out': 'new_v', 'new_v_hgrn_norm_w': 'new_v', 'new_v_gdn_conv_w': 'new_v', 'new_v_gdn_a_log': 'new_v', 'new_v_gdn_dt_bias': 'new_v', 'new_v_gdn_norm_w': 'new_v', 'new_v_gmlp_ln_w': 'new_v', 'new_v_gmlp_ln_b': 'new_v', 'new_v_gmlp_w_s': 'new_v', 'new_v_gmlp_b_s': 'new_v', 'new_v_conv_dw_w': 'new_v', 'new_v_conv_dw_b': 'new_v', 'new_v_conv_ln_w': 'new_v', 'new_v_conv_ln_b': 'new_v', 'new_v_w_ff1': 'new_v', 'new_v_w_ff2': 'new_v'}


def _forward(args):
    return _fwd_reference(*[args[k] for k in FWD_PARAMS])


def _output_shape():
    def fwd():
        inp = _fwd_setup_inputs(0)
        return _fwd_reference(*[inp[k] for k in FWD_PARAMS])
    out = _jax.eval_shape(fwd)
    return out.shape, out.dtype

N_MICROBATCH = 1
ADAM_LR = 0.001
ADAM_B1 = 0.9
ADAM_B2 = 0.999
ADAM_EPS = 1e-08
ADAM_WD = 0.01
ADAM_STEP = 10
PER_EXAMPLE_BATCH_AXIS = {'x': 0, 'loss_target': 0}
SHARED_INPUTS = []
_WEIGHT_DTYPES = {'lower_bounds': _jnp.float32, 'norm_mix_pre': _jnp.float32, 'norm_mix_post': _jnp.float32, 'norm_ff_pre': _jnp.float32, 'norm_ff_post': _jnp.float32, 'w_in': _jnp.float32, 'w_out': _jnp.float32, 'hgrn_norm_w': _jnp.float32, 'gdn_conv_w': _jnp.float32, 'gdn_a_log': _jnp.float32, 'gdn_dt_bias': _jnp.float32, 'gdn_norm_w': _jnp.float32, 'gmlp_ln_w': _jnp.float32, 'gmlp_ln_b': _jnp.float32, 'gmlp_w_s': _jnp.float32, 'gmlp_b_s': _jnp.float32, 'conv_dw_w': _jnp.float32, 'conv_dw_b': _jnp.float32, 'conv_ln_w': _jnp.float32, 'conv_ln_b': _jnp.float32, 'w_ff1': _jnp.float32, 'w_ff2': _jnp.float32}
MOMENT_SCALE = {'lower_bounds': 3.447601e-02, 'norm_mix_pre': 1.946431e+00, 'norm_mix_post': 1.671623e+01, 'norm_ff_pre': 1.993021e+00, 'norm_ff_post': 1.741676e+01, 'w_in': 1.122573e+00, 'w_out': 4.201251e+00, 'hgrn_norm_w': 4.418057e+00, 'gdn_conv_w': 1.126176e+00, 'gdn_a_log': 1.536742e+00, 'gdn_dt_bias': 1.493960e+00, 'gdn_norm_w': 5.182758e+00, 'gmlp_ln_w': 2.687078e-01, 'gmlp_ln_b': 3.014040e-01, 'gmlp_w_s': 2.185895e-01, 'gmlp_b_s': 6.052636e-01, 'conv_dw_w': 2.089292e+00, 'conv_dw_b': 1.811347e+01, 'conv_ln_w': 7.359408e+00, 'conv_ln_b': 1.023677e+01, 'w_ff1': 1.020965e+00, 'w_ff2': 5.039630e+00}


def _to_microbatches(a, axis):
    t = _jnp.moveaxis(a, axis, 0)
    t = t.reshape((N_MICROBATCH, t.shape[0] // N_MICROBATCH) + t.shape[1:])
    return _jnp.moveaxis(t, 1, axis + 1)


def setup_inputs(seed: int = 0) -> dict:
    inp = _fwd_setup_inputs(seed)
    key = _jax.random.fold_in(_jax.random.key(seed), 7919)
    shape, _ = _output_shape()
    out = dict(inp)
    out["loss_target"] = _jax.random.normal(_jax.random.fold_in(key, 0), shape, _jnp.float32)
    for i, name in enumerate(TWIN_WEIGHTS):
        w = inp[name].astype(_jnp.float32)
        if MOMENT_SCALE is None:
            s = _jnp.sqrt(_jnp.mean(_jnp.square(w)) + 1e-30)
        else:
            s = MOMENT_SCALE[name]
        km, kv = _jax.random.split(_jax.random.fold_in(key, i + 1))
        out[name] = w
        out["m_" + name] = s * _jax.random.normal(km, w.shape, _jnp.float32)
        out["v_" + name] = (s * s) * _jax.random.uniform(kv, w.shape, _jnp.float32, 0.5, 1.5)
    if N_MICROBATCH > 1:
        for name, axis in PER_EXAMPLE_BATCH_AXIS.items():
            out[name] = _to_microbatches(out[name], axis)
    return {'x': out['x'], 'lower_bounds': out['lower_bounds'], 'norm_mix_pre': out['norm_mix_pre'], 'norm_mix_post': out['norm_mix_post'], 'norm_ff_pre': out['norm_ff_pre'], 'norm_ff_post': out['norm_ff_post'], 'w_in': out['w_in'], 'w_out': out['w_out'], 'hgrn_norm_w': out['hgrn_norm_w'], 'gdn_conv_w': out['gdn_conv_w'], 'gdn_a_log': out['gdn_a_log'], 'gdn_dt_bias': out['gdn_dt_bias'], 'gdn_norm_w': out['gdn_norm_w'], 'gmlp_ln_w': out['gmlp_ln_w'], 'gmlp_ln_b': out['gmlp_ln_b'], 'gmlp_w_s': out['gmlp_w_s'], 'gmlp_b_s': out['gmlp_b_s'], 'conv_dw_w': out['conv_dw_w'], 'conv_dw_b': out['conv_dw_b'], 'conv_ln_w': out['conv_ln_w'], 'conv_ln_b': out['conv_ln_b'], 'w_ff1': out['w_ff1'], 'w_ff2': out['w_ff2'], 'loss_target': out['loss_target'], 'm_lower_bounds': out['m_lower_bounds'], 'm_norm_mix_pre': out['m_norm_mix_pre'], 'm_norm_mix_post': out['m_norm_mix_post'], 'm_norm_ff_pre': out['m_norm_ff_pre'], 'm_norm_ff_post': out['m_norm_ff_post'], 'm_w_in': out['m_w_in'], 'm_w_out': out['m_w_out'], 'm_hgrn_norm_w': out['m_hgrn_norm_w'], 'm_gdn_conv_w': out['m_gdn_conv_w'], 'm_gdn_a_log': out['m_gdn_a_log'], 'm_gdn_dt_bias': out['m_gdn_dt_bias'], 'm_gdn_norm_w': out['m_gdn_norm_w'], 'm_gmlp_ln_w': out['m_gmlp_ln_w'], 'm_gmlp_ln_b': out['m_gmlp_ln_b'], 'm_gmlp_w_s': out['m_gmlp_w_s'], 'm_gmlp_b_s': out['m_gmlp_b_s'], 'm_conv_dw_w': out['m_conv_dw_w'], 'm_conv_dw_b': out['m_conv_dw_b'], 'm_conv_ln_w': out['m_conv_ln_w'], 'm_conv_ln_b': out['m_conv_ln_b'], 'm_w_ff1': out['m_w_ff1'], 'm_w_ff2': out['m_w_ff2'], 'v_lower_bounds': out['v_lower_bounds'], 'v_norm_mix_pre': out['v_norm_mix_pre'], 'v_norm_mix_post': out['v_norm_mix_post'], 'v_norm_ff_pre': out['v_norm_ff_pre'], 'v_norm_ff_post': out['v_norm_ff_post'], 'v_w_in': out['v_w_in'], 'v_w_out': out['v_w_out'], 'v_hgrn_norm_w': out['v_hgrn_norm_w'], 'v_gdn_conv_w': out['v_gdn_conv_w'], 'v_gdn_a_log': out['v_gdn_a_log'], 'v_gdn_dt_bias': out['v_gdn_dt_bias'], 'v_gdn_norm_w': out['v_gdn_norm_w'], 'v_gmlp_ln_w': out['v_gmlp_ln_w'], 'v_gmlp_ln_b': out['v_gmlp_ln_b'], 'v_gmlp_w_s': out['v_gmlp_w_s'], 'v_gmlp_b_s': out['v_gmlp_b_s'], 'v_conv_dw_w': out['v_conv_dw_w'], 'v_conv_dw_b': out['v_conv_dw_b'], 'v_conv_ln_w': out['v_conv_ln_w'], 'v_conv_ln_b': out['v_conv_ln_b'], 'v_w_ff1': out['v_w_ff1'], 'v_w_ff2': out['v_w_ff2']}


def _loss(weights, diff, rest, loss_target):
    with _jax.named_scope("forward"):
        args = {**rest, TWIN_DIFF_INPUT: diff, **{k: w.astype(_WEIGHT_DTYPES[k]) for k, w in weights.items()}}
        y = _forward(args)
    with _jax.named_scope("loss_head"):
        err = _jnp.square(y.astype(_jnp.float32) - loss_target)
        return 0.5 * _jnp.sum(_jnp.mean(err, axis=-1)) if err.ndim else 0.5 * err


def _adamw(w, g, m, v):
    m = ADAM_B1 * m + (1.0 - ADAM_B1) * g
    v = ADAM_B2 * v + (1.0 - ADAM_B2) * _jnp.square(g)
    m_hat = m / (1.0 - ADAM_B1 ** ADAM_STEP)
    v_hat = v / (1.0 - ADAM_B2 ** ADAM_STEP)
    delta = -ADAM_LR * (m_hat / (_jnp.sqrt(v_hat) + ADAM_EPS) + ADAM_WD * w)
    return delta, m, v


def reference(x, lower_bounds, norm_mix_pre, norm_mix_post, norm_ff_pre, norm_ff_post, w_in, w_out, hgrn_norm_w, gdn_conv_w, gdn_a_log, gdn_dt_bias, gdn_norm_w, gmlp_ln_w, gmlp_ln_b, gmlp_w_s, gmlp_b_s, conv_dw_w, conv_dw_b, conv_ln_w, conv_ln_b, w_ff1, w_ff2, loss_target, m_lower_bounds, m_norm_mix_pre, m_norm_mix_post, m_norm_ff_pre, m_norm_ff_post, m_w_in, m_w_out, m_hgrn_norm_w, m_gdn_conv_w, m_gdn_a_log, m_gdn_dt_bias, m_gdn_norm_w, m_gmlp_ln_w, m_gmlp_ln_b, m_gmlp_w_s, m_gmlp_b_s, m_conv_dw_w, m_conv_dw_b, m_conv_ln_w, m_conv_ln_b, m_w_ff1, m_w_ff2, v_lower_bounds, v_norm_mix_pre, v_norm_mix_post, v_norm_ff_pre, v_norm_ff_post, v_w_in, v_w_out, v_hgrn_norm_w, v_gdn_conv_w, v_gdn_a_log, v_gdn_dt_bias, v_gdn_norm_w, v_gmlp_ln_w, v_gmlp_ln_b, v_gmlp_w_s, v_gmlp_b_s, v_conv_dw_w, v_conv_dw_b, v_conv_ln_w, v_conv_ln_b, v_w_ff1, v_w_ff2):
    given = dict(x=x, lower_bounds=lower_bounds, norm_mix_pre=norm_mix_pre, norm_mix_post=norm_mix_post, norm_ff_pre=norm_ff_pre, norm_ff_post=norm_ff_post, w_in=w_in, w_out=w_out, hgrn_norm_w=hgrn_norm_w, gdn_conv_w=gdn_conv_w, gdn_a_log=gdn_a_log, gdn_dt_bias=gdn_dt_bias, gdn_norm_w=gdn_norm_w, gmlp_ln_w=gmlp_ln_w, gmlp_ln_b=gmlp_ln_b, gmlp_w_s=gmlp_w_s, gmlp_b_s=gmlp_b_s, conv_dw_w=conv_dw_w, conv_dw_b=conv_dw_b, conv_ln_w=conv_ln_w, conv_ln_b=conv_ln_b, w_ff1=w_ff1, w_ff2=w_ff2, loss_target=loss_target, m_lower_bounds=m_lower_bounds, m_norm_mix_pre=m_norm_mix_pre, m_norm_mix_post=m_norm_mix_post, m_norm_ff_pre=m_norm_ff_pre, m_norm_ff_post=m_norm_ff_post, m_w_in=m_w_in, m_w_out=m_w_out, m_hgrn_norm_w=m_hgrn_norm_w, m_gdn_conv_w=m_gdn_conv_w, m_gdn_a_log=m_gdn_a_log, m_gdn_dt_bias=m_gdn_dt_bias, m_gdn_norm_w=m_gdn_norm_w, m_gmlp_ln_w=m_gmlp_ln_w, m_gmlp_ln_b=m_gmlp_ln_b, m_gmlp_w_s=m_gmlp_w_s, m_gmlp_b_s=m_gmlp_b_s, m_conv_dw_w=m_conv_dw_w, m_conv_dw_b=m_conv_dw_b, m_conv_ln_w=m_conv_ln_w, m_conv_ln_b=m_conv_ln_b, m_w_ff1=m_w_ff1, m_w_ff2=m_w_ff2, v_lower_bounds=v_lower_bounds, v_norm_mix_pre=v_norm_mix_pre, v_norm_mix_post=v_norm_mix_post, v_norm_ff_pre=v_norm_ff_pre, v_norm_ff_post=v_norm_ff_post, v_w_in=v_w_in, v_w_out=v_w_out, v_hgrn_norm_w=v_hgrn_norm_w, v_gdn_conv_w=v_gdn_conv_w, v_gdn_a_log=v_gdn_a_log, v_gdn_dt_bias=v_gdn_dt_bias, v_gdn_norm_w=v_gdn_norm_w, v_gmlp_ln_w=v_gmlp_ln_w, v_gmlp_ln_b=v_gmlp_ln_b, v_gmlp_w_s=v_gmlp_w_s, v_gmlp_b_s=v_gmlp_b_s, v_conv_dw_w=v_conv_dw_w, v_conv_dw_b=v_conv_dw_b, v_conv_ln_w=v_conv_ln_w, v_conv_ln_b=v_conv_ln_b, v_w_ff1=v_w_ff1, v_w_ff2=v_w_ff2)
    weights = {n: given[n] for n in TWIN_WEIGHTS}
    shared = {n: given[n] for n in SHARED_INPUTS}
    per_example = {n: given[n] for n in ['x']}
    grad_fn = _jax.value_and_grad(_loss, argnums=(0, 1))

    def one_microbatch(ex, loss_target):
        ex = dict(ex)
        diff = ex.pop(TWIN_DIFF_INPUT)
        return grad_fn(weights, diff, {**shared, **ex}, loss_target)

    if N_MICROBATCH == 1:
        loss, (grad_w, grad_x) = one_microbatch(per_example, given["loss_target"])
    else:
        def body(carry, xs):
            loss_sum, grad_sum = carry
            l_k, (gw_k, gx_k) = one_microbatch(xs[0], xs[1])
            with _jax.named_scope("update"):
                return (loss_sum + l_k, _jax.tree.map(_jnp.add, grad_sum, gw_k)), gx_k

        init = (_jnp.zeros((), _jnp.float32), _jax.tree.map(_jnp.zeros_like, weights))
        (loss, grad_w), grad_x = _jax.lax.scan(body, init, (per_example, given["loss_target"]))
    with _jax.named_scope("update"):
        delta_w, new_m, new_v = {}, {}, {}
        for n in TWIN_WEIGHTS:
            delta_w[n], new_m[n], new_v[n] = _adamw(weights[n], grad_w[n], given["m_" + n], given["v_" + n])
    return (loss, grad_x, *[grad_w[n] for n in TWIN_WEIGHTS], *[delta_w[n] for n in TWIN_WEIGHTS],
            *[new_m[n] for n in TWIN_WEIGHTS], *[new_v[n] for n in TWIN_WEIGHTS])
```

```python
import functools
import math

import jax
import jax.numpy as jnp
from jax import lax
from jax.experimental import pallas as pl
from jax.experimental.pallas import tpu as pltpu

f32 = jnp.float32
bf16 = jnp.bfloat16
HI = lax.Precision.HIGHEST

N_DEV = 8
DEPTH = 2
D_MODEL = 2048
GW = 512
HD = 128
NH = 4
CHUNK = 64
MIX_CHUNK = 128
CONV_WIDTH = 31
SHORT_CONV = 4
D_FF = 4 * D_MODEL
D_IN = 12 * GW + 2 * NH
D_IN_PAD = 12 * GW + 128
ROW_TILE = 256
EPS = 1e-6
TINY = 1e-30
ADAM_LR, ADAM_B1, ADAM_B2, ADAM_EPS, ADAM_WD, ADAM_STEP = 0.001, 0.9, 0.999, 1e-08, 0.01, 10
MESH = pl.DeviceIdType.MESH


def _dotb(a, b, ca, cb):
    return lax.dot_general(a.astype(bf16), b.astype(bf16), (((ca,), (cb,)), ((), ())),
                           preferred_element_type=f32)


@jax.custom_vjp
def mm(a, b):
    return _dotb(a, b, 1, 0)


def _mm_f(a, b):
    return mm(a, b), (a, b)


def _mm_b(res, ct):
    a, b = res
    return _dotb(ct, b, 1, 1), _dotb(a, ct, 0, 0)


mm.defvjp(_mm_f, _mm_b)


@jax.custom_vjp
def mm_nt(a, b):
    return _dotb(a, b, 1, 1)


def _mmnt_f(a, b):
    return mm_nt(a, b), (a, b)


def _mmnt_b(res, ct):
    a, b = res
    return _dotb(ct, b, 1, 0), _dotb(ct, a, 0, 0)


mm_nt.defvjp(_mmnt_f, _mmnt_b)


@jax.custom_vjp
def mm_tn(a, b):
    return _dotb(a, b, 0, 0)


def _mmtn_f(a, b):
    return mm_tn(a, b), (a, b)


def _mmtn_b(res, ct):
    a, b = res
    return _dotb(b, ct, 1, 1), _dotb(a, ct, 1, 0)


mm_tn.defvjp(_mmtn_f, _mmtn_b)


def mmh(a, b):
    return jnp.dot(a, b, precision=HI, preferred_element_type=f32)


def _rms(x, w):
    return x * lax.rsqrt(jnp.mean(x * x, axis=-1, keepdims=True) + EPS) * w


def _ln(x, w, b):
    mu = jnp.mean(x, axis=-1, keepdims=True)
    xc = x - mu
    var = jnp.mean(xc * xc, axis=-1, keepdims=True)
    return xc * lax.rsqrt(var + EPS) * w + b


def _gelu(x):
    return 0.5 * x * (1.0 + lax.erf(x * (2.0 ** -0.5)))


def _iota2(n, m, axis):
    return lax.broadcasted_iota(jnp.int32, (n, m), axis)


def _tri(n, strict=False):
    r, c = _iota2(n, n, 0), _iota2(n, n, 1)
    return (r > c) if strict else (r >= c)


def _eye(n):
    return (_iota2(n, n, 0) == _iota2(n, n, 1)).astype(f32)


def fn_norm(x, g):
    return (_rms(x, g),)


def fn_lb(lower_bounds):
    s = jax.nn.softmax(lower_bounds, axis=0)
    rows, cum = [], None
    for i in range(DEPTH):
        cum = s[i:i + 1] if cum is None else cum + s[i:i + 1]
        rows.append(cum - s[0:1])
    return (jnp.concatenate(rows, axis=0),)


def fn_hgrn(aq, af, ai, ag, lb, nw, st):
    c = aq.shape[0]
    sig = jax.nn.sigmoid(af)
    f = lb + (1.0 - lb) * sig
    logf = jnp.log(jnp.maximum(f, TINY))
    k = (1.0 - lb) * jax.nn.sigmoid(-af)
    q = jax.nn.silu(aq)
    v = ai
    b = mmh(_tri(c).astype(f32), logf)
    rel = b[:, None, :] - b[None, :, :]
    dec = jnp.exp(jnp.minimum(rel, 0.0)) * q[:, None, :] * k[None, :, :]
    scores = jnp.where(_tri(c), jnp.sum(dec, axis=-1), 0.0)
    b_end = b[c - 1:c, :]
    out = mm(scores, v) + mm_nt(q * jnp.exp(b), st)
    st_new = st * jnp.exp(b_end) + mm_tn(v, k * jnp.exp(b_end - b))
    o = _rms(out, nw) * jax.nn.silu(ag)
    return o, st_new


def fn_gdn_pre(tq, tk, tv, bq, bk, bv, p8, conv_w, alog, dtb):
    tile = bq.shape[0]
    h = tq.shape[0]
    outs = []
    for seg, (tl, cur) in enumerate(((tq, bq), (tk, bk), (tv, bv))):
        xe = jnp.concatenate([tl, cur], axis=0)
        acc = None
        for kk in range(SHORT_CONV):
            off = h - (SHORT_CONV - 1) + kk
            term = conv_w[kk:kk + 1, seg * GW:(seg + 1) * GW] * xe[off:off + tile, :]
            acc = term if acc is None else acc + term
        outs.append(jax.nn.silu(acc))
    sq, sk, sv = outs
    qh, kh = [], []
    for hh in range(NH):
        a = sq[:, hh * HD:(hh + 1) * HD]
        qh.append(a * lax.rsqrt(jnp.sum(a * a, axis=-1, keepdims=True) + EPS) * (HD ** -0.5))
        a = sk[:, hh * HD:(hh + 1) * HD]
        kh.append(a * lax.rsqrt(jnp.sum(a * a, axis=-1, keepdims=True) + EPS))
    q = jnp.concatenate(qh, axis=1)
    k = jnp.concatenate(kh, axis=1)
    beta = jax.nn.sigmoid(p8)
    g = -jnp.exp(alog) * jax.nn.softplus(p8 + dtb)
    r, cc = _iota2(128, GW, 0), _iota2(128, GW, 1) // HD
    e_beta = (r == cc).astype(f32)
    e_g = (r == cc + NH).astype(f32)
    return q, k, sv, mmh(beta, e_beta), mmh(g, e_g)


def fn_gdn(q, k, v, beta, g, z, nw, s):
    c = q.shape[0]
    gc = mmh(_tri(c).astype(f32), g)
    gcol = gc[:, 0:1]
    grow = jnp.sum(gcol * _eye(c), axis=0, keepdims=True)
    gamma = jnp.where(_tri(c), jnp.exp(jnp.minimum(gcol - grow, 0.0)), 0.0)
    kb = k * beta
    m = jnp.where(_tri(c, strict=True), mm_nt(kb, k) * gamma, 0.0)
    eye = _eye(c)
    tinv = eye - m
    p = m
    for _ in range(int(math.log2(c)) - 1):
        p = mmh(p, p)
        tinv = mmh(tinv, eye + p)
    egc = jnp.exp(gc)
    u = mmh(tinv, v * beta)
    w = mmh(tinv, kb * egc)
    qk = mm_nt(q, k) * gamma
    gc_end = gc[c - 1:c, :]
    q_dec = q * egc
    k_dec = k * jnp.exp(gc_end - gc)
    v_new = u - mm(w, s)
    out = mm(q_dec, s) + mm(qk, v_new)
    s_new = s * jnp.exp(gc_end) + mm_tn(k_dec, v_new)
    o = _rms(out, nw) * jax.nn.silu(z)
    return o, s_new


def fn_gmlp(cu, cv, ln_w, ln_b, w_s, b_s):
    n = cu.shape[0]
    ug = _gelu(cu)
    vn = _ln(_gelu(cv), ln_w, ln_b)
    eye = _eye(n)
    cols = []
    for hh in range(NH):
        wc = jnp.where(_tri(n), w_s[hh * n:(hh + 1) * n, :], 0.0)
        bcol = jnp.sum(b_s[hh:hh + 1, :] * eye, axis=1, keepdims=True)
        cols.append(mm(wc, vn[:, hh * HD:(hh + 1) * HD]) + bcol)
    return (ug * jnp.concatenate(cols, axis=1),)


def fn_conv(ta, tg, a, gate, dw_w, dw_b, ln_w, ln_b):
    tile = a.shape[0]
    h = ta.shape[0]
    ya = jnp.concatenate([ta, a], axis=0)
    yg = jnp.concatenate([tg, gate], axis=0)
    y = ya * jax.nn.sigmoid(yg)
    acc = None
    for kk in range(CONV_WIDTH):
        off = h - (CONV_WIDTH - 1) + kk
        term = dw_w[kk:kk + 1, :] * y[off:off + tile, :]
        acc = term if acc is None else acc + term
    return (jax.nn.silu(_ln(acc + dw_b, ln_w, ln_b)),)


def _full_spec(arr):
    nd = arr.ndim
    return pl.BlockSpec(arr.shape, lambda *_: (0,) * nd)


def rowwise(name, fn, tiled, params, outs, tile):
    t = tiled[0][0].shape[0]
    tile = min(tile, t)
    nt, np_ = len(tiled), len(params)

    def body(*refs):
        vals = [r[...].astype(f32) for r in refs[:nt + np_]]
        res = fn(*vals)
        for o_ref, r in zip(refs[nt + np_:], res):
            o_ref[...] = r.astype(o_ref.dtype)

    in_specs = [pl.BlockSpec((tile, w), lambda i, c=c: (i, c)) for _, w, c in tiled]
    in_specs += [_full_spec(p) for p in params]
    out_specs = [pl.BlockSpec((tile, w), lambda i: (i, 0)) for w, _ in outs]
    out_shape = [jax.ShapeDtypeStruct((t, w), dt) for w, dt in outs]
    return pl.pallas_call(body, grid=(t // tile,), in_specs=in_specs, out_specs=out_specs,
                          out_shape=out_shape, name=name)(*[a for a, _, _ in tiled], *params)


def rowwise_bwd(name, fn, tiled, params, cots, gouts, tile, addto=None):
    t = tiled[0][0].shape[0]
    tile = min(tile, t)
    nt, np_, nc = len(tiled), len(params), len(cots)
    na = 0 if addto is None else 1
    gidx = [i for i, g in enumerate(gouts) if g is not None]

    def body(*refs):
        i = pl.program_id(0)
        vals = [r[...].astype(f32) for r in refs[:nt + np_]]
        cvals = tuple(r[...].astype(f32) for r in refs[nt + np_:nt + np_ + nc])
        _, vjp = jax.vjp(fn, *vals)
        grads = vjp(cvals)
        orefs = refs[nt + np_ + nc + na:]
        for n, j in enumerate(gidx):
            g = grads[j]
            if na and n == 0:
                g = g + refs[nt + np_ + nc][...].astype(f32)
            orefs[n][...] = g.astype(orefs[n].dtype)
        prefs = orefs[len(gidx):]

        @pl.when(i == 0)
        def _():
            for r in prefs:
                r[...] = jnp.zeros_like(r)

        for r, g in zip(prefs, grads[nt:]):
            r[...] += g

    in_specs = [pl.BlockSpec((tile, w), lambda i, c=c: (i, c)) for _, w, c in tiled]
    in_specs += [_full_spec(p) for p in params]
    in_specs += [pl.BlockSpec((tile, w), lambda i, c=c: (i, c)) for _, w, c in cots]
    args = [a for a, _, _ in tiled] + list(params) + [a for a, _, _ in cots]
    if na:
        in_specs.append(pl.BlockSpec((tile, addto[1]), lambda i, c=addto[2]: (i, c)))
        args.append(addto[0])
    out_specs = [pl.BlockSpec((tile, tiled[j][1]), lambda i: (i, 0)) for j in gidx]
    out_shape = [jax.ShapeDtypeStruct((t, tiled[j][1]), gouts[j]) for j in gidx]
    out_specs += [_full_spec(p) for p in params]
    out_shape += [jax.ShapeDtypeStruct(p.shape, f32) for p in params]
    res = pl.pallas_call(body, grid=(t // tile,), in_specs=in_specs, out_specs=out_specs,
                         out_shape=out_shape, name=name)(*args)
    return res[:len(gidx)], res[len(gidx):]


def halo_fwd(name, fn, tiled, params, outs, tile, halo):
    t = tiled[0][0].shape[0]
    tile = min(tile, t)
    hal = [j for j, x in enumerate(tiled) if x[3]]
    nt, nh, np_ = len(tiled), len(hal), len(params)
    per = tile // halo

    def body(*refs):
        i = pl.program_id(0)
        first = (i > 0).astype(f32)
        tails = [r[...].astype(f32) * first for r in refs[:nh]]
        vals = [r[...].astype(f32) for r in refs[nh:nh + nt + np_]]
        res = fn(*tails, *vals)
        for o_ref, r in zip(refs[nh + nt + np_:], res):
            o_ref[...] = r.astype(o_ref.dtype)

    in_specs = [pl.BlockSpec((halo, tiled[j][1]), lambda i, c=tiled[j][2]: (jnp.maximum(i * per - 1, 0), c))
                for j in hal]
    in_specs += [pl.BlockSpec((tile, w), lambda i, c=c: (i, c)) for _, w, c, _ in tiled]
    in_specs += [_full_spec(p) for p in params]
    out_specs = [pl.BlockSpec((tile, w), lambda i: (i, 0)) for w, _ in outs]
    out_shape = [jax.ShapeDtypeStruct((t, w), dt) for w, dt in outs]
    args = [tiled[j][0] for j in hal] + [x[0] for x in tiled] + list(params)
    return pl.pallas_call(body, grid=(t // tile,), in_specs=in_specs, out_specs=out_specs,
                          out_shape=out_shape, name=name)(*args)


def halo_bwd(name, fn, tiled, params, cots, gdtype, tile, halo):
    t = tiled[0][0].shape[0]
    tile = min(tile, t)
    hal = [j for j, x in enumerate(tiled) if x[3]]
    nt, nh, np_, nc = len(tiled), len(hal), len(params), len(cots)
    per = tile // halo
    n_tiles = t // tile

    def body(*refs):
        s = pl.program_id(0)
        i = n_tiles - 1 - s
        first = (i > 0).astype(f32)
        tails = [r[...].astype(f32) * first for r in refs[:nh]]
        vals = [r[...].astype(f32) for r in refs[nh:nh + nt + np_]]
        cvals = tuple(r[...].astype(f32) for r in refs[nh + nt + np_:nh + nt + np_ + nc])
        n_in = nh + nt + np_ + nc
        orefs = refs[n_in:n_in + nt]
        prefs = refs[n_in + nt:n_in + nt + np_]
        carries = refs[n_in + nt + np_:]

        @pl.when(s == 0)
        def _():
            for r in prefs:
                r[...] = jnp.zeros_like(r)
            for r in carries:
                r[...] = jnp.zeros_like(r)

        _, vjp = jax.vjp(fn, *tails, *vals)
        grads = vjp(cvals)
        for j in range(nt):
            g = grads[nh + j]
            if j in hal:
                cr = carries[hal.index(j)]
                g = jnp.concatenate([g[:tile - halo], g[tile - halo:] + cr[...]], axis=0)
            orefs[j][...] = g.astype(orefs[j].dtype)
        for n in range(nh):
            carries[n][...] = grads[n] * first
        for r, g in zip(prefs, grads[nh + nt:]):
            r[...] += g

    rev = lambda s: n_tiles - 1 - s
    in_specs = [pl.BlockSpec((halo, tiled[j][1]),
                             lambda s, c=tiled[j][2]: (jnp.maximum(rev(s) * per - 1, 0), c)) for j in hal]
    in_specs += [pl.BlockSpec((tile, w), lambda s, c=c: (rev(s), c)) for _, w, c, _ in tiled]
    in_specs += [_full_spec(p) for p in params]
    in_specs += [pl.BlockSpec((tile, w), lambda s, c=c: (rev(s), c)) for _, w, c in cots]
    out_specs = [pl.BlockSpec((tile, w), lambda s: (rev(s), 0)) for _, w, _, _ in tiled]
    out_shape = [jax.ShapeDtypeStruct((t, w), gdtype) for _, w, _, _ in tiled]
    out_specs += [_full_spec(p) for p in params]
    out_shape += [jax.ShapeDtypeStruct(p.shape, f32) for p in params]
    scratch = [pltpu.VMEM((halo, tiled[j][1]), f32) for j in hal]
    args = [tiled[j][0] for j in hal] + [x[0] for x in tiled] + list(params) + [a for a, _, _ in cots]
    res = pl.pallas_call(body, grid=(n_tiles,), in_specs=in_specs, out_specs=out_specs,
                         out_shape=out_shape, scratch_shapes=scratch, name=name)(*args)
    return res[:nt], res[nt:]


def scan_fwd(name, fn, tiled, pparams, sparams, out_dtype):
    t = tiled[0][0].shape[0]
    n = t // CHUNK
    nt, npp, nsp = len(tiled), len(pparams), len(sparams)

    def body(*refs):
        c = pl.program_id(1)
        n_in = nt + npp + nsp
        o_ref, sv_ref, st = refs[n_in], refs[n_in + 1], refs[n_in + 2]

        @pl.when(c == 0)
        def _():
            st[...] = jnp.zeros_like(st)

        vals = [r[...].astype(f32) for r in refs[:n_in]]
        s_in = st[...]
        sv_ref[...] = s_in
        o, s_new = fn(*vals, s_in)
        o_ref[...] = o.astype(o_ref.dtype)
        st[...] = s_new

    in_specs = [pl.BlockSpec((CHUNK, HD), lambda h, c, b=b: (c, b + h)) for _, b in tiled]
    in_specs += [pl.BlockSpec((1, HD), lambda h, c: (0, h)) for _ in pparams]
    in_specs += [_full_spec(p) for p in sparams]
    out_specs = [pl.BlockSpec((CHUNK, HD), lambda h, c: (c, h)),
                 pl.BlockSpec((None, None, HD, HD), lambda h, c: (h, c, 0, 0))]
    out_shape = [jax.ShapeDtypeStruct((t, GW), out_dtype), jax.ShapeDtypeStruct((NH, n, HD, HD), f32)]
    return pl.pallas_call(body, grid=(NH, n), in_specs=in_specs, out_specs=out_specs, out_shape=out_shape,
                          scratch_shapes=[pltpu.VMEM((HD, HD), f32)], name=name)(
        *[a for a, _ in tiled], *pparams, *sparams)


def scan_bwd(name, fn, tiled, pparams, sparams, states, cot, gdtypes):
    t = tiled[0][0].shape[0]
    n = t // CHUNK
    nt, npp, nsp = len(tiled), len(pparams), len(sparams)

    def body(*refs):
        h, s = pl.program_id(0), pl.program_id(1)
        n_in = nt + npp + nsp
        vals = [r[...].astype(f32) for r in refs[:n_in]]
        s_in = refs[n_in][...]
        do = refs[n_in + 1][...].astype(f32)
        orefs = refs[n_in + 2:n_in + 2 + nt]
        pprefs = refs[n_in + 2 + nt:n_in + 2 + nt + npp]
        sprefs = refs[n_in + 2 + nt + npp:n_in + 2 + nt + npp + nsp]
        ds = refs[n_in + 2 + nt + npp + nsp]

        @pl.when(s == 0)
        def _():
            ds[...] = jnp.zeros_like(ds)
            for r in pprefs:
                r[...] = jnp.zeros_like(r)

        @pl.when((s == 0) & (h == 0))
        def _():
            for r in sprefs:
                r[...] = jnp.zeros_like(r)

        _, vjp = jax.vjp(fn, *vals, s_in)
        grads = vjp((do, ds[...]))
        for j in range(nt):
            orefs[j][...] = grads[j].astype(orefs[j].dtype)
        for r, g in zip(pprefs, grads[nt:nt + npp]):
            r[...] += g
        for r, g in zip(sprefs, grads[nt + npp:nt + npp + nsp]):
            r[...] += g
        ds[...] = grads[n_in]

    rev = lambda s: n - 1 - s
    in_specs = [pl.BlockSpec((CHUNK, HD), lambda h, s, b=b: (rev(s), b + h)) for _, b in tiled]
    in_specs += [pl.BlockSpec((1, HD), lambda h, s: (0, h)) for _ in pparams]
    in_specs += [_full_spec(p) for p in sparams]
    in_specs += [pl.BlockSpec((None, None, HD, HD), lambda h, s: (h, rev(s), 0, 0)),
                 pl.BlockSpec((CHUNK, HD), lambda h, s, b=cot[1]: (rev(s), b + h))]
    out_specs = [pl.BlockSpec((CHUNK, HD), lambda h, s: (rev(s), h)) for _ in tiled]
    out_shape = [jax.ShapeDtypeStruct((t, GW), dt) for dt in gdtypes]
    out_specs += [pl.BlockSpec((1, HD), lambda h, s: (0, h)) for _ in pparams]
    out_shape += [jax.ShapeDtypeStruct(p.shape, f32) for p in pparams]
    out_specs += [_full_spec(p) for p in sparams]
    out_shape += [jax.ShapeDtypeStruct(p.shape, f32) for p in sparams]
    res = pl.pallas_call(body, grid=(NH, n), in_specs=in_specs, out_specs=out_specs, out_shape=out_shape,
                         scratch_shapes=[pltpu.VMEM((HD, HD), f32)], name=name)(
        *[a for a, _ in tiled], *pparams, *sparams, states, cot[0])
    return res[:nt], res[nt:nt + npp], res[nt + npp:]


def matmul(name, a, b, mode, out_dtype=f32, tm=512, tn=512, tk=512, epilogue=None, extra=None):
    if mode == 'nn':
        (m, k), n = a.shape, b.shape[1]
    elif mode == 'nt':
        (m, k), n = a.shape, b.shape[0]
    else:
        (k, m), n = a.shape, b.shape[1]
    tm, tn, tk = min(tm, m), min(tn, n), min(tk, k)
    nk = k // tk
    ca, cb = {'nn': (1, 0), 'nt': (1, 1), 'tn': (0, 0)}[mode]

    def body(*refs):
        a_ref, b_ref = refs[0], refs[1]
        acc = refs[-1]
        kk = pl.program_id(2)

        @pl.when(kk == 0)
        def _():
            acc[...] = jnp.zeros_like(acc)

        acc[...] += _dotb(a_ref[...], b_ref[...], ca, cb)

        @pl.when(kk == nk - 1)
        def _():
            r = acc[...]
            if epilogue == 'relu2':
                refs[2][...] = r
                refs[3][...] = jnp.square(jnp.maximum(r, 0.0)).astype(bf16)
            elif epilogue == 'relu2_bwd':
                refs[3][...] = (r * 2.0 * jnp.maximum(refs[2][...], 0.0)).astype(refs[3].dtype)
            else:
                refs[2][...] = r.astype(refs[2].dtype)

    if mode == 'nn':
        a_spec = pl.BlockSpec((tm, tk), lambda i, j, kk: (i, kk))
        b_spec = pl.BlockSpec((tk, tn), lambda i, j, kk: (kk, j))
    elif mode == 'nt':
        a_spec = pl.BlockSpec((tm, tk), lambda i, j, kk: (i, kk))
        b_spec = pl.BlockSpec((tn, tk), lambda i, j, kk: (j, kk))
    else:
        a_spec = pl.BlockSpec((tk, tm), lambda i, j, kk: (kk, i))
        b_spec = pl.BlockSpec((tk, tn), lambda i, j, kk: (kk, j))
    o_spec = pl.BlockSpec((tm, tn), lambda i, j, kk: (i, j))
    in_specs, args = [a_spec, b_spec], [a, b]
    if epilogue == 'relu2':
        out_specs = [o_spec, o_spec]
        out_shape = [jax.ShapeDtypeStruct((m, n), f32), jax.ShapeDtypeStruct((m, n), bf16)]
    else:
        out_specs, out_shape = o_spec, jax.ShapeDtypeStruct((m, n), out_dtype)
        if epilogue == 'relu2_bwd':
            in_specs.append(o_spec)
            args.append(extra)
    return pl.pallas_call(body, grid=(m // tm, n // tn, nk), in_specs=in_specs, out_specs=out_specs,
                          out_shape=out_shape, scratch_shapes=[pltpu.VMEM((tm, tn), f32)], name=name)(*args)


def final_loss(name, x, y, g, target):
    t, d = x.shape
    tile = min(ROW_TILE, t)

    def body(x_ref, y_ref, g_ref, t_ref, dx_ref, l_ref):
        i = pl.program_id(0)

        @pl.when(i == 0)
        def _():
            l_ref[...] = jnp.zeros_like(l_ref)

        err = x_ref[...] + _rms(y_ref[...], g_ref[...]) - t_ref[...]
        dx_ref[...] = err * (1.0 / d)
        l_ref[...] += 0.5 * jnp.sum(jnp.mean(err * err, axis=-1, keepdims=True), axis=0, keepdims=True)

    row = pl.BlockSpec((tile, d), lambda i: (i, 0))
    return pl.pallas_call(
        body, grid=(t // tile,), in_specs=[row, row, _full_spec(g), row],
        out_specs=[row, pl.BlockSpec((1, 1), lambda i: (0, 0))],
        out_shape=[jax.ShapeDtypeStruct((t, d), f32), jax.ShapeDtypeStruct((1, 1), f32)], name=name)(x, y, g, target)


def adamw(name, w, m, v, gslots, tr=128):
    nl, r, c = w.shape
    tr = min(tr, r)
    nr = r // tr
    ns = gslots[0].shape[0]
    c1 = 1.0 / (1.0 - ADAM_B1 ** ADAM_STEP)
    c2 = 1.0 / (1.0 - ADAM_B2 ** ADAM_STEP)

    def body(*refs):
        w_ref, m_ref, v_ref = refs[:3]
        g_refs = refs[3:3 + nl]
        go_ref, d_ref, mo_ref, vo_ref = refs[3 + nl:]
        l = pl.program_id(0)
        g = None
        for li in range(nl):
            s = g_refs[li][0].astype(f32)
            for k in range(1, ns):
                s = s + g_refs[li][k].astype(f32)
            g = s if g is None else jnp.where(l == li, s, g)
        mn = ADAM_B1 * m_ref[...] + (1.0 - ADAM_B1) * g
        vn = ADAM_B2 * v_ref[...] + (1.0 - ADAM_B2) * jnp.square(g)
        go_ref[...] = g
        mo_ref[...] = mn
        vo_ref[...] = vn
        d_ref[...] = -ADAM_LR * ((mn * c1) / (jnp.sqrt(vn * c2) + ADAM_EPS) + ADAM_WD * w_ref[...])

    blk = pl.BlockSpec((None, tr, c), lambda l, i: (l, i, 0))

    def gspec(li):
        return pl.BlockSpec((ns, tr, c), lambda l, i: (0, jnp.where(l == li, i, jnp.where(l < li, 0, nr - 1)), 0))

    return pl.pallas_call(
        body, grid=(nl, nr), in_specs=[blk, blk, blk] + [gspec(li) for li in range(nl)],
        out_specs=[blk] * 4, out_shape=[jax.ShapeDtypeStruct(w.shape, f32)] * 4, name=name)(w, m, v, *gslots)


def exchange(name, arrays, gather):
    n = len(arrays)

    def body(*refs):
        ins, outs = refs[:n], refs[n:2 * n]
        send_sems, recv_sems, loc_sems = refs[2 * n:]
        x, y, c = lax.axis_index("x"), lax.axis_index("y"), lax.axis_index("c")
        me = 4 * x + 2 * y + c
        sends, recvs, locs = [], [], []
        for a in range(n):
            lc = pltpu.make_async_copy(ins[a] if gather else ins[a].at[me], outs[a].at[me], loc_sems.at[a])
            lc.start()
            locs.append(lc)
            for k in range(1, N_DEV):
                px, py, pc = x ^ ((k >> 2) & 1), y ^ ((k >> 1) & 1), c ^ (k & 1)
                peer = 4 * px + 2 * py + pc
                src = ins[a] if gather else ins[a].at[peer]
                cp = pltpu.make_async_remote_copy(src_ref=src, dst_ref=outs[a].at[me], send_sem=send_sems.at[a, k],
                                                  recv_sem=recv_sems.at[a, k], device_id=(px, py, pc), device_id_type=MESH)
                cp.start()
                sends.append(cp)
                recvs.append(pltpu.make_async_remote_copy(src_ref=src, dst_ref=outs[a].at[peer], send_sem=send_sems.at[a, k],
                                                          recv_sem=recv_sems.at[a, k], device_id=(px, py, pc),
                                                          device_id_type=MESH))
        for cp in recvs:
            cp.wait_recv()
        for cp in sends:
            cp.wait_send()
        for lc in locs:
            lc.wait()

    anyspec = pl.BlockSpec(memory_space=pl.ANY)
    out_shape = [jax.ShapeDtypeStruct(((N_DEV,) + a.shape) if gather else a.shape, a.dtype) for a in arrays]
    return pl.pallas_call(
        body, in_specs=[anyspec] * n, out_specs=[anyspec] * n, out_shape=out_shape,
        scratch_shapes=[pltpu.SemaphoreType.DMA((n, N_DEV)), pltpu.SemaphoreType.DMA((n, N_DEV)),
                        pltpu.SemaphoreType.DMA((n,))], name=name)(*arrays)


REPLICATED = ('lower_bounds', 'norm_mix_pre', 'norm_mix_post', 'norm_ff_pre', 'norm_ff_post', 'hgrn_norm_w',
              'gdn_a_log', 'gdn_dt_bias', 'gdn_norm_w', 'gmlp_ln_w', 'gmlp_ln_b', 'gmlp_w_s', 'gmlp_b_s',
              'conv_dw_b', 'conv_ln_w', 'conv_ln_b')
WEIGHTS = ('lower_bounds', 'norm_mix_pre', 'norm_mix_post', 'norm_ff_pre', 'norm_ff_post', 'w_in', 'w_out',
           'hgrn_norm_w', 'gdn_conv_w', 'gdn_a_log', 'gdn_dt_bias', 'gdn_norm_w', 'gmlp_ln_w', 'gmlp_ln_b',
           'gmlp_w_s', 'gmlp_b_s', 'conv_dw_w', 'conv_dw_b', 'conv_ln_w', 'conv_ln_b', 'w_ff1', 'w_ff2')


def _row(v):
    return v.reshape(1, -1)


def _pad_lanes(v, offset):
    return jnp.zeros((1, 128), f32).at[0, offset:offset + v.shape[0]].set(v)


def _layer_fwd(l, x0, p, lb_all):
    t = x0.shape[0]
    sv = {'x0': x0}
    (h,) = rowwise(f"norm_mix_pre{l}", fn_norm, [(x0, D_MODEL, 0)], [p['g_mix_pre']], [(D_MODEL, bf16)], ROW_TILE)
    proj = matmul(f"proj{l}", h, p['w_in'], 'nn', f32, tn=896)
    sv.update(h=h, proj=proj)
    lb = lb_all[l:l + 1]
    o_a, st_a = scan_fwd(f"hgrn{l}", fn_hgrn, [(proj, 0), (proj, 4), (proj, 8), (proj, 12)], [lb], [p['hgrn_norm_w']], bf16)
    q, k, v, beta, g = halo_fwd(
        f"gdn_pre{l}", fn_gdn_pre,
        [(proj, GW, 4, True), (proj, GW, 5, True), (proj, GW, 6, True), (proj, 128, 48, False)],
        [p['gdn_conv_w'], p['alog'], p['dtb']], [(GW, f32)] * 5, ROW_TILE, 8)
    o_b, st_b = scan_fwd(f"gdn{l}", fn_gdn, [(q, 0), (k, 0), (v, 0), (beta, 0), (g, 0), (proj, 28)], [],
                         [p['gdn_norm_w']], bf16)
    (o_c,) = rowwise(f"gmlp{l}", fn_gmlp, [(proj, GW, 8), (proj, GW, 9)],
                     [p['gmlp_ln_w'], p['gmlp_ln_b'], p['gmlp_w_s'], p['gmlp_b_s']], [(GW, bf16)], MIX_CHUNK)
    (o_d,) = halo_fwd(f"conv{l}", fn_conv, [(proj, GW, 10, True), (proj, GW, 11, True)],
                      [p['conv_dw_w'], p['conv_dw_b'], p['conv_ln_w'], p['conv_ln_b']], [(GW, bf16)], ROW_TILE, 32)
    mix = jnp.concatenate([o_a, o_b, o_c, o_d], axis=1)
    y1 = matmul(f"out_proj{l}", mix, p['w_out'], 'nn', f32)
    (x1,) = rowwise(f"res_mix{l}", lambda x, y, gg: (x + _rms(y, gg),), [(x0, D_MODEL, 0), (y1, D_MODEL, 0)],
                    [p['g_mix_post']], [(D_MODEL, f32)], ROW_TILE)
    (h2,) = rowwise(f"norm_ff_pre{l}", fn_norm, [(x1, D_MODEL, 0)], [p['g_ff_pre']], [(D_MODEL, bf16)], ROW_TILE)
    u, a = matmul(f"ff1_{l}", h2, p['w_ff1'], 'nn', epilogue='relu2')
    y2 = matmul(f"ff2_{l}", a, p['w_ff2'], 'nn', f32)
    sv.update(st_a=st_a, q=q, k=k, v=v, beta=beta, g=g, st_b=st_b, mix=mix, y1=y1, x1=x1, h2=h2, u=u, a=a, y2=y2)
    return sv


def _layer_bwd(l, dx, sv, p, lb_all):
    gr = {}
    t = dx.shape[0]
    (dy2,), (gr['norm_ff_post'],) = rowwise_bwd(f"res_ff_bwd{l}", fn_norm, [(sv['y2'], D_MODEL, 0)], [p['g_ff_post']],
                                                [(dx, D_MODEL, 0)], [bf16], ROW_TILE)
    du = matmul(f"ff2_dx{l}", dy2, p['w_ff2'], 'nt', bf16, epilogue='relu2_bwd', extra=sv['u'])
    gr['w_ff2'] = matmul(f"ff2_dw{l}", sv['a'], dy2, 'tn', f32)
    dh2 = matmul(f"ff1_dx{l}", du, p['w_ff1'], 'nt', f32)
    gr['w_ff1'] = matmul(f"ff1_dw{l}", sv['h2'], du, 'tn', f32)
    (dx1,), (gr['norm_ff_pre'],) = rowwise_bwd(f"norm_ff_pre_bwd{l}", fn_norm, [(sv['x1'], D_MODEL, 0)], [p['g_ff_pre']],
                                               [(dh2, D_MODEL, 0)], [f32], ROW_TILE, addto=(dx, D_MODEL, 0))
    (dy1,), (gr['norm_mix_post'],) = rowwise_bwd(f"res_mix_bwd{l}", fn_norm, [(sv['y1'], D_MODEL, 0)], [p['g_mix_post']],
                                                 [(dx1, D_MODEL, 0)], [bf16], ROW_TILE)
    dmix = matmul(f"out_proj_dx{l}", dy1, p['w_out'], 'nt', f32)
    gr['w_out'] = matmul(f"out_proj_dw{l}", sv['mix'], dy1, 'tn', f32)
    proj = sv['proj']
    lb = lb_all[l:l + 1]
    d_a, (dlb,), (gr['hgrn_norm_w'],) = scan_bwd(
        f"hgrn_bwd{l}", fn_hgrn, [(proj, 0), (proj, 4), (proj, 8), (proj, 12)], [lb], [p['hgrn_norm_w']],
        sv['st_a'], (dmix, 0), [bf16] * 4)
    d_b, _, (gr['gdn_norm_w'],) = scan_bwd(
        f"gdn_bwd{l}", fn_gdn, [(sv['q'], 0), (sv['k'], 0), (sv['v'], 0), (sv['beta'], 0), (sv['g'], 0), (proj, 28)],
        [], [p['gdn_norm_w']], sv['st_b'], (dmix, 4), [f32] * 5 + [bf16])
    d_bp, (gr['gdn_conv_w'], dalog, ddtb) = halo_bwd(
        f"gdn_pre_bwd{l}", fn_gdn_pre,
        [(proj, GW, 4, True), (proj, GW, 5, True), (proj, GW, 6, True), (proj, 128, 48, False)],
        [p['gdn_conv_w'], p['alog'], p['dtb']], [(d_b[j], GW, 0) for j in range(5)], bf16, ROW_TILE, 8)
    gr['gdn_a_log'] = dalog[0, NH:2 * NH]
    gr['gdn_dt_bias'] = ddtb[0, NH:2 * NH]
    d_c, (gr['gmlp_ln_w'], gr['gmlp_ln_b'], gr['gmlp_w_s'], gr['gmlp_b_s']) = rowwise_bwd(
        f"gmlp_bwd{l}", fn_gmlp, [(proj, GW, 8), (proj, GW, 9)],
        [p['gmlp_ln_w'], p['gmlp_ln_b'], p['gmlp_w_s'], p['gmlp_b_s']], [(dmix, GW, 2)], [bf16, bf16], MIX_CHUNK)
    d_d, (gr['conv_dw_w'], gr['conv_dw_b'], gr['conv_ln_w'], gr['conv_ln_b']) = halo_bwd(
        f"conv_bwd{l}", fn_conv, [(proj, GW, 10, True), (proj, GW, 11, True)],
        [p['conv_dw_w'], p['conv_dw_b'], p['conv_ln_w'], p['conv_ln_b']], [(dmix, GW, 3)], bf16, ROW_TILE, 32)
    dproj = jnp.concatenate(list(d_a) + [d_bp[0], d_bp[1], d_bp[2], d_b[5]] + list(d_c) + list(d_d) + [d_bp[3]], axis=1)
    dh = matmul(f"proj_dx{l}", dproj, p['w_in'], 'nt', f32, tk=896)
    gr['w_in'] = matmul(f"proj_dw{l}", sv['h'], dproj, 'tn', f32, tn=896)
    (dx0,), (gr['norm_mix_pre'],) = rowwise_bwd(f"norm_mix_pre_bwd{l}", fn_norm, [(sv['x0'], D_MODEL, 0)], [p['g_mix_pre']],
                                                [(dh, D_MODEL, 0)], [f32], ROW_TILE, addto=(dx1, D_MODEL, 0))
    return dx0, gr, dlb


def kernel(x, lower_bounds, norm_mix_pre, norm_mix_post, norm_ff_pre, norm_ff_post, w_in, w_out, hgrn_norm_w, gdn_conv_w, gdn_a_log, gdn_dt_bias, gdn_norm_w, gmlp_ln_w, gmlp_ln_b, gmlp_w_s, gmlp_b_s, conv_dw_w, conv_dw_b, conv_ln_w, conv_ln_b, w_ff1, w_ff2, loss_target, m_lower_bounds, m_norm_mix_pre, m_norm_mix_post, m_norm_ff_pre, m_norm_ff_post, m_w_in, m_w_out, m_hgrn_norm_w, m_gdn_conv_w, m_gdn_a_log, m_gdn_dt_bias, m_gdn_norm_w, m_gmlp_ln_w, m_gmlp_ln_b, m_gmlp_w_s, m_gmlp_b_s, m_conv_dw_w, m_conv_dw_b, m_conv_ln_w, m_conv_ln_b, m_w_ff1, m_w_ff2, v_lower_bounds, v_norm_mix_pre, v_norm_mix_post, v_norm_ff_pre, v_norm_ff_post, v_w_in, v_w_out, v_hgrn_norm_w, v_gdn_conv_w, v_gdn_a_log, v_gdn_dt_bias, v_gdn_norm_w, v_gmlp_ln_w, v_gmlp_ln_b, v_gmlp_w_s, v_gmlp_b_s, v_conv_dw_w, v_conv_dw_b, v_conv_ln_w, v_conv_ln_b, v_w_ff1, v_w_ff2):
    loc = dict(locals())
    W = {n: loc[n] for n in WEIGHTS}
    M = {n: loc['m_' + n] for n in WEIGHTS}
    V = {n: loc['v_' + n] for n in WEIGHTS}
    t = x.shape[1]
    me = 4 * lax.axis_index("x") + 2 * lax.axis_index("y") + lax.axis_index("c")

    g_in, g_out, g_ff1, g_ff2, g_gconv, g_dconv = exchange(
        "gather_weights",
        [w_in.astype(bf16), w_out.astype(bf16), w_ff1.astype(bf16), w_ff2.astype(bf16), gdn_conv_w, conv_dw_w], True)
    w_in_full = jnp.moveaxis(g_in, 0, 2).reshape(DEPTH, D_MODEL, D_IN)
    w_in_pad = jnp.concatenate([w_in_full[:, :, :8 * GW], w_in_full[:, :, 8 * GW + 2 * NH:],
                                w_in_full[:, :, 8 * GW:8 * GW + 2 * NH],
                                jnp.zeros((DEPTH, D_MODEL, 128 - 2 * NH), bf16)], axis=2)
    w_out_full = jnp.moveaxis(g_out, 0, 1).reshape(DEPTH, D_MODEL, D_MODEL)
    w_ff1_full = jnp.moveaxis(g_ff1, 0, 2).reshape(DEPTH, D_MODEL, D_FF)
    w_ff2_full = jnp.moveaxis(g_ff2, 0, 1).reshape(DEPTH, D_FF, D_MODEL)
    gconv_full = jnp.moveaxis(g_gconv, 0, 2).reshape(DEPTH, SHORT_CONV, 3 * GW)
    dconv_full = jnp.moveaxis(g_dconv, 0, 2).reshape(DEPTH, CONV_WIDTH, GW)

    (lb_all,) = rowwise("lower_bounds", fn_lb, [(lower_bounds, GW, 0)], [], [(GW, f32)], DEPTH)

    P = []
    for l in range(DEPTH):
        P.append(dict(
            g_mix_pre=_row(norm_mix_pre[l]), g_mix_post=_row(norm_mix_post[l]), g_ff_pre=_row(norm_ff_pre[l]),
            g_ff_post=_row(norm_ff_post[l]), w_in=w_in_pad[l], w_out=w_out_full[l], w_ff1=w_ff1_full[l],
            w_ff2=w_ff2_full[l], hgrn_norm_w=_row(hgrn_norm_w[l]), gdn_conv_w=gconv_full[l],
            alog=_pad_lanes(gdn_a_log[l], NH), dtb=_pad_lanes(gdn_dt_bias[l], NH), gdn_norm_w=_row(gdn_norm_w[l]),
            gmlp_ln_w=_row(gmlp_ln_w[l]), gmlp_ln_b=_row(gmlp_ln_b[l]), gmlp_w_s=gmlp_w_s[l].reshape(NH * MIX_CHUNK, MIX_CHUNK),
            gmlp_b_s=gmlp_b_s[l], conv_dw_w=dconv_full[l], conv_dw_b=_row(conv_dw_b[l]), conv_ln_w=_row(conv_ln_w[l]),
            conv_ln_b=_row(conv_ln_b[l])))

    xs = x[0]
    saved = []
    for l in range(DEPTH):
        sv = _layer_fwd(l, xs, P[l], lb_all)
        saved.append(sv)
        if l < DEPTH - 1:
            (xs,) = rowwise(f"res_ff{l}", lambda a, y, gg: (a + _rms(y, gg),), [(sv['x1'], D_MODEL, 0), (sv['y2'], D_MODEL, 0)],
                            [P[l]['g_ff_post']], [(D_MODEL, f32)], ROW_TILE)
    sv = saved[-1]
    dx, loss_loc = final_loss("final_loss", sv['x1'], sv['y2'], P[-1]['g_ff_post'], loss_target[0])
    loss = lax.psum(loss_loc[0, 0], ("x", "y", "c"))

    G = {}
    dlb_rows = []
    for l in reversed(range(DEPTH)):
        dx, gr, dlb = _layer_bwd(l, dx, saved[l], P[l], lb_all)
        G[l] = gr
        dlb_rows.append(dlb)
    dlb_all = jnp.concatenate(dlb_rows[::-1], axis=0)
    (g_lower_bounds,), _ = rowwise_bwd("lower_bounds_bwd", fn_lb, [(lower_bounds, GW, 0)], [], [(dlb_all, GW, 0)],
                                       [f32], DEPTH)
    grad_x = dx[None]

    def stack(name, f=lambda a: a):
        return jnp.stack([f(G[l][name]) for l in range(DEPTH)], axis=0)

    full = {
        'lower_bounds': g_lower_bounds,
        'norm_mix_pre': stack('norm_mix_pre', lambda a: a[0]), 'norm_mix_post': stack('norm_mix_post', lambda a: a[0]),
        'norm_ff_pre': stack('norm_ff_pre', lambda a: a[0]), 'norm_ff_post': stack('norm_ff_post', lambda a: a[0]),
        'hgrn_norm_w': stack('hgrn_norm_w', lambda a: a[0]), 'gdn_a_log': stack('gdn_a_log'), 'gdn_dt_bias': stack('gdn_dt_bias'),
        'gdn_norm_w': stack('gdn_norm_w', lambda a: a[0]), 'gmlp_ln_w': stack('gmlp_ln_w', lambda a: a[0]),
        'gmlp_ln_b': stack('gmlp_ln_b', lambda a: a[0]),
        'gmlp_w_s': stack('gmlp_w_s', lambda a: a.reshape(NH, MIX_CHUNK, MIX_CHUNK)), 'gmlp_b_s': stack('gmlp_b_s'),
        'conv_dw_b': stack('conv_dw_b', lambda a: a[0]), 'conv_ln_w': stack('conv_ln_w', lambda a: a[0]),
        'conv_ln_b': stack('conv_ln_b', lambda a: a[0]),
        'gdn_conv_w': stack('gdn_conv_w'), 'conv_dw_w': stack('conv_dw_w'),
    }

    small_names = list(REPLICATED) + ['gdn_conv_w', 'conv_dw_w']
    flat = jnp.concatenate([full[n].reshape(-1) for n in small_names])
    n_small = flat.shape[0]
    n_pad = -(-n_small // 1024) * 1024
    packed = jnp.pad(flat, (0, n_pad - n_small)).reshape(n_pad // 128, 128)

    def slots_in(l):
        gp = G[l]['w_in']
        gl = jnp.concatenate([gp[:, :8 * GW], gp[:, 12 * GW:12 * GW + 2 * NH], gp[:, 8 * GW:12 * GW]], axis=1)
        return jnp.moveaxis(gl.reshape(D_MODEL, N_DEV, D_IN // N_DEV), 1, 0)

    send = []
    for l in range(DEPTH):
        send += [slots_in(l), G[l]['w_out'].reshape(N_DEV, D_MODEL // N_DEV, D_MODEL),
                 jnp.moveaxis(G[l]['w_ff1'].reshape(D_MODEL, N_DEV, D_FF // N_DEV), 1, 0),
                 G[l]['w_ff2'].reshape(N_DEV, D_FF // N_DEV, D_MODEL)]
    (small_slots,) = exchange("gather_small_grads", [packed], True)
    recv = exchange("scatter_grads", send, False)

    out = {}
    for j, name in enumerate(('w_in', 'w_out', 'w_ff1', 'w_ff2')):
        out[name] = adamw(f"adamw_{name}", W[name], M[name], V[name], [recv[4 * l + j] for l in range(DEPTH)])

    def pack(d, fill):
        parts = [d[n].reshape(-1) for n in REPLICATED]
        parts.append(jnp.full((n_pad - sum(a.shape[0] for a in parts),), fill, f32))
        return jnp.concatenate(parts).reshape(1, n_pad // 128, 128)

    sm = adamw("adamw_small", pack(W, 0.0), pack(M, 0.0), pack(V, 1.0), [small_slots], tr=n_pad // 128)
    off = 0
    for n in REPLICATED:
        sz = W[n].size
        out[n] = tuple(a.reshape(-1)[off:off + sz].reshape(W[n].shape) for a in sm)
        off += sz
    gsum = sm[0].reshape(-1)
    for n, full_shape in (('gdn_conv_w', (DEPTH, SHORT_CONV, 3 * GW)), ('conv_dw_w', (DEPTH, CONV_WIDTH, GW))):
        sz = math.prod(full_shape)
        gfull = gsum[off:off + sz].reshape(full_shape)
        off += sz
        sh = W[n].shape
        gmine = lax.dynamic_slice_in_dim(gfull, me * sh[2], sh[2], axis=2)
        r = adamw(f"adamw_{n}", W[n].reshape(1, sh[0] * sh[1], sh[2]), M[n].reshape(1, sh[0] * sh[1], sh[2]),
                  V[n].reshape(1, sh[0] * sh[1], sh[2]), [gmine.reshape(1, sh[0] * sh[1], sh[2])], tr=sh[0] * sh[1])
        out[n] = tuple(a.reshape(sh) for a in r)

    return (loss, grad_x, *[out[n][0] for n in WEIGHTS], *[out[n][1] for n in WEIGHTS],
            *[out[n][2] for n in WEIGHTS], *[out[n][3] for n in WEIGHTS])
```

```python
import functools
import math

import jax
import jax.numpy as jnp
from jax import lax
from jax.experimental import pallas as pl
from jax.experimental.pallas import tpu as pltpu

f32 = jnp.float32
bf16 = jnp.bfloat16
HI = lax.Precision.HIGHEST

N_DEV = 8
DEPTH = 2
D_MODEL = 2048
GW = 512
HD = 128
NH = 4
CHUNK = 64
MIX_CHUNK = 128
CONV_WIDTH = 31
SHORT_CONV = 4
D_FF = 4 * D_MODEL
D_IN = 12 * GW + 2 * NH
D_IN_PAD = 12 * GW + 128
ROW_TILE = 256
EPS = 1e-6
TINY = 1e-30
ADAM_LR, ADAM_B1, ADAM_B2, ADAM_EPS, ADAM_WD, ADAM_STEP = 0.001, 0.9, 0.999, 1e-08, 0.01, 10
MESH = pl.DeviceIdType.MESH


def _dotb(a, b, ca, cb):
    return lax.dot_general(a.astype(bf16), b.astype(bf16), (((ca,), (cb,)), ((), ())),
                           preferred_element_type=f32)


@jax.custom_vjp
def mm(a, b):
    return _dotb(a, b, 1, 0)


def _mm_f(a, b):
    return mm(a, b), (a, b)


def _mm_b(res, ct):
    a, b = res
    return _dotb(ct, b, 1, 1), _dotb(a, ct, 0, 0)


mm.defvjp(_mm_f, _mm_b)


@jax.custom_vjp
def mm_nt(a, b):
    return _dotb(a, b, 1, 1)


def _mmnt_f(a, b):
    return mm_nt(a, b), (a, b)


def _mmnt_b(res, ct):
    a, b = res
    return _dotb(ct, b, 1, 0), _dotb(ct, a, 0, 0)


mm_nt.defvjp(_mmnt_f, _mmnt_b)


@jax.custom_vjp
def mm_tn(a, b):
    return _dotb(a, b, 0, 0)


def _mmtn_f(a, b):
    return mm_tn(a, b), (a, b)


def _mmtn_b(res, ct):
    a, b = res
    return _dotb(b, ct, 1, 1), _dotb(a, ct, 1, 0)


mm_tn.defvjp(_mmtn_f, _mmtn_b)


def mmh(a, b):
    return jnp.dot(a, b, precision=HI, preferred_element_type=f32)


def _rms(x, w):
    return x * lax.rsqrt(jnp.mean(x * x, axis=-1, keepdims=True) + EPS) * w


def _ln(x, w, b):
    mu = jnp.mean(x, axis=-1, keepdims=True)
    xc = x - mu
    var = jnp.mean(xc * xc, axis=-1, keepdims=True)
    return xc * lax.rsqrt(var + EPS) * w + b


def _gelu(x):
    return 0.5 * x * (1.0 + lax.erf(x * (2.0 ** -0.5)))


def _iota2(n, m, axis):
    return lax.broadcasted_iota(jnp.int32, (n, m), axis)


def _tri(n, strict=False):
    r, c = _iota2(n, n, 0), _iota2(n, n, 1)
    return (r > c) if strict else (r >= c)


def _eye(n):
    return (_iota2(n, n, 0) == _iota2(n, n, 1)).astype(f32)


def fn_norm(x, g):
    return (_rms(x, g),)


def fn_lb(lower_bounds):
    s = jax.nn.softmax(lower_bounds, axis=0)
    rows, cum = [], None
    for i in range(DEPTH):
        cum = s[i:i + 1] if cum is None else cum + s[i:i + 1]
        rows.append(cum - s[0:1])
    return (jnp.concatenate(rows, axis=0),)


def fn_hgrn(aq, af, ai, ag, lb, nw, st):
    c = aq.shape[0]
    sig = jax.nn.sigmoid(af)
    f = lb + (1.0 - lb) * sig
    logf = jnp.log(jnp.maximum(f, TINY))
    k = (1.0 - lb) * jax.nn.sigmoid(-af)
    q = jax.nn.silu(aq)
    v = ai
    b = mmh(_tri(c).astype(f32), logf)
    rel = b[:, None, :] - b[None, :, :]
    dec = jnp.exp(jnp.minimum(rel, 0.0)) * q[:, None, :] * k[None, :, :]
    scores = jnp.where(_tri(c), jnp.sum(dec, axis=-1), 0.0)
    b_end = b[c - 1:c, :]
    out = mm(scores, v) + mm_nt(q * jnp.exp(b), st)
    st_new = st * jnp.exp(b_end) + mm_tn(v, k * jnp.exp(b_end - b))
    o = _rms(out, nw) * jax.nn.silu(ag)
    return o, st_new


def fn_gdn_pre(tq, tk, tv, bq, bk, bv, p8, conv_w, alog, dtb):
    tile = bq.shape[0]
    h = tq.shape[0]
    outs = []
    for seg, (tl, cur) in enumerate(((tq, bq), (tk, bk), (tv, bv))):
        xe = jnp.concatenate([tl, cur], axis=0)
        acc = None
        for kk in range(SHORT_CONV):
            off = h - (SHORT_CONV - 1) + kk
            term = conv_w[kk:kk + 1, seg * GW:(seg + 1) * GW] * xe[off:off + tile, :]
            acc = term if acc is None else acc + term
        outs.append(jax.nn.silu(acc))
    sq, sk, sv = outs
    qh, kh = [], []
    for hh in range(NH):
        a = sq[:, hh * HD:(hh + 1) * HD]
        qh.append(a * lax.rsqrt(jnp.sum(a * a, axis=-1, keepdims=True) + EPS) * (HD ** -0.5))
        a = sk[:, hh * HD:(hh + 1) * HD]
        kh.append(a * lax.rsqrt(jnp.sum(a * a, axis=-1, keepdims=True) + EPS))
    q = jnp.concatenate(qh, axis=1)
    k = jnp.concatenate(kh, axis=1)
    beta = jax.nn.sigmoid(p8)
    g = -jnp.exp(alog) * jax.nn.softplus(p8 + dtb)
    r, cc = _iota2(128, GW, 0), _iota2(128, GW, 1) // HD
    e_beta = (r == cc).astype(f32)
    e_g = (r == cc + NH).astype(f32)
    return q, k, sv, mmh(beta, e_beta), mmh(g, e_g)


def fn_gdn(q, k, v, beta, g, z, nw, s):
    c = q.shape[0]
    gc = mmh(_tri(c).astype(f32), g)
    gcol = gc[:, 0:1]
    grow = jnp.sum(gcol * _eye(c), axis=0, keepdims=True)
    gamma = jnp.where(_tri(c), jnp.exp(jnp.minimum(gcol - grow, 0.0)), 0.0)
    kb = k * beta
    m = jnp.where(_tri(c, strict=True), mm_nt(kb, k) * gamma, 0.0)
    eye = _eye(c)
    tinv = eye - m
    p = m
    for _ in range(int(math.log2(c)) - 1):
        p = mmh(p, p)
        tinv = mmh(tinv, eye + p)
    egc = jnp.exp(gc)
    u = mmh(tinv, v * beta)
    w = mmh(tinv, kb * egc)
    qk = mm_nt(q, k) * gamma
    gc_end = gc[c - 1:c, :]
    q_dec = q * egc
    k_dec = k * jnp.exp(gc_end - gc)
    v_new = u - mm(w, s)
    out = mm(q_dec, s) + mm(qk, v_new)
    s_new = s * jnp.exp(gc_end) + mm_tn(k_dec, v_new)
    o = _rms(out, nw) * jax.nn.silu(z)
    return o, s_new


def fn_gmlp(cu, cv, ln_w, ln_b, w_s, b_s):
    n = cu.shape[0]
    ug = _gelu(cu)
    vn = _ln(_gelu(cv), ln_w, ln_b)
    eye = _eye(n)
    cols = []
    for hh in range(NH):
        wc = jnp.where(_tri(n), w_s[hh * n:(hh + 1) * n, :], 0.0)
        bcol = jnp.sum(b_s[hh:hh + 1, :] * eye, axis=1, keepdims=True)
        cols.append(mm(wc, vn[:, hh * HD:(hh + 1) * HD]) + bcol)
    return (ug * jnp.concatenate(cols, axis=1),)


def fn_conv(ta, tg, a, gate, dw_w, dw_b, ln_w, ln_b):
    tile = a.shape[0]
    h = ta.shape[0]
    ya = jnp.concatenate([ta, a], axis=0)
    yg = jnp.concatenate([tg, gate], axis=0)
    y = ya * jax.nn.sigmoid(yg)
    acc = None
    for kk in range(CONV_WIDTH):
        off = h - (CONV_WIDTH - 1) + kk
        term = dw_w[kk:kk + 1, :] * y[off:off + tile, :]
        acc = term if acc is None else acc + term
    return (jax.nn.silu(_ln(acc + dw_b, ln_w, ln_b)),)


def _full_spec(arr):
    nd = arr.ndim
    return pl.BlockSpec(arr.shape, lambda *_: (0,) * nd)


def rowwise(name, fn, tiled, params, outs, tile):
    t = tiled[0][0].shape[0]
    tile = min(tile, t)
    nt, np_ = len(tiled), len(params)

    def body(*refs):
        vals = [r[...].astype(f32) for r in refs[:nt + np_]]
        res = fn(*vals)
        for o_ref, r in zip(refs[nt + np_:], res):
            o_ref[...] = r.astype(o_ref.dtype)

    in_specs = [pl.BlockSpec((tile, w), lambda i, c=c: (i, c)) for _, w, c in tiled]
    in_specs += [_full_spec(p) for p in params]
    out_specs = [pl.BlockSpec((tile, w), lambda i: (i, 0)) for w, _ in outs]
    out_shape = [jax.ShapeDtypeStruct((t, w), dt) for w, dt in outs]
    return pl.pallas_call(body, grid=(t // tile,), in_specs=in_specs, out_specs=out_specs,
                          out_shape=out_shape, name=name)(*[a for a, _, _ in tiled], *params)


def rowwise_bwd(name, fn, tiled, params, cots, gouts, tile, addto=None):
    t = tiled[0][0].shape[0]
    tile = min(tile, t)
    nt, np_, nc = len(tiled), len(params), len(cots)
    na = 0 if addto is None else 1
    gidx = [i for i, g in enumerate(gouts) if g is not None]

    def body(*refs):
        i = pl.program_id(0)
        vals = [r[...].astype(f32) for r in refs[:nt + np_]]
        cvals = tuple(r[...].astype(f32) for r in refs[nt + np_:nt + np_ + nc])
        _, vjp = jax.vjp(fn, *vals)
        grads = vjp(cvals)
        orefs = refs[nt + np_ + nc + na:]
        for n, j in enumerate(gidx):
            g = grads[j]
            if na and n == 0:
                g = g + refs[nt + np_ + nc][...].astype(f32)
            orefs[n][...] = g.astype(orefs[n].dtype)
        prefs = orefs[len(gidx):]

        @pl.when(i == 0)
        def _():
            for r in prefs:
                r[...] = jnp.zeros_like(r)

        for r, g in zip(prefs, grads[nt:]):
            r[...] += g

    in_specs = [pl.BlockSpec((tile, w), lambda i, c=c: (i, c)) for _, w, c in tiled]
    in_specs += [_full_spec(p) for p in params]
    in_specs += [pl.BlockSpec((tile, w), lambda i, c=c: (i, c)) for _, w, c in cots]
    args = [a for a, _, _ in tiled] + list(params) + [a for a, _, _ in cots]
    if na:
        in_specs.append(pl.BlockSpec((tile, addto[1]), lambda i, c=addto[2]: (i, c)))
        args.append(addto[0])
    out_specs = [pl.BlockSpec((tile, tiled[j][1]), lambda i: (i, 0)) for j in gidx]
    out_shape = [jax.ShapeDtypeStruct((t, tiled[j][1]), gouts[j]) for j in gidx]
    out_specs += [_full_spec(p) for p in params]
    out_shape += [jax.ShapeDtypeStruct(p.shape, f32) for p in params]
    res = pl.pallas_call(body, grid=(t // tile,), in_specs=in_specs, out_specs=out_specs,
                         out_shape=out_shape, name=name)(*args)
    return res[:len(gidx)], res[len(gidx):]


def halo_fwd(name, fn, tiled, params, outs, tile, halo):
    t = tiled[0][0].shape[0]
    tile = min(tile, t)
    hal = [j for j, x in enumerate(tiled) if x[3]]
    nt, nh, np_ = len(tiled), len(hal), len(params)
    per = tile // halo

    def body(*refs):
        i = pl.program_id(0)
        first = (i > 0).astype(f32)
        tails = [r[...].astype(f32) * first for r in refs[:nh]]
        vals = [r[...].astype(f32) for r in refs[nh:nh + nt + np_]]
        res = fn(*tails, *vals)
        for o_ref, r in zip(refs[nh + nt + np_:], res):
            o_ref[...] = r.astype(o_ref.dtype)

    in_specs = [pl.BlockSpec((halo, tiled[j][1]), lambda i, c=tiled[j][2]: (jnp.maximum(i * per - 1, 0), c))
                for j in hal]
    in_specs += [pl.BlockSpec((tile, w), lambda i, c=c: (i, c)) for _, w, c, _ in tiled]
    in_specs += [_full_spec(p) for p in params]
    out_specs = [pl.BlockSpec((tile, w), lambda i: (i, 0)) for w, _ in outs]
    out_shape = [jax.ShapeDtypeStruct((t, w), dt) for w, dt in outs]
    args = [tiled[j][0] for j in hal] + [x[0] for x in tiled] + list(params)
    return pl.pallas_call(body, grid=(t // tile,), in_specs=in_specs, out_specs=out_specs,
                          out_shape=out_shape, name=name)(*args)


def halo_bwd(name, fn, tiled, params, cots, gdtype, tile, halo):
    t = tiled[0][0].shape[0]
    tile = min(tile, t)
    hal = [j for j, x in enumerate(tiled) if x[3]]
    nt, nh, np_, nc = len(tiled), len(hal), len(params), len(cots)
    per = tile // halo
    n_tiles = t // tile

    def body(*refs):
        s = pl.program_id(0)
        i = n_tiles - 1 - s
        first = (i > 0).astype(f32)
        tails = [r[...].astype(f32) * first for r in refs[:nh]]
        vals = [r[...].astype(f32) for r in refs[nh:nh + nt + np_]]
        cvals = tuple(r[...].astype(f32) for r in refs[nh + nt + np_:nh + nt + np_ + nc])
        n_in = nh + nt + np_ + nc
        orefs = refs[n_in:n_in + nt]
        prefs = refs[n_in + nt:n_in + nt + np_]
        carries = refs[n_in + nt + np_:]

        @pl.when(s == 0)
        def _():
            for r in prefs:
                r[...] = jnp.zeros_like(r)
            for r in carries:
                r[...] = jnp.zeros_like(r)

        _, vjp = jax.vjp(fn, *tails, *vals)
        grads = vjp(cvals)
        for j in range(nt):
            g = grads[nh + j]
            if j in hal:
                cr = carries[hal.index(j)]
                g = jnp.concatenate([g[:tile - halo], g[tile - halo:] + cr[...]], axis=0)
            orefs[j][...] = g.astype(orefs[j].dtype)
        for n in range(nh):
            carries[n][...] = grads[n] * first
        for r, g in zip(prefs, grads[nh + nt:]):
            r[...] += g

    rev = lambda s: n_tiles - 1 - s
    in_specs = [pl.BlockSpec((halo, tiled[j][1]),
                             lambda s, c=tiled[j][2]: (jnp.maximum(rev(s) * per - 1, 0), c)) for j in hal]
    in_specs += [pl.BlockSpec((tile, w), lambda s, c=c: (rev(s), c)) for _, w, c, _ in tiled]
    in_specs += [_full_spec(p) for p in params]
    in_specs += [pl.BlockSpec((tile, w), lambda s, c=c: (rev(s), c)) for _, w, c in cots]
    out_specs = [pl.BlockSpec((tile, w), lambda s: (rev(s), 0)) for _, w, _, _ in tiled]
    out_shape = [jax.ShapeDtypeStruct((t, w), gdtype) for _, w, _, _ in tiled]
    out_specs += [_full_spec(p) for p in params]
    out_shape += [jax.ShapeDtypeStruct(p.shape, f32) for p in params]
    scratch = [pltpu.VMEM((halo, tiled[j][1]), f32) for j in hal]
    args = [tiled[j][0] for j in hal] + [x[0] for x in tiled] + list(params) + [a for a, _, _ in cots]
    res = pl.pallas_call(body, grid=(n_tiles,), in_specs=in_specs, out_specs=out_specs,
                         out_shape=out_shape, scratch_shapes=scratch, name=name)(*args)
    return res[:nt], res[nt:]


def scan_fwd(name, fn, tiled, pparams, sparams, out_dtype, hb):
    t = tiled[0][0].shape[0]
    n = t // CHUNK
    nt, npp, nsp = len(tiled), len(pparams), len(sparams)
    w = HD * hb

    def body(*refs):
        c = pl.program_id(1)
        n_in = nt + npp + nsp
        o_ref, sv_ref, st = refs[n_in], refs[n_in + 1], refs[n_in + 2]

        @pl.when(c == 0)
        def _():
            st[...] = jnp.zeros_like(st)

        vals = [r[...].astype(f32) for r in refs[:n_in]]
        outs = []
        for hh in range(hb):
            hv = [v[:, hh * HD:(hh + 1) * HD] for v in vals[:nt + npp]] + vals[nt + npp:]
            s_in = st[hh]
            sv_ref[hh] = s_in
            o, s_new = fn(*hv, s_in)
            st[hh] = s_new
            outs.append(o)
        o_ref[...] = (outs[0] if hb == 1 else jnp.concatenate(outs, axis=1)).astype(o_ref.dtype)

    in_specs = [pl.BlockSpec((CHUNK, w), lambda g, c, b=b: (c, b // hb + g)) for _, b in tiled]
    in_specs += [pl.BlockSpec((1, w), lambda g, c: (0, g)) for _ in pparams]
    in_specs += [_full_spec(p) for p in sparams]
    out_specs = [pl.BlockSpec((CHUNK, w), lambda g, c: (c, g)),
                 pl.BlockSpec((hb, None, HD, HD), lambda g, c: (g, c, 0, 0))]
    out_shape = [jax.ShapeDtypeStruct((t, GW), out_dtype), jax.ShapeDtypeStruct((NH, n, HD, HD), f32)]
    return pl.pallas_call(body, grid=(NH // hb, n), in_specs=in_specs, out_specs=out_specs, out_shape=out_shape,
                          scratch_shapes=[pltpu.VMEM((hb, HD, HD), f32)], name=name)(
        *[a for a, _ in tiled], *pparams, *sparams)


def scan_bwd(name, fn, tiled, pparams, sparams, states, cot, gdtypes, hb):
    t = tiled[0][0].shape[0]
    n = t // CHUNK
    nt, npp, nsp = len(tiled), len(pparams), len(sparams)
    w = HD * hb

    def body(*refs):
        g, s = pl.program_id(0), pl.program_id(1)
        n_in = nt + npp + nsp
        vals = [r[...].astype(f32) for r in refs[:n_in]]
        st_ref = refs[n_in]
        do = refs[n_in + 1][...].astype(f32)
        orefs = refs[n_in + 2:n_in + 2 + nt]
        pprefs = refs[n_in + 2 + nt:n_in + 2 + nt + npp]
        sprefs = refs[n_in + 2 + nt + npp:n_in + 2 + nt + npp + nsp]
        ds = refs[n_in + 2 + nt + npp + nsp]

        @pl.when(s == 0)
        def _():
            ds[...] = jnp.zeros_like(ds)
            for r in pprefs:
                r[...] = jnp.zeros_like(r)

        @pl.when((s == 0) & (g == 0))
        def _():
            for r in sprefs:
                r[...] = jnp.zeros_like(r)

        per_head = []
        for hh in range(hb):
            sl = slice(hh * HD, (hh + 1) * HD)
            hv = [v[:, sl] for v in vals[:nt + npp]] + vals[nt + npp:]
            _, vjp = jax.vjp(fn, *hv, st_ref[hh])
            grads = vjp((do[:, sl], ds[hh]))
            ds[hh] = grads[n_in]
            per_head.append(grads)
        cat = lambda j: per_head[0][j] if hb == 1 else jnp.concatenate([gr[j] for gr in per_head], axis=1)
        for j in range(nt):
            orefs[j][...] = cat(j).astype(orefs[j].dtype)
        for j, r in enumerate(pprefs):
            r[...] += cat(nt + j)
        for j, r in enumerate(sprefs):
            tot = per_head[0][nt + npp + j]
            for gr in per_head[1:]:
                tot = tot + gr[nt + npp + j]
            r[...] += tot

    rev = lambda s: n - 1 - s
    in_specs = [pl.BlockSpec((CHUNK, w), lambda g, s, b=b: (rev(s), b // hb + g)) for _, b in tiled]
    in_specs += [pl.BlockSpec((1, w), lambda g, s: (0, g)) for _ in pparams]
    in_specs += [_full_spec(p) for p in sparams]
    in_specs += [pl.BlockSpec((hb, None, HD, HD), lambda g, s: (g, rev(s), 0, 0)),
                 pl.BlockSpec((CHUNK, w), lambda g, s, b=cot[1]: (rev(s), b // hb + g))]
    out_specs = [pl.BlockSpec((CHUNK, w), lambda g, s: (rev(s), g)) for _ in tiled]
    out_shape = [jax.ShapeDtypeStruct((t, GW), dt) for dt in gdtypes]
    out_specs += [pl.BlockSpec((1, w), lambda g, s: (0, g)) for _ in pparams]
    out_shape += [jax.ShapeDtypeStruct(p.shape, f32) for p in pparams]
    out_specs += [_full_spec(p) for p in sparams]
    out_shape += [jax.ShapeDtypeStruct(p.shape, f32) for p in sparams]
    res = pl.pallas_call(body, grid=(NH // hb, n), in_specs=in_specs, out_specs=out_specs, out_shape=out_shape,
                         scratch_shapes=[pltpu.VMEM((hb, HD, HD), f32)], name=name)(
        *[a for a, _ in tiled], *pparams, *sparams, states, cot[0])
    return res[:nt], res[nt:nt + npp], res[nt + npp:]


def matmul(name, a, b, mode, out_dtype=f32, tm=1024, tn=1024, tk=2048, epilogue=None, extra=None, slots=None):
    if mode == 'nn':
        (m, k), n = a.shape, b.shape[1]
    elif mode == 'nt':
        (m, k), n = a.shape, b.shape[0]
    else:
        (k, m), n = a.shape, b.shape[1]
    tm, tn, tk = min(tm, m), min(tn, n), min(tk, k)
    if slots == 'rows':
        tm = min(tm, m // N_DEV)
    if slots == 'cols':
        tn = min(tn, n // N_DEV)
    nk = k // tk
    ca, cb = {'nn': (1, 0), 'nt': (1, 1), 'tn': (0, 0)}[mode]

    def finish(refs, r):
        if epilogue == 'relu2':
            refs[2][...] = r
            refs[3][...] = jnp.square(jnp.maximum(r, 0.0)).astype(bf16)
        elif epilogue == 'relu2_bwd':
            refs[3][...] = (r * 2.0 * jnp.maximum(refs[2][...], 0.0)).astype(refs[3].dtype)
        else:
            refs[2][...] = r.astype(refs[2].dtype)

    def body(*refs):
        part = _dotb(refs[0][...], refs[1][...], ca, cb)
        if nk == 1:
            finish(refs, part)
            return
        acc = refs[-1]
        kk = pl.program_id(2)

        @pl.when(kk == 0)
        def _():
            acc[...] = part

        @pl.when(kk > 0)
        def _():
            acc[...] += part

        @pl.when(kk == nk - 1)
        def _():
            finish(refs, acc[...])

    if mode == 'nn':
        a_spec = pl.BlockSpec((tm, tk), lambda i, j, kk: (i, kk))
        b_spec = pl.BlockSpec((tk, tn), lambda i, j, kk: (kk, j))
    elif mode == 'nt':
        a_spec = pl.BlockSpec((tm, tk), lambda i, j, kk: (i, kk))
        b_spec = pl.BlockSpec((tn, tk), lambda i, j, kk: (j, kk))
    else:
        a_spec = pl.BlockSpec((tk, tm), lambda i, j, kk: (kk, i))
        b_spec = pl.BlockSpec((tk, tn), lambda i, j, kk: (kk, j))
    o_spec = pl.BlockSpec((tm, tn), lambda i, j, kk: (i, j))
    in_specs, args = [a_spec, b_spec], [a, b]
    if epilogue == 'relu2':
        out_specs = [o_spec, o_spec]
        out_shape = [jax.ShapeDtypeStruct((m, n), f32), jax.ShapeDtypeStruct((m, n), bf16)]
    elif slots == 'rows':
        per = (m // N_DEV) // tm
        out_specs = pl.BlockSpec((None, None, tm, tn), lambda i, j, kk: ((i // per) % 2, (i // per) // 2, i % per, j))
        out_shape = jax.ShapeDtypeStruct((2, N_DEV // 2, m // N_DEV, n), out_dtype)
    elif slots == 'cols':
        per = (n // N_DEV) // tn
        out_specs = pl.BlockSpec((None, None, tm, tn), lambda i, j, kk: ((j // per) % 2, (j // per) // 2, i, j % per))
        out_shape = jax.ShapeDtypeStruct((2, N_DEV // 2, m, n // N_DEV), out_dtype)
    else:
        out_specs, out_shape = o_spec, jax.ShapeDtypeStruct((m, n), out_dtype)
        if epilogue == 'relu2_bwd':
            in_specs.append(o_spec)
            args.append(extra)
    scratch = [pltpu.VMEM((tm, tn), f32)] if nk > 1 else []
    return pl.pallas_call(body, grid=(m // tm, n // tn, nk), in_specs=in_specs, out_specs=out_specs,
                          out_shape=out_shape, scratch_shapes=scratch, name=name)(*args)


def final_loss(name, x, y, g, target):
    t, d = x.shape
    tile = min(ROW_TILE, t)

    def body(x_ref, y_ref, g_ref, t_ref, dx_ref, l_ref):
        i = pl.program_id(0)

        @pl.when(i == 0)
        def _():
            l_ref[...] = jnp.zeros_like(l_ref)

        err = x_ref[...] + _rms(y_ref[...], g_ref[...]) - t_ref[...]
        dx_ref[...] = err * (1.0 / d)
        l_ref[...] += 0.5 * jnp.sum(jnp.mean(err * err, axis=-1, keepdims=True), axis=0, keepdims=True)

    row = pl.BlockSpec((tile, d), lambda i: (i, 0))
    return pl.pallas_call(
        body, grid=(t // tile,), in_specs=[row, row, _full_spec(g), row],
        out_specs=[row, pl.BlockSpec((1, 1), lambda i: (0, 0))],
        out_shape=[jax.ShapeDtypeStruct((t, d), f32), jax.ShapeDtypeStruct((1, 1), f32)], name=name)(x, y, g, target)


def adamw(name, w, m, v, gslots, tr=128):
    nl, r, c = w.shape
    tr = min(tr, r)
    nr = r // tr
    ns = gslots[0].shape[0]
    c1 = 1.0 / (1.0 - ADAM_B1 ** ADAM_STEP)
    c2 = 1.0 / (1.0 - ADAM_B2 ** ADAM_STEP)

    def body(*refs):
        w_ref, m_ref, v_ref = refs[:3]
        g_refs = refs[3:3 + nl]
        go_ref, d_ref, mo_ref, vo_ref = refs[3 + nl:]
        l = pl.program_id(0)
        g = None
        for li in range(nl):
            s = g_refs[li][0].astype(f32)
            for k in range(1, ns):
                s = s + g_refs[li][k].astype(f32)
            g = s if g is None else jnp.where(l == li, s, g)
        mn = ADAM_B1 * m_ref[...] + (1.0 - ADAM_B1) * g
        vn = ADAM_B2 * v_ref[...] + (1.0 - ADAM_B2) * jnp.square(g)
        go_ref[...] = g
        mo_ref[...] = mn
        vo_ref[...] = vn
        d_ref[...] = -ADAM_LR * ((mn * c1) / (jnp.sqrt(vn * c2) + ADAM_EPS) + ADAM_WD * w_ref[...])

    blk = pl.BlockSpec((None, tr, c), lambda l, i: (l, i, 0))

    def gspec(li):
        return pl.BlockSpec((ns, tr, c), lambda l, i: (0, jnp.where(l == li, i, jnp.where(l < li, 0, nr - 1)), 0))

    return pl.pallas_call(
        body, grid=(nl, nr), in_specs=[blk, blk, blk] + [gspec(li) for li in range(nl)],
        out_specs=[blk] * 4, out_shape=[jax.ShapeDtypeStruct(w.shape, f32)] * 4, name=name)(w, m, v, *gslots)


def exchange(name, arrays, gather, group):
    n = len(arrays)
    ns = {'all': 8, 'chips': 4, 'core': 2}[group]

    def body(*refs):
        ins, outs = refs[:n], refs[n:2 * n]
        send_sems, recv_sems, loc_sems = refs[2 * n:]
        x, y, c = lax.axis_index("x"), lax.axis_index("y"), lax.axis_index("c")

        def member(k):
            if group == 'all':
                px, py, pc = x ^ ((k >> 2) & 1), y ^ ((k >> 1) & 1), c ^ (k & 1)
                return (px, py, pc), 4 * px + 2 * py + pc
            if group == 'chips':
                px, py = x ^ ((k >> 1) & 1), y ^ (k & 1)
                return (px, py, c), 2 * px + py
            return (x, y, c ^ k), c ^ k

        _, me = member(0)
        sends, recvs, locs = [], [], []
        for a in range(n):
            lc = pltpu.make_async_copy(ins[a] if gather else ins[a].at[me], outs[a].at[me], loc_sems.at[a])
            lc.start()
            locs.append(lc)
            for k in range(1, ns):
                dev, peer = member(k)
                src = ins[a] if gather else ins[a].at[peer]
                cp = pltpu.make_async_remote_copy(src_ref=src, dst_ref=outs[a].at[me], send_sem=send_sems.at[a, k],
                                                  recv_sem=recv_sems.at[a, k], device_id=dev, device_id_type=MESH)
                cp.start()
                sends.append(cp)
                recvs.append(pltpu.make_async_remote_copy(src_ref=src, dst_ref=outs[a].at[peer], send_sem=send_sems.at[a, k],
                                                          recv_sem=recv_sems.at[a, k], device_id=dev, device_id_type=MESH))
        for cp in recvs:
            cp.wait_recv()
        for cp in sends:
            cp.wait_send()
        for lc in locs:
            lc.wait()

    anyspec = pl.BlockSpec(memory_space=pl.ANY)
    out_shape = [jax.ShapeDtypeStruct(((ns,) + a.shape) if gather else a.shape, a.dtype) for a in arrays]
    return pl.pallas_call(
        body, in_specs=[anyspec] * n, out_specs=[anyspec] * n, out_shape=out_shape,
        scratch_shapes=[pltpu.SemaphoreType.DMA((n, ns)), pltpu.SemaphoreType.DMA((n, ns)),
                        pltpu.SemaphoreType.DMA((n,))], name=name)(*arrays)


def chip_sum(name, pair):
    _, nc, r, c = pair.shape
    tr = min(ROW_TILE, r)

    def body(p_ref, o_ref):
        o_ref[...] = (p_ref[0].astype(f32) + p_ref[1].astype(f32)).astype(o_ref.dtype)

    return pl.pallas_call(
        body, grid=(nc, r // tr), in_specs=[pl.BlockSpec((2, None, tr, c), lambda s, i: (0, s, i, 0))],
        out_specs=pl.BlockSpec((None, tr, c), lambda s, i: (s, i, 0)),
        out_shape=jax.ShapeDtypeStruct((nc, r, c), pair.dtype), name=name)(pair)


REPLICATED = ('lower_bounds', 'norm_mix_pre', 'norm_mix_post', 'norm_ff_pre', 'norm_ff_post', 'hgrn_norm_w',
              'gdn_a_log', 'gdn_dt_bias', 'gdn_norm_w', 'gmlp_ln_w', 'gmlp_ln_b', 'gmlp_w_s', 'gmlp_b_s',
              'conv_dw_b', 'conv_ln_w', 'conv_ln_b')
WEIGHTS = ('lower_bounds', 'norm_mix_pre', 'norm_mix_post', 'norm_ff_pre', 'norm_ff_post', 'w_in', 'w_out',
           'hgrn_norm_w', 'gdn_conv_w', 'gdn_a_log', 'gdn_dt_bias', 'gdn_norm_w', 'gmlp_ln_w', 'gmlp_ln_b',
           'gmlp_w_s', 'gmlp_b_s', 'conv_dw_w', 'conv_dw_b', 'conv_ln_w', 'conv_ln_b', 'w_ff1', 'w_ff2')


def _row(v):
    return v.reshape(1, -1)


def _pad_lanes(v, offset):
    return jnp.pad(v, (offset, 128 - offset - v.shape[0])).reshape(1, 128)


def _layer_fwd(l, x0, p, lb_all):
    t = x0.shape[0]
    sv = {'x0': x0}
    (h,) = rowwise(f"norm_mix_pre{l}", fn_norm, [(x0, D_MODEL, 0)], [p['g_mix_pre']], [(D_MODEL, bf16)], ROW_TILE)
    proj = matmul(f"proj{l}", h, p['w_in'], 'nn', f32, tn=896)
    sv.update(h=h, proj=proj)
    lb = lb_all[l:l + 1]
    o_a, st_a = scan_fwd(f"hgrn{l}", fn_hgrn, [(proj, 0), (proj, 4), (proj, 8), (proj, 12)], [lb], [p['hgrn_norm_w']], bf16, 1)
    q, k, v, beta, g = halo_fwd(
        f"gdn_pre{l}", fn_gdn_pre,
        [(proj, GW, 4, True), (proj, GW, 5, True), (proj, GW, 6, True), (proj, 128, 48, False)],
        [p['gdn_conv_w'], p['alog'], p['dtb']], [(GW, f32)] * 5, ROW_TILE, 8)
    o_b, st_b = scan_fwd(f"gdn{l}", fn_gdn, [(q, 0), (k, 0), (v, 0), (beta, 0), (g, 0), (proj, 28)], [],
                         [p['gdn_norm_w']], bf16, NH)
    (o_c,) = rowwise(f"gmlp{l}", fn_gmlp, [(proj, GW, 8), (proj, GW, 9)],
                     [p['gmlp_ln_w'], p['gmlp_ln_b'], p['gmlp_w_s'], p['gmlp_b_s']], [(GW, bf16)], MIX_CHUNK)
    (o_d,) = halo_fwd(f"conv{l}", fn_conv, [(proj, GW, 10, True), (proj, GW, 11, True)],
                      [p['conv_dw_w'], p['conv_dw_b'], p['conv_ln_w'], p['conv_ln_b']], [(GW, bf16)], ROW_TILE, 32)
    mix = jnp.concatenate([o_a, o_b, o_c, o_d], axis=1)
    y1 = matmul(f"out_proj{l}", mix, p['w_out'], 'nn', f32)
    (x1,) = rowwise(f"res_mix{l}", lambda x, y, gg: (x + _rms(y, gg),), [(x0, D_MODEL, 0), (y1, D_MODEL, 0)],
                    [p['g_mix_post']], [(D_MODEL, f32)], ROW_TILE)
    (h2,) = rowwise(f"norm_ff_pre{l}", fn_norm, [(x1, D_MODEL, 0)], [p['g_ff_pre']], [(D_MODEL, bf16)], ROW_TILE)
    u, a = matmul(f"ff1_{l}", h2, p['w_ff1'], 'nn', epilogue='relu2')
    y2 = matmul(f"ff2_{l}", a, p['w_ff2'], 'nn', f32)
    sv.update(st_a=st_a, q=q, k=k, v=v, beta=beta, g=g, st_b=st_b, mix=mix, y1=y1, x1=x1, h2=h2, u=u, a=a, y2=y2)
    return sv


def _layer_bwd(l, dx, sv, p, lb_all):
    gr = {}
    t = dx.shape[0]
    (dy2,), (gr['norm_ff_post'],) = rowwise_bwd(f"res_ff_bwd{l}", fn_norm, [(sv['y2'], D_MODEL, 0)], [p['g_ff_post']],
                                                [(dx, D_MODEL, 0)], [bf16], ROW_TILE)
    du = matmul(f"ff2_dx{l}", dy2, p['w_ff2'], 'nt', bf16, epilogue='relu2_bwd', extra=sv['u'])
    gr['w_ff2'] = matmul(f"ff2_dw{l}", sv['a'], dy2, 'tn', bf16, slots='rows')
    dh2 = matmul(f"ff1_dx{l}", du, p['w_ff1'], 'nt', f32)
    gr['w_ff1'] = matmul(f"ff1_dw{l}", sv['h2'], du, 'tn', bf16, slots='cols')
    (dx1,), (gr['norm_ff_pre'],) = rowwise_bwd(f"norm_ff_pre_bwd{l}", fn_norm, [(sv['x1'], D_MODEL, 0)], [p['g_ff_pre']],
                                               [(dh2, D_MODEL, 0)], [f32], ROW_TILE, addto=(dx, D_MODEL, 0))
    (dy1,), (gr['norm_mix_post'],) = rowwise_bwd(f"res_mix_bwd{l}", fn_norm, [(sv['y1'], D_MODEL, 0)], [p['g_mix_post']],
                                                 [(dx1, D_MODEL, 0)], [bf16], ROW_TILE)
    dmix = matmul(f"out_proj_dx{l}", dy1, p['w_out'], 'nt', f32)
    gr['w_out'] = matmul(f"out_proj_dw{l}", sv['mix'], dy1, 'tn', bf16, slots='rows')
    proj = sv['proj']
    lb = lb_all[l:l + 1]
    d_a, (dlb,), (gr['hgrn_norm_w'],) = scan_bwd(
        f"hgrn_bwd{l}", fn_hgrn, [(proj, 0), (proj, 4), (proj, 8), (proj, 12)], [lb], [p['hgrn_norm_w']],
        sv['st_a'], (dmix, 0), [bf16] * 4, 1)
    d_b, _, (gr['gdn_norm_w'],) = scan_bwd(
        f"gdn_bwd{l}", fn_gdn, [(sv['q'], 0), (sv['k'], 0), (sv['v'], 0), (sv['beta'], 0), (sv['g'], 0), (proj, 28)],
        [], [p['gdn_norm_w']], sv['st_b'], (dmix, 4), [f32] * 5 + [bf16], NH)
    d_bp, (gr['gdn_conv_w'], dalog, ddtb) = halo_bwd(
        f"gdn_pre_bwd{l}", fn_gdn_pre,
        [(proj, GW, 4, True), (proj, GW, 5, True), (proj, GW, 6, True), (proj, 128, 48, False)],
        [p['gdn_conv_w'], p['alog'], p['dtb']], [(d_b[j], GW, 0) for j in range(5)], bf16, ROW_TILE, 8)
    gr['gdn_a_log'] = dalog[0, NH:2 * NH]
    gr['gdn_dt_bias'] = ddtb[0, NH:2 * NH]
    d_c, (gr['gmlp_ln_w'], gr['gmlp_ln_b'], gr['gmlp_w_s'], gr['gmlp_b_s']) = rowwise_bwd(
        f"gmlp_bwd{l}", fn_gmlp, [(proj, GW, 8), (proj, GW, 9)],
        [p['gmlp_ln_w'], p['gmlp_ln_b'], p['gmlp_w_s'], p['gmlp_b_s']], [(dmix, GW, 2)], [bf16, bf16], MIX_CHUNK)
    d_d, (gr['conv_dw_w'], gr['conv_dw_b'], gr['conv_ln_w'], gr['conv_ln_b']) = halo_bwd(
        f"conv_bwd{l}", fn_conv, [(proj, GW, 10, True), (proj, GW, 11, True)],
        [p['conv_dw_w'], p['conv_dw_b'], p['conv_ln_w'], p['conv_ln_b']], [(dmix, GW, 3)], bf16, ROW_TILE, 32)
    dproj = jnp.concatenate(list(d_a) + [d_bp[0], d_bp[1], d_bp[2], d_b[5]] + list(d_c) + list(d_d) + [d_bp[3]], axis=1)
    dh = matmul(f"proj_dx{l}", dproj, p['w_in'], 'nt', f32, tk=896)
    gr['w_in'] = matmul(f"proj_dw{l}", sv['h'], dproj, 'tn', bf16, tn=896)
    (dx0,), (gr['norm_mix_pre'],) = rowwise_bwd(f"norm_mix_pre_bwd{l}", fn_norm, [(sv['x0'], D_MODEL, 0)], [p['g_mix_pre']],
                                                [(dh, D_MODEL, 0)], [f32], ROW_TILE, addto=(dx1, D_MODEL, 0))
    return dx0, gr, dlb


def kernel(x, lower_bounds, norm_mix_pre, norm_mix_post, norm_ff_pre, norm_ff_post, w_in, w_out, hgrn_norm_w, gdn_conv_w, gdn_a_log, gdn_dt_bias, gdn_norm_w, gmlp_ln_w, gmlp_ln_b, gmlp_w_s, gmlp_b_s, conv_dw_w, conv_dw_b, conv_ln_w, conv_ln_b, w_ff1, w_ff2, loss_target, m_lower_bounds, m_norm_mix_pre, m_norm_mix_post, m_norm_ff_pre, m_norm_ff_post, m_w_in, m_w_out, m_hgrn_norm_w, m_gdn_conv_w, m_gdn_a_log, m_gdn_dt_bias, m_gdn_norm_w, m_gmlp_ln_w, m_gmlp_ln_b, m_gmlp_w_s, m_gmlp_b_s, m_conv_dw_w, m_conv_dw_b, m_conv_ln_w, m_conv_ln_b, m_w_ff1, m_w_ff2, v_lower_bounds, v_norm_mix_pre, v_norm_mix_post, v_norm_ff_pre, v_norm_ff_post, v_w_in, v_w_out, v_hgrn_norm_w, v_gdn_conv_w, v_gdn_a_log, v_gdn_dt_bias, v_gdn_norm_w, v_gmlp_ln_w, v_gmlp_ln_b, v_gmlp_w_s, v_gmlp_b_s, v_conv_dw_w, v_conv_dw_b, v_conv_ln_w, v_conv_ln_b, v_w_ff1, v_w_ff2):
    loc = dict(locals())
    W = {n: loc[n] for n in WEIGHTS}
    M = {n: loc['m_' + n] for n in WEIGHTS}
    V = {n: loc['v_' + n] for n in WEIGHTS}
    t = x.shape[1]
    me = 4 * lax.axis_index("x") + 2 * lax.axis_index("y") + lax.axis_index("c")

    shards = [w_in.astype(bf16), w_out.astype(bf16), w_ff1.astype(bf16), w_ff2.astype(bf16), gdn_conv_w, conv_dw_w]
    by_chip = exchange("gather_weights_ici", shards, True, 'chips')
    by_core = exchange("gather_weights_d2d", by_chip, True, 'core')
    g_in, g_out, g_ff1, g_ff2, g_gconv, g_dconv = [
        jnp.swapaxes(a, 0, 1).reshape((N_DEV,) + a.shape[2:]) for a in by_core]
    w_in_full = jnp.moveaxis(g_in, 0, 2).reshape(DEPTH, D_MODEL, D_IN)
    w_in_pad = jnp.concatenate([w_in_full[:, :, :8 * GW], w_in_full[:, :, 8 * GW + 2 * NH:],
                                w_in_full[:, :, 8 * GW:8 * GW + 2 * NH],
                                jnp.zeros((DEPTH, D_MODEL, 128 - 2 * NH), bf16)], axis=2)
    w_out_full = jnp.moveaxis(g_out, 0, 1).reshape(DEPTH, D_MODEL, D_MODEL)
    w_ff1_full = jnp.moveaxis(g_ff1, 0, 2).reshape(DEPTH, D_MODEL, D_FF)
    w_ff2_full = jnp.moveaxis(g_ff2, 0, 1).reshape(DEPTH, D_FF, D_MODEL)
    gconv_full = jnp.moveaxis(g_gconv, 0, 2).reshape(DEPTH, SHORT_CONV, 3 * GW)
    dconv_full = jnp.moveaxis(g_dconv, 0, 2).reshape(DEPTH, CONV_WIDTH, GW)

    (lb_all,) = rowwise("lower_bounds", fn_lb, [(lower_bounds, GW, 0)], [], [(GW, f32)], DEPTH)

    P = []
    for l in range(DEPTH):
        P.append(dict(
            g_mix_pre=_row(norm_mix_pre[l]), g_mix_post=_row(norm_mix_post[l]), g_ff_pre=_row(norm_ff_pre[l]),
            g_ff_post=_row(norm_ff_post[l]), w_in=w_in_pad[l], w_out=w_out_full[l], w_ff1=w_ff1_full[l],
            w_ff2=w_ff2_full[l], hgrn_norm_w=_row(hgrn_norm_w[l]), gdn_conv_w=gconv_full[l],
            alog=_pad_lanes(gdn_a_log[l], NH), dtb=_pad_lanes(gdn_dt_bias[l], NH), gdn_norm_w=_row(gdn_norm_w[l]),
            gmlp_ln_w=_row(gmlp_ln_w[l]), gmlp_ln_b=_row(gmlp_ln_b[l]), gmlp_w_s=gmlp_w_s[l].reshape(NH * MIX_CHUNK, MIX_CHUNK),
            gmlp_b_s=gmlp_b_s[l], conv_dw_w=dconv_full[l], conv_dw_b=_row(conv_dw_b[l]), conv_ln_w=_row(conv_ln_w[l]),
            conv_ln_b=_row(conv_ln_b[l])))

    xs = x[0]
    saved = []
    for l in range(DEPTH):
        sv = _layer_fwd(l, xs, P[l], lb_all)
        saved.append(sv)
        if l < DEPTH - 1:
            (xs,) = rowwise(f"res_ff{l}", lambda a, y, gg: (a + _rms(y, gg),), [(sv['x1'], D_MODEL, 0), (sv['y2'], D_MODEL, 0)],
                            [P[l]['g_ff_post']], [(D_MODEL, f32)], ROW_TILE)
    sv = saved[-1]
    dx, loss_loc = final_loss("final_loss", sv['x1'], sv['y2'], P[-1]['g_ff_post'], loss_target[0])
    loss = lax.psum(loss_loc[0, 0], ("x", "y", "c"))

    G = {}
    dlb_rows = []
    for l in reversed(range(DEPTH)):
        dx, gr, dlb = _layer_bwd(l, dx, saved[l], P[l], lb_all)
        G[l] = gr
        dlb_rows.append(dlb)
    dlb_all = jnp.concatenate(dlb_rows[::-1], axis=0)
    (g_lower_bounds,), _ = rowwise_bwd("lower_bounds_bwd", fn_lb, [(lower_bounds, GW, 0)], [], [(dlb_all, GW, 0)],
                                       [f32], DEPTH)
    grad_x = dx[None]

    def stack(name, f=lambda a: a):
        return jnp.stack([f(G[l][name]) for l in range(DEPTH)], axis=0)

    full = {
        'lower_bounds': g_lower_bounds,
        'norm_mix_pre': stack('norm_mix_pre', lambda a: a[0]), 'norm_mix_post': stack('norm_mix_post', lambda a: a[0]),
        'norm_ff_pre': stack('norm_ff_pre', lambda a: a[0]), 'norm_ff_post': stack('norm_ff_post', lambda a: a[0]),
        'hgrn_norm_w': stack('hgrn_norm_w', lambda a: a[0]), 'gdn_a_log': stack('gdn_a_log'), 'gdn_dt_bias': stack('gdn_dt_bias'),
        'gdn_norm_w': stack('gdn_norm_w', lambda a: a[0]), 'gmlp_ln_w': stack('gmlp_ln_w', lambda a: a[0]),
        'gmlp_ln_b': stack('gmlp_ln_b', lambda a: a[0]),
        'gmlp_w_s': stack('gmlp_w_s', lambda a: a.reshape(NH, MIX_CHUNK, MIX_CHUNK)), 'gmlp_b_s': stack('gmlp_b_s'),
        'conv_dw_b': stack('conv_dw_b', lambda a: a[0]), 'conv_ln_w': stack('conv_ln_w', lambda a: a[0]),
        'conv_ln_b': stack('conv_ln_b', lambda a: a[0]),
        'gdn_conv_w': stack('gdn_conv_w'), 'conv_dw_w': stack('conv_dw_w'),
    }

    small_names = list(REPLICATED) + ['gdn_conv_w', 'conv_dw_w']
    flat = jnp.concatenate([full[n].reshape(-1) for n in small_names])
    n_small = flat.shape[0]
    n_pad = -(-n_small // 1024) * 1024
    packed = jnp.pad(flat, (0, n_pad - n_small)).reshape(n_pad // 128, 128)

    def slots_in(l):
        gp = G[l]['w_in']
        gl = jnp.concatenate([gp[:, :8 * GW], gp[:, 12 * GW:12 * GW + 2 * NH], gp[:, 8 * GW:12 * GW]], axis=1)
        return jnp.transpose(gl.reshape(D_MODEL, N_DEV // 2, 2, D_IN // N_DEV), (2, 1, 0, 3))

    send = []
    for l in range(DEPTH):
        send += [slots_in(l), G[l]['w_out'], G[l]['w_ff1'], G[l]['w_ff2']]
    (small_slots,) = exchange("gather_small_grads", [packed], True, 'all')
    pairs = exchange("scatter_grads_d2d", send, False, 'core')
    partial = [chip_sum(f"chip_sum{i}", pr) for i, pr in enumerate(pairs)]
    recv = exchange("scatter_grads_ici", partial, False, 'chips')

    out = {}
    for j, name in enumerate(('w_in', 'w_out', 'w_ff1', 'w_ff2')):
        out[name] = adamw(f"adamw_{name}", W[name], M[name], V[name], [recv[4 * l + j] for l in range(DEPTH)])

    def pack(d, fill):
        parts = [d[n].reshape(-1) for n in REPLICATED]
        parts.append(jnp.full((n_pad - sum(a.shape[0] for a in parts),), fill, f32))
        return jnp.concatenate(parts).reshape(1, n_pad // 128, 128)

    sm = adamw("adamw_small", pack(W, 0.0), pack(M, 0.0), pack(V, 1.0), [small_slots], tr=n_pad // 128)
    off = 0
    for n in REPLICATED:
        sz = W[n].size
        out[n] = tuple(a.reshape(-1)[off:off + sz].reshape(W[n].shape) for a in sm)
        off += sz
    gsum = sm[0].reshape(-1)
    for n, full_shape in (('gdn_conv_w', (DEPTH, SHORT_CONV, 3 * GW)), ('conv_dw_w', (DEPTH, CONV_WIDTH, GW))):
        sz = math.prod(full_shape)
        gfull = gsum[off:off + sz].reshape(full_shape)
        off += sz
        sh = W[n].shape
        gmine = lax.dynamic_slice_in_dim(gfull, me * sh[2], sh[2], axis=2)
        r = adamw(f"adamw_{n}", W[n].reshape(1, sh[0] * sh[1], sh[2]), M[n].reshape(1, sh[0] * sh[1], sh[2]),
                  V[n].reshape(1, sh[0] * sh[1], sh[2]), [gmine.reshape(1, sh[0] * sh[1], sh[2])], tr=sh[0] * sh[1])
        out[n] = tuple(a.reshape(sh) for a in r)

    return (loss, grad_x, *[out[n][0] for n in WEIGHTS], *[out[n][1] for n in WEIGHTS],
            *[out[n][2] for n in WEIGHTS], *[out[n][3] for n in WEIGHTS])
```

```python
import functools
import math

import jax
import jax.numpy as jnp
from jax import lax
from jax.experimental import pallas as pl
from jax.experimental.pallas import tpu as pltpu

f32 = jnp.float32
bf16 = jnp.bfloat16
HI = lax.Precision.HIGHEST

N_DEV = 8
DEPTH = 2
D_MODEL = 2048
GW = 512
HD = 128
NH = 4
CHUNK = 64
MIX_CHUNK = 128
CONV_WIDTH = 31
SHORT_CONV = 4
D_FF = 4 * D_MODEL
D_IN = 12 * GW + 2 * NH
D_IN_PAD = 12 * GW + 128
ROW_TILE = 256
EPS = 1e-6
TINY = 1e-30
ADAM_LR, ADAM_B1, ADAM_B2, ADAM_EPS, ADAM_WD, ADAM_STEP = 0.001, 0.9, 0.999, 1e-08, 0.01, 10
MESH = pl.DeviceIdType.MESH


def _dotb(a, b, ca, cb):
    return lax.dot_general(a.astype(bf16), b.astype(bf16), (((ca,), (cb,)), ((), ())),
                           preferred_element_type=f32)


@jax.custom_vjp
def mm(a, b):
    return _dotb(a, b, 1, 0)


def _mm_f(a, b):
    return mm(a, b), (a, b)


def _mm_b(res, ct):
    a, b = res
    return _dotb(ct, b, 1, 1), _dotb(a, ct, 0, 0)


mm.defvjp(_mm_f, _mm_b)


@jax.custom_vjp
def mm_nt(a, b):
    return _dotb(a, b, 1, 1)


def _mmnt_f(a, b):
    return mm_nt(a, b), (a, b)


def _mmnt_b(res, ct):
    a, b = res
    return _dotb(ct, b, 1, 0), _dotb(ct, a, 0, 0)


mm_nt.defvjp(_mmnt_f, _mmnt_b)


@jax.custom_vjp
def mm_tn(a, b):
    return _dotb(a, b, 0, 0)


def _mmtn_f(a, b):
    return mm_tn(a, b), (a, b)


def _mmtn_b(res, ct):
    a, b = res
    return _dotb(b, ct, 1, 1), _dotb(a, ct, 1, 0)


mm_tn.defvjp(_mmtn_f, _mmtn_b)


def mmh(a, b):
    return jnp.dot(a, b, precision=HI, preferred_element_type=f32)


def _rms(x, w):
    return x * lax.rsqrt(jnp.mean(x * x, axis=-1, keepdims=True) + EPS) * w


def _ln(x, w, b):
    mu = jnp.mean(x, axis=-1, keepdims=True)
    xc = x - mu
    var = jnp.mean(xc * xc, axis=-1, keepdims=True)
    return xc * lax.rsqrt(var + EPS) * w + b


def _gelu(x):
    return 0.5 * x * (1.0 + lax.erf(x * (2.0 ** -0.5)))


def _iota2(n, m, axis):
    return lax.broadcasted_iota(jnp.int32, (n, m), axis)


def _tri(n, strict=False):
    r, c = _iota2(n, n, 0), _iota2(n, n, 1)
    return (r > c) if strict else (r >= c)


def _eye(n):
    return (_iota2(n, n, 0) == _iota2(n, n, 1)).astype(f32)


def fn_norm(x, g):
    return (_rms(x, g),)


def fn_lb(lower_bounds):
    s = jax.nn.softmax(lower_bounds, axis=0)
    rows, cum = [], None
    for i in range(DEPTH):
        cum = s[i:i + 1] if cum is None else cum + s[i:i + 1]
        rows.append(cum - s[0:1])
    return (jnp.concatenate(rows, axis=0),)


def fn_hgrn(aq, af, ai, ag, lb, nw, st):
    c = aq.shape[0]
    sig = jax.nn.sigmoid(af)
    f = lb + (1.0 - lb) * sig
    logf = jnp.log(jnp.maximum(f, TINY))
    k = (1.0 - lb) * jax.nn.sigmoid(-af)
    q = jax.nn.silu(aq)
    v = ai
    b = mmh(_tri(c).astype(f32), logf)
    rel = b[:, None, :] - b[None, :, :]
    dec = jnp.exp(jnp.minimum(rel, 0.0)) * q[:, None, :] * k[None, :, :]
    scores = jnp.where(_tri(c), jnp.sum(dec, axis=-1), 0.0)
    b_end = b[c - 1:c, :]
    out = mm(scores, v) + mm_nt(q * jnp.exp(b), st)
    st_new = st * jnp.exp(b_end) + mm_tn(v, k * jnp.exp(b_end - b))
    o = _rms(out, nw) * jax.nn.silu(ag)
    return o, st_new


def fn_gdn_pre(tq, tk, tv, bq, bk, bv, p8, conv_w, alog, dtb):
    tile = bq.shape[0]
    h = tq.shape[0]
    outs = []
    for seg, (tl, cur) in enumerate(((tq, bq), (tk, bk), (tv, bv))):
        xe = jnp.concatenate([tl, cur], axis=0)
        acc = None
        for kk in range(SHORT_CONV):
            off = h - (SHORT_CONV - 1) + kk
            term = conv_w[kk:kk + 1, seg * GW:(seg + 1) * GW] * xe[off:off + tile, :]
            acc = term if acc is None else acc + term
        outs.append(jax.nn.silu(acc))
    sq, sk, sv = outs
    qh, kh = [], []
    for hh in range(NH):
        a = sq[:, hh * HD:(hh + 1) * HD]
        qh.append(a * lax.rsqrt(jnp.sum(a * a, axis=-1, keepdims=True) + EPS) * (HD ** -0.5))
        a = sk[:, hh * HD:(hh + 1) * HD]
        kh.append(a * lax.rsqrt(jnp.sum(a * a, axis=-1, keepdims=True) + EPS))
    q = jnp.concatenate(qh, axis=1)
    k = jnp.concatenate(kh, axis=1)
    beta = jax.nn.sigmoid(p8)
    g = -jnp.exp(alog) * jax.nn.softplus(p8 + dtb)
    r, cc = _iota2(128, GW, 0), _iota2(128, GW, 1) // HD
    e_beta = (r == cc).astype(f32)
    e_g = (r == cc + NH).astype(f32)
    return q, k, sv, mmh(beta, e_beta), mmh(g, e_g)


@jax.custom_vjp
def _inverse_given(m, tinv):
    return tinv


def _inverse_given_f(m, tinv):
    return tinv, tinv


def _inverse_given_b(tinv, ct):
    x = lax.dot_general(ct, tinv, (((1,), (1,)), ((), ())), precision=HI, preferred_element_type=f32)
    dm = -lax.dot_general(tinv, x, (((0,), (0,)), ((), ())), precision=HI, preferred_element_type=f32)
    return dm, jnp.zeros_like(tinv)


_inverse_given.defvjp(_inverse_given_f, _inverse_given_b)


def fn_gdn(q, k, v, beta, g, z, nw, s, tinv_saved=None):
    c = q.shape[0]
    gc = mmh(_tri(c).astype(f32), g)
    gcol = gc[:, 0:1]
    grow = jnp.sum(gcol * _eye(c), axis=0, keepdims=True)
    gamma = jnp.where(_tri(c), jnp.exp(jnp.minimum(gcol - grow, 0.0)), 0.0)
    kb = k * beta
    m = jnp.where(_tri(c, strict=True), mm_nt(kb, k) * gamma, 0.0)
    if tinv_saved is None:
        eye = _eye(c)
        tinv = eye - m
        p = m
        for _ in range(int(math.log2(c)) - 1):
            p = mmh(p, p)
            tinv = mmh(tinv, eye + p)
    else:
        tinv = _inverse_given(m, tinv_saved)
    egc = jnp.exp(gc)
    u = mmh(tinv, v * beta)
    w = mmh(tinv, kb * egc)
    qk = mm_nt(q, k) * gamma
    gc_end = gc[c - 1:c, :]
    q_dec = q * egc
    k_dec = k * jnp.exp(gc_end - gc)
    v_new = u - mm(w, s)
    out = mm(q_dec, s) + mm(qk, v_new)
    s_new = s * jnp.exp(gc_end) + mm_tn(k_dec, v_new)
    o = _rms(out, nw) * jax.nn.silu(z)
    return (o, s_new, tinv) if tinv_saved is None else (o, s_new)


def fn_gmlp(cu, cv, ln_w, ln_b, w_s, b_s):
    n = cu.shape[0]
    ug = _gelu(cu)
    vn = _ln(_gelu(cv), ln_w, ln_b)
    eye = _eye(n)
    cols = []
    for hh in range(NH):
        wc = jnp.where(_tri(n), w_s[hh * n:(hh + 1) * n, :], 0.0)
        bcol = jnp.sum(b_s[hh:hh + 1, :] * eye, axis=1, keepdims=True)
        cols.append(mm(wc, vn[:, hh * HD:(hh + 1) * HD]) + bcol)
    return (ug * jnp.concatenate(cols, axis=1),)


def fn_conv(ta, tg, a, gate, dw_w, dw_b, ln_w, ln_b):
    tile = a.shape[0]
    h = ta.shape[0]
    ya = jnp.concatenate([ta, a], axis=0)
    yg = jnp.concatenate([tg, gate], axis=0)
    y = ya * jax.nn.sigmoid(yg)
    acc = None
    for kk in range(CONV_WIDTH):
        off = h - (CONV_WIDTH - 1) + kk
        term = dw_w[kk:kk + 1, :] * y[off:off + tile, :]
        acc = term if acc is None else acc + term
    return (jax.nn.silu(_ln(acc + dw_b, ln_w, ln_b)),)


def _full_spec(arr):
    nd = arr.ndim
    return pl.BlockSpec(arr.shape, lambda *_: (0,) * nd)


def rowwise(name, fn, tiled, params, outs, tile):
    t = tiled[0][0].shape[0]
    tile = min(tile, t)
    nt, np_ = len(tiled), len(params)

    def body(*refs):
        vals = [r[...].astype(f32) for r in refs[:nt + np_]]
        res = fn(*vals)
        for o_ref, r in zip(refs[nt + np_:], res):
            o_ref[...] = r.astype(o_ref.dtype)

    in_specs = [pl.BlockSpec((tile, w), lambda i, c=c: (i, c)) for _, w, c in tiled]
    in_specs += [_full_spec(p) for p in params]
    out_specs = [pl.BlockSpec((tile, w), lambda i: (i, 0)) for w, _ in outs]
    out_shape = [jax.ShapeDtypeStruct((t, w), dt) for w, dt in outs]
    return pl.pallas_call(body, grid=(t // tile,), in_specs=in_specs, out_specs=out_specs,
                          out_shape=out_shape, name=name)(*[a for a, _, _ in tiled], *params)


def rowwise_bwd(name, fn, tiled, params, cots, gouts, tile, addto=None):
    t = tiled[0][0].shape[0]
    tile = min(tile, t)
    nt, np_, nc = len(tiled), len(params), len(cots)
    na = 0 if addto is None else 1
    gidx = [i for i, g in enumerate(gouts) if g is not None]

    def body(*refs):
        i = pl.program_id(0)
        vals = [r[...].astype(f32) for r in refs[:nt + np_]]
        cvals = tuple(r[...].astype(f32) for r in refs[nt + np_:nt + np_ + nc])
        _, vjp = jax.vjp(fn, *vals)
        grads = vjp(cvals)
        orefs = refs[nt + np_ + nc + na:]
        for n, j in enumerate(gidx):
            g = grads[j]
            if na and n == 0:
                g = g + refs[nt + np_ + nc][...].astype(f32)
            orefs[n][...] = g.astype(orefs[n].dtype)
        prefs = orefs[len(gidx):]

        @pl.when(i == 0)
        def _():
            for r in prefs:
                r[...] = jnp.zeros_like(r)

        for r, g in zip(prefs, grads[nt:]):
            r[...] += g

    in_specs = [pl.BlockSpec((tile, w), lambda i, c=c: (i, c)) for _, w, c in tiled]
    in_specs += [_full_spec(p) for p in params]
    in_specs += [pl.BlockSpec((tile, w), lambda i, c=c: (i, c)) for _, w, c in cots]
    args = [a for a, _, _ in tiled] + list(params) + [a for a, _, _ in cots]
    if na:
        in_specs.append(pl.BlockSpec((tile, addto[1]), lambda i, c=addto[2]: (i, c)))
        args.append(addto[0])
    out_specs = [pl.BlockSpec((tile, tiled[j][1]), lambda i: (i, 0)) for j in gidx]
    out_shape = [jax.ShapeDtypeStruct((t, tiled[j][1]), gouts[j]) for j in gidx]
    out_specs += [_full_spec(p) for p in params]
    out_shape += [jax.ShapeDtypeStruct(p.shape, f32) for p in params]
    res = pl.pallas_call(body, grid=(t // tile,), in_specs=in_specs, out_specs=out_specs,
                         out_shape=out_shape, name=name)(*args)
    return res[:len(gidx)], res[len(gidx):]


def halo_fwd(name, fn, tiled, params, outs, tile, halo):
    t = tiled[0][0].shape[0]
    tile = min(tile, t)
    hal = [j for j, x in enumerate(tiled) if x[3]]
    nt, nh, np_ = len(tiled), len(hal), len(params)
    per = tile // halo

    def body(*refs):
        i = pl.program_id(0)
        first = (i > 0).astype(f32)
        tails = [r[...].astype(f32) * first for r in refs[:nh]]
        vals = [r[...].astype(f32) for r in refs[nh:nh + nt + np_]]
        res = fn(*tails, *vals)
        for o_ref, r in zip(refs[nh + nt + np_:], res):
            o_ref[...] = r.astype(o_ref.dtype)

    in_specs = [pl.BlockSpec((halo, tiled[j][1]), lambda i, c=tiled[j][2]: (jnp.maximum(i * per - 1, 0), c))
                for j in hal]
    in_specs += [pl.BlockSpec((tile, w), lambda i, c=c: (i, c)) for _, w, c, _ in tiled]
    in_specs += [_full_spec(p) for p in params]
    out_specs = [pl.BlockSpec((tile, w), lambda i: (i, 0)) for w, _ in outs]
    out_shape = [jax.ShapeDtypeStruct((t, w), dt) for w, dt in outs]
    args = [tiled[j][0] for j in hal] + [x[0] for x in tiled] + list(params)
    return pl.pallas_call(body, grid=(t // tile,), in_specs=in_specs, out_specs=out_specs,
                          out_shape=out_shape, name=name)(*args)


def halo_bwd(name, fn, tiled, params, cots, gdtype, tile, halo):
    t = tiled[0][0].shape[0]
    tile = min(tile, t)
    hal = [j for j, x in enumerate(tiled) if x[3]]
    nt, nh, np_, nc = len(tiled), len(hal), len(params), len(cots)
    per = tile // halo
    n_tiles = t // tile

    def body(*refs):
        s = pl.program_id(0)
        i = n_tiles - 1 - s
        first = (i > 0).astype(f32)
        tails = [r[...].astype(f32) * first for r in refs[:nh]]
        vals = [r[...].astype(f32) for r in refs[nh:nh + nt + np_]]
        cvals = tuple(r[...].astype(f32) for r in refs[nh + nt + np_:nh + nt + np_ + nc])
        n_in = nh + nt + np_ + nc
        orefs = refs[n_in:n_in + nt]
        prefs = refs[n_in + nt:n_in + nt + np_]
        carries = refs[n_in + nt + np_:]

        @pl.when(s == 0)
        def _():
            for r in prefs:
                r[...] = jnp.zeros_like(r)
            for r in carries:
                r[...] = jnp.zeros_like(r)

        _, vjp = jax.vjp(fn, *tails, *vals)
        grads = vjp(cvals)
        for j in range(nt):
            g = grads[nh + j]
            if j in hal:
                cr = carries[hal.index(j)]
                g = jnp.concatenate([g[:tile - halo], g[tile - halo:] + cr[...]], axis=0)
            orefs[j][...] = g.astype(orefs[j].dtype)
        for n in range(nh):
            carries[n][...] = grads[n] * first
        for r, g in zip(prefs, grads[nh + nt:]):
            r[...] += g

    rev = lambda s: n_tiles - 1 - s
    in_specs = [pl.BlockSpec((halo, tiled[j][1]),
                             lambda s, c=tiled[j][2]: (jnp.maximum(rev(s) * per - 1, 0), c)) for j in hal]
    in_specs += [pl.BlockSpec((tile, w), lambda s, c=c: (rev(s), c)) for _, w, c, _ in tiled]
    in_specs += [_full_spec(p) for p in params]
    in_specs += [pl.BlockSpec((tile, w), lambda s, c=c: (rev(s), c)) for _, w, c in cots]
    out_specs = [pl.BlockSpec((tile, w), lambda s: (rev(s), 0)) for _, w, _, _ in tiled]
    out_shape = [jax.ShapeDtypeStruct((t, w), gdtype) for _, w, _, _ in tiled]
    out_specs += [_full_spec(p) for p in params]
    out_shape += [jax.ShapeDtypeStruct(p.shape, f32) for p in params]
    scratch = [pltpu.VMEM((halo, tiled[j][1]), f32) for j in hal]
    args = [tiled[j][0] for j in hal] + [x[0] for x in tiled] + list(params) + [a for a, _, _ in cots]
    res = pl.pallas_call(body, grid=(n_tiles,), in_specs=in_specs, out_specs=out_specs,
                         out_shape=out_shape, scratch_shapes=scratch, name=name)(*args)
    return res[:nt], res[nt:]


def scan_fwd(name, fn, tiled, pparams, sparams, out_dtype, hb, n_extra=0):
    t = tiled[0][0].shape[0]
    n = t // CHUNK
    nt, npp, nsp = len(tiled), len(pparams), len(sparams)
    w = HD * hb

    def body(*refs):
        c = pl.program_id(1)
        n_in = nt + npp + nsp
        o_ref, sv_ref = refs[n_in], refs[n_in + 1]
        ex_refs, st = refs[n_in + 2:n_in + 2 + n_extra], refs[n_in + 2 + n_extra]

        @pl.when(c == 0)
        def _():
            st[...] = jnp.zeros_like(st)

        vals = [r[...].astype(f32) for r in refs[:n_in]]
        outs = []
        for hh in range(hb):
            hv = [v[:, hh * HD:(hh + 1) * HD] for v in vals[:nt + npp]] + vals[nt + npp:]
            s_in = st[hh]
            sv_ref[hh] = s_in
            res = fn(*hv, s_in)
            st[hh] = res[1]
            outs.append(res[0])
            for e_ref, e in zip(ex_refs, res[2:]):
                e_ref[hh] = e
        o_ref[...] = (outs[0] if hb == 1 else jnp.concatenate(outs, axis=1)).astype(o_ref.dtype)

    in_specs = [pl.BlockSpec((CHUNK, w), lambda g, c, b=b: (c, b // hb + g)) for _, b in tiled]
    in_specs += [pl.BlockSpec((1, w), lambda g, c: (0, g)) for _ in pparams]
    in_specs += [_full_spec(p) for p in sparams]
    out_specs = [pl.BlockSpec((CHUNK, w), lambda g, c: (c, g)),
                 pl.BlockSpec((hb, None, HD, HD), lambda g, c: (g, c, 0, 0))]
    out_shape = [jax.ShapeDtypeStruct((t, GW), out_dtype), jax.ShapeDtypeStruct((NH, n, HD, HD), f32)]
    out_specs += [pl.BlockSpec((hb, None, CHUNK, CHUNK), lambda g, c: (g, c, 0, 0))] * n_extra
    out_shape += [jax.ShapeDtypeStruct((NH, n, CHUNK, CHUNK), f32)] * n_extra
    return pl.pallas_call(body, grid=(NH // hb, n), in_specs=in_specs, out_specs=out_specs, out_shape=out_shape,
                          scratch_shapes=[pltpu.VMEM((hb, HD, HD), f32)], name=name)(
        *[a for a, _ in tiled], *pparams, *sparams)


def scan_bwd(name, fn, tiled, pparams, sparams, states, cot, gdtypes, hb, extras=()):
    t = tiled[0][0].shape[0]
    n = t // CHUNK
    nt, npp, nsp, nex = len(tiled), len(pparams), len(sparams), len(extras)
    w = HD * hb

    def body(*refs):
        g, s = pl.program_id(0), pl.program_id(1)
        n_in = nt + npp + nsp
        vals = [r[...].astype(f32) for r in refs[:n_in]]
        st_ref = refs[n_in]
        do = refs[n_in + 1][...].astype(f32)
        ex_refs = refs[n_in + 2:n_in + 2 + nex]
        n_op = n_in + 2 + nex
        orefs = refs[n_op:n_op + nt]
        pprefs = refs[n_op + nt:n_op + nt + npp]
        sprefs = refs[n_op + nt + npp:n_op + nt + npp + nsp]
        ds = refs[n_op + nt + npp + nsp]

        @pl.when(s == 0)
        def _():
            ds[...] = jnp.zeros_like(ds)
            for r in pprefs:
                r[...] = jnp.zeros_like(r)

        @pl.when((s == 0) & (g == 0))
        def _():
            for r in sprefs:
                r[...] = jnp.zeros_like(r)

        per_head = []
        for hh in range(hb):
            sl = slice(hh * HD, (hh + 1) * HD)
            hv = [v[:, sl] for v in vals[:nt + npp]] + vals[nt + npp:]
            ex = [r[hh] for r in ex_refs]
            _, vjp = jax.vjp(lambda *a: fn(*a, *ex), *hv, st_ref[hh])
            grads = vjp((do[:, sl], ds[hh]))
            ds[hh] = grads[n_in]
            per_head.append(grads)
        cat = lambda j: per_head[0][j] if hb == 1 else jnp.concatenate([gr[j] for gr in per_head], axis=1)
        for j in range(nt):
            orefs[j][...] = cat(j).astype(orefs[j].dtype)
        for j, r in enumerate(pprefs):
            r[...] += cat(nt + j)
        for j, r in enumerate(sprefs):
            tot = per_head[0][nt + npp + j]
            for gr in per_head[1:]:
                tot = tot + gr[nt + npp + j]
            r[...] += tot

    rev = lambda s: n - 1 - s
    in_specs = [pl.BlockSpec((CHUNK, w), lambda g, s, b=b: (rev(s), b // hb + g)) for _, b in tiled]
    in_specs += [pl.BlockSpec((1, w), lambda g, s: (0, g)) for _ in pparams]
    in_specs += [_full_spec(p) for p in sparams]
    in_specs += [pl.BlockSpec((hb, None, HD, HD), lambda g, s: (g, rev(s), 0, 0)),
                 pl.BlockSpec((CHUNK, w), lambda g, s, b=cot[1]: (rev(s), b // hb + g))]
    in_specs += [pl.BlockSpec((hb, None, CHUNK, CHUNK), lambda g, s: (g, rev(s), 0, 0)) for _ in extras]
    out_specs = [pl.BlockSpec((CHUNK, w), lambda g, s: (rev(s), g)) for _ in tiled]
    out_shape = [jax.ShapeDtypeStruct((t, GW), dt) for dt in gdtypes]
    out_specs += [pl.BlockSpec((1, w), lambda g, s: (0, g)) for _ in pparams]
    out_shape += [jax.ShapeDtypeStruct(p.shape, f32) for p in pparams]
    out_specs += [_full_spec(p) for p in sparams]
    out_shape += [jax.ShapeDtypeStruct(p.shape, f32) for p in sparams]
    res = pl.pallas_call(body, grid=(NH // hb, n), in_specs=in_specs, out_specs=out_specs, out_shape=out_shape,
                         scratch_shapes=[pltpu.VMEM((hb, HD, HD), f32)], name=name)(
        *[a for a, _ in tiled], *pparams, *sparams, states, cot[0], *extras)
    return res[:nt], res[nt:nt + npp], res[nt + npp:]


def matmul(name, a, b, mode, out_dtype=f32, tm=1024, tn=1024, tk=2048, epilogue=None, extra=None, slots=None):
    if mode == 'nn':
        (m, k), n = a.shape, b.shape[1]
    elif mode == 'nt':
        (m, k), n = a.shape, b.shape[0]
    else:
        (k, m), n = a.shape, b.shape[1]
    tm, tn, tk = min(tm, m), min(tn, n), min(tk, k)
    if slots == 'rows':
        tm = min(tm, m // N_DEV)
    if slots == 'cols':
        tn = min(tn, n // N_DEV)
    nk = k // tk
    ca, cb = {'nn': (1, 0), 'nt': (1, 1), 'tn': (0, 0)}[mode]

    def finish(refs, r):
        if epilogue == 'relu2':
            refs[2][...] = r
            refs[3][...] = jnp.square(jnp.maximum(r, 0.0)).astype(bf16)
        elif epilogue == 'relu2_bwd':
            refs[3][...] = (r * 2.0 * jnp.maximum(refs[2][...], 0.0)).astype(refs[3].dtype)
        else:
            refs[2][...] = r.astype(refs[2].dtype)

    def body(*refs):
        part = _dotb(refs[0][...], refs[1][...], ca, cb)
        if nk == 1:
            finish(refs, part)
            return
        acc = refs[-1]
        kk = pl.program_id(2)

        @pl.when(kk == 0)
        def _():
            acc[...] = part

        @pl.when(kk > 0)
        def _():
            acc[...] += part

        @pl.when(kk == nk - 1)
        def _():
            finish(refs, acc[...])

    if mode == 'nn':
        a_spec = pl.BlockSpec((tm, tk), lambda i, j, kk: (i, kk))
        b_spec = pl.BlockSpec((tk, tn), lambda i, j, kk: (kk, j))
    elif mode == 'nt':
        a_spec = pl.BlockSpec((tm, tk), lambda i, j, kk: (i, kk))
        b_spec = pl.BlockSpec((tn, tk), lambda i, j, kk: (j, kk))
    else:
        a_spec = pl.BlockSpec((tk, tm), lambda i, j, kk: (kk, i))
        b_spec = pl.BlockSpec((tk, tn), lambda i, j, kk: (kk, j))
    o_spec = pl.BlockSpec((tm, tn), lambda i, j, kk: (i, j))
    in_specs, args = [a_spec, b_spec], [a, b]
    if epilogue == 'relu2':
        out_specs = [o_spec, o_spec]
        out_shape = [jax.ShapeDtypeStruct((m, n), f32), jax.ShapeDtypeStruct((m, n), bf16)]
    elif slots == 'rows':
        per = (m // N_DEV) // tm
        out_specs = pl.BlockSpec((None, None, tm, tn), lambda i, j, kk: ((i // per) % 2, (i // per) // 2, i % per, j))
        out_shape = jax.ShapeDtypeStruct((2, N_DEV // 2, m // N_DEV, n), out_dtype)
    elif slots == 'cols':
        per = (n // N_DEV) // tn
        out_specs = pl.BlockSpec((None, None, tm, tn), lambda i, j, kk: ((j // per) % 2, (j // per) // 2, i, j % per))
        out_shape = jax.ShapeDtypeStruct((2, N_DEV // 2, m, n // N_DEV), out_dtype)
    else:
        out_specs, out_shape = o_spec, jax.ShapeDtypeStruct((m, n), out_dtype)
        if epilogue == 'relu2_bwd':
            in_specs.append(o_spec)
            args.append(extra)
    scratch = [pltpu.VMEM((tm, tn), f32)] if nk > 1 else []
    return pl.pallas_call(body, grid=(m // tm, n // tn, nk), in_specs=in_specs, out_specs=out_specs,
                          out_shape=out_shape, scratch_shapes=scratch, name=name)(*args)


def final_loss(name, x, y, g, target):
    t, d = x.shape
    tile = min(ROW_TILE, t)

    def body(x_ref, y_ref, g_ref, t_ref, dx_ref, l_ref):
        i = pl.program_id(0)

        @pl.when(i == 0)
        def _():
            l_ref[...] = jnp.zeros_like(l_ref)

        err = x_ref[...] + _rms(y_ref[...], g_ref[...]) - t_ref[...]
        dx_ref[...] = err * (1.0 / d)
        l_ref[...] += 0.5 * jnp.sum(jnp.mean(err * err, axis=-1, keepdims=True), axis=0, keepdims=True)

    row = pl.BlockSpec((tile, d), lambda i: (i, 0))
    return pl.pallas_call(
        body, grid=(t // tile,), in_specs=[row, row, _full_spec(g), row],
        out_specs=[row, pl.BlockSpec((1, 1), lambda i: (0, 0))],
        out_shape=[jax.ShapeDtypeStruct((t, d), f32), jax.ShapeDtypeStruct((1, 1), f32)], name=name)(x, y, g, target)


def adamw(name, w, m, v, gslots, tr=128):
    nl, r, c = w.shape
    tr = min(tr, r)
    nr = r // tr
    ns = gslots[0].shape[0]
    c1 = 1.0 / (1.0 - ADAM_B1 ** ADAM_STEP)
    c2 = 1.0 / (1.0 - ADAM_B2 ** ADAM_STEP)

    def body(*refs):
        w_ref, m_ref, v_ref = refs[:3]
        g_refs = refs[3:3 + nl]
        go_ref, d_ref, mo_ref, vo_ref = refs[3 + nl:]
        l = pl.program_id(0)
        g = None
        for li in range(nl):
            s = g_refs[li][0].astype(f32)
            for k in range(1, ns):
                s = s + g_refs[li][k].astype(f32)
            g = s if g is None else jnp.where(l == li, s, g)
        mn = ADAM_B1 * m_ref[...] + (1.0 - ADAM_B1) * g
        vn = ADAM_B2 * v_ref[...] + (1.0 - ADAM_B2) * jnp.square(g)
        go_ref[...] = g
        mo_ref[...] = mn
        vo_ref[...] = vn
        d_ref[...] = -ADAM_LR * ((mn * c1) / (jnp.sqrt(vn * c2) + ADAM_EPS) + ADAM_WD * w_ref[...])

    blk = pl.BlockSpec((None, tr, c), lambda l, i: (l, i, 0))

    def gspec(li):
        return pl.BlockSpec((ns, tr, c), lambda l, i: (0, jnp.where(l == li, i, jnp.where(l < li, 0, nr - 1)), 0))

    return pl.pallas_call(
        body, grid=(nl, nr), in_specs=[blk, blk, blk] + [gspec(li) for li in range(nl)],
        out_specs=[blk] * 4, out_shape=[jax.ShapeDtypeStruct(w.shape, f32)] * 4, name=name)(w, m, v, *gslots)


def exchange(name, arrays, gather, group):
    n = len(arrays)
    ns = {'all': 8, 'chips': 4, 'core': 2}[group]

    def body(*refs):
        ins, outs = refs[:n], refs[n:2 * n]
        send_sems, recv_sems, loc_sems = refs[2 * n:]
        x, y, c = lax.axis_index("x"), lax.axis_index("y"), lax.axis_index("c")

        def member(k):
            if group == 'all':
                px, py, pc = x ^ ((k >> 2) & 1), y ^ ((k >> 1) & 1), c ^ (k & 1)
                return (px, py, pc), 4 * px + 2 * py + pc
            if group == 'chips':
                px, py = x ^ ((k >> 1) & 1), y ^ (k & 1)
                return (px, py, c), 2 * px + py
            return (x, y, c ^ k), c ^ k

        _, me = member(0)
        sends, recvs, locs = [], [], []
        for a in range(n):
            lc = pltpu.make_async_copy(ins[a] if gather else ins[a].at[me], outs[a].at[me], loc_sems.at[a])
            lc.start()
            locs.append(lc)
            for k in range(1, ns):
                dev, peer = member(k)
                src = ins[a] if gather else ins[a].at[peer]
                cp = pltpu.make_async_remote_copy(src_ref=src, dst_ref=outs[a].at[me], send_sem=send_sems.at[a, k],
                                                  recv_sem=recv_sems.at[a, k], device_id=dev, device_id_type=MESH)
                cp.start()
                sends.append(cp)
                recvs.append(pltpu.make_async_remote_copy(src_ref=src, dst_ref=outs[a].at[peer], send_sem=send_sems.at[a, k],
                                                          recv_sem=recv_sems.at[a, k], device_id=dev, device_id_type=MESH))
        for cp in recvs:
            cp.wait_recv()
        for cp in sends:
            cp.wait_send()
        for lc in locs:
            lc.wait()

    anyspec = pl.BlockSpec(memory_space=pl.ANY)
    out_shape = [jax.ShapeDtypeStruct(((ns,) + a.shape) if gather else a.shape, a.dtype) for a in arrays]
    return pl.pallas_call(
        body, in_specs=[anyspec] * n, out_specs=[anyspec] * n, out_shape=out_shape,
        scratch_shapes=[pltpu.SemaphoreType.DMA((n, ns)), pltpu.SemaphoreType.DMA((n, ns)),
                        pltpu.SemaphoreType.DMA((n,))], name=name)(*arrays)


def gather_two_level(name, arrays):
    n = len(arrays)

    def body(*refs):
        ins, outs = refs[:n], refs[n:2 * n]
        send_sems, recv_sems, loc_sems = refs[2 * n:]
        x, y, c = lax.axis_index("x"), lax.axis_index("y"), lax.axis_index("c")
        sib = (x, y, 1 - c)

        def chip(k):
            px, py = x ^ ((k >> 1) & 1), y ^ (k & 1)
            return (px, py), 2 * px + py

        _, mine = chip(0)

        def copy(a, sem, src, slot, to):
            return pltpu.make_async_remote_copy(src_ref=src, dst_ref=outs[a].at[slot], send_sem=send_sems.at[a, sem],
                                                recv_sem=recv_sems.at[a, sem], device_id=to, device_id_type=MESH)

        sends, locs = [], []
        for a in range(n):
            lc = pltpu.make_async_copy(ins[a], outs[a].at[2 * mine + c], loc_sems.at[a])
            lc.start()
            locs.append(lc)
            sends.append(copy(a, 0, ins[a], 2 * mine + c, sib))
            for k in range(1, 4):
                (px, py), _ = chip(k)
                sends.append(copy(a, k, ins[a], 2 * mine + c, (px, py, c)))
        for cp in sends:
            cp.start()
        passed = []
        for a in range(n):
            for k in range(1, 4):
                _, other = chip(k)
                slot = 2 * other + c
                copy(a, k, outs[a].at[slot], slot, sib).wait_recv()
                fw = copy(a, 3 + k, outs[a].at[slot], slot, sib)
                fw.start()
                passed.append(fw)
        for a in range(n):
            copy(a, 0, ins[a], 2 * mine + 1 - c, sib).wait_recv()
            for k in range(1, 4):
                _, other = chip(k)
                slot = 2 * other + 1 - c
                copy(a, 3 + k, outs[a].at[slot], slot, sib).wait_recv()
        for cp in sends + passed:
            cp.wait_send()
        for lc in locs:
            lc.wait()

    anyspec = pl.BlockSpec(memory_space=pl.ANY)
    out_shape = [jax.ShapeDtypeStruct((N_DEV,) + a.shape, a.dtype) for a in arrays]
    return pl.pallas_call(
        body, in_specs=[anyspec] * n, out_specs=[anyspec] * n, out_shape=out_shape,
        scratch_shapes=[pltpu.SemaphoreType.DMA((n, 7)), pltpu.SemaphoreType.DMA((n, 7)),
                        pltpu.SemaphoreType.DMA((n,))], name=name)(*arrays)


def send_to_sibling(name, arrays):
    n = len(arrays)

    def body(*refs):
        ins, outs = refs[:n], refs[n:2 * n]
        send_sems, recv_sems = refs[2 * n:]
        x, y, c = lax.axis_index("x"), lax.axis_index("y"), lax.axis_index("c")
        cps = [pltpu.make_async_remote_copy(src_ref=ins[a].at[1 - c], dst_ref=outs[a], send_sem=send_sems.at[a],
                                            recv_sem=recv_sems.at[a], device_id=(x, y, 1 - c), device_id_type=MESH)
               for a in range(n)]
        for cp in cps:
            cp.start()
        for cp in cps:
            cp.wait()

    anyspec = pl.BlockSpec(memory_space=pl.ANY)
    out_shape = [jax.ShapeDtypeStruct(a.shape[1:], a.dtype) for a in arrays]
    return pl.pallas_call(
        body, in_specs=[anyspec] * n, out_specs=[anyspec] * n, out_shape=out_shape,
        scratch_shapes=[pltpu.SemaphoreType.DMA((n,)), pltpu.SemaphoreType.DMA((n,))], name=name)(*arrays)


def chip_sum(name, both, other):
    _, nc, r, c = both.shape
    tr = min(ROW_TILE, r)

    def body(b_ref, o_ref, s_ref):
        core = lax.axis_index("c")
        own = jnp.where(core == 0, b_ref[0], b_ref[1]).astype(f32)
        s_ref[...] = (own + o_ref[...].astype(f32)).astype(s_ref.dtype)

    return pl.pallas_call(
        body, grid=(nc, r // tr),
        in_specs=[pl.BlockSpec((2, None, tr, c), lambda s, i: (0, s, i, 0)),
                  pl.BlockSpec((None, tr, c), lambda s, i: (s, i, 0))],
        out_specs=pl.BlockSpec((None, tr, c), lambda s, i: (s, i, 0)),
        out_shape=jax.ShapeDtypeStruct((nc, r, c), both.dtype), name=name)(both, other)


REPLICATED = ('lower_bounds', 'norm_mix_pre', 'norm_mix_post', 'norm_ff_pre', 'norm_ff_post', 'hgrn_norm_w',
              'gdn_a_log', 'gdn_dt_bias', 'gdn_norm_w', 'gmlp_ln_w', 'gmlp_ln_b', 'gmlp_w_s', 'gmlp_b_s',
              'conv_dw_b', 'conv_ln_w', 'conv_ln_b')
WEIGHTS = ('lower_bounds', 'norm_mix_pre', 'norm_mix_post', 'norm_ff_pre', 'norm_ff_post', 'w_in', 'w_out',
           'hgrn_norm_w', 'gdn_conv_w', 'gdn_a_log', 'gdn_dt_bias', 'gdn_norm_w', 'gmlp_ln_w', 'gmlp_ln_b',
           'gmlp_w_s', 'gmlp_b_s', 'conv_dw_w', 'conv_dw_b', 'conv_ln_w', 'conv_ln_b', 'w_ff1', 'w_ff2')


def _row(v):
    return v.reshape(1, -1)


def _pad_lanes(v, offset):
    return jnp.pad(v, (offset, 128 - offset - v.shape[0])).reshape(1, 128)


def _layer_fwd(l, x0, p, lb_all):
    t = x0.shape[0]
    sv = {'x0': x0}
    (h,) = rowwise(f"norm_mix_pre{l}", fn_norm, [(x0, D_MODEL, 0)], [p['g_mix_pre']], [(D_MODEL, bf16)], ROW_TILE)
    proj = matmul(f"proj{l}", h, p['w_in'], 'nn', f32, tn=896)
    sv.update(h=h, proj=proj)
    lb = lb_all[l:l + 1]
    o_a, st_a = scan_fwd(f"hgrn{l}", fn_hgrn, [(proj, 0), (proj, 4), (proj, 8), (proj, 12)], [lb], [p['hgrn_norm_w']], bf16, 1)
    q, k, v, beta, g = halo_fwd(
        f"gdn_pre{l}", fn_gdn_pre,
        [(proj, GW, 4, True), (proj, GW, 5, True), (proj, GW, 6, True), (proj, 128, 48, False)],
        [p['gdn_conv_w'], p['alog'], p['dtb']], [(GW, f32)] * 5, ROW_TILE, 8)
    o_b, st_b, tinv_b = scan_fwd(f"gdn{l}", fn_gdn, [(q, 0), (k, 0), (v, 0), (beta, 0), (g, 0), (proj, 28)], [],
                                 [p['gdn_norm_w']], bf16, NH, n_extra=1)
    (o_c,) = rowwise(f"gmlp{l}", fn_gmlp, [(proj, GW, 8), (proj, GW, 9)],
                     [p['gmlp_ln_w'], p['gmlp_ln_b'], p['gmlp_w_s'], p['gmlp_b_s']], [(GW, bf16)], MIX_CHUNK)
    (o_d,) = halo_fwd(f"conv{l}", fn_conv, [(proj, GW, 10, True), (proj, GW, 11, True)],
                      [p['conv_dw_w'], p['conv_dw_b'], p['conv_ln_w'], p['conv_ln_b']], [(GW, bf16)], ROW_TILE, 32)
    mix = jnp.concatenate([o_a, o_b, o_c, o_d], axis=1)
    y1 = matmul(f"out_proj{l}", mix, p['w_out'], 'nn', f32)
    (x1,) = rowwise(f"res_mix{l}", lambda x, y, gg: (x + _rms(y, gg),), [(x0, D_MODEL, 0), (y1, D_MODEL, 0)],
                    [p['g_mix_post']], [(D_MODEL, f32)], ROW_TILE)
    (h2,) = rowwise(f"norm_ff_pre{l}", fn_norm, [(x1, D_MODEL, 0)], [p['g_ff_pre']], [(D_MODEL, bf16)], ROW_TILE)
    u, a = matmul(f"ff1_{l}", h2, p['w_ff1'], 'nn', epilogue='relu2')
    y2 = matmul(f"ff2_{l}", a, p['w_ff2'], 'nn', f32)
    sv.update(st_a=st_a, q=q, k=k, v=v, beta=beta, g=g, st_b=st_b, tinv_b=tinv_b, mix=mix, y1=y1, x1=x1, h2=h2, u=u, a=a, y2=y2)
    return sv


def _layer_bwd(l, dx, sv, p, lb_all):
    gr = {}
    t = dx.shape[0]
    (dy2,), (gr['norm_ff_post'],) = rowwise_bwd(f"res_ff_bwd{l}", fn_norm, [(sv['y2'], D_MODEL, 0)], [p['g_ff_post']],
                                                [(dx, D_MODEL, 0)], [bf16], ROW_TILE)
    du = matmul(f"ff2_dx{l}", dy2, p['w_ff2'], 'nt', bf16, epilogue='relu2_bwd', extra=sv['u'])
    gr['w_ff2'] = matmul(f"ff2_dw{l}", sv['a'], dy2, 'tn', bf16, slots='rows')
    dh2 = matmul(f"ff1_dx{l}", du, p['w_ff1'], 'nt', f32)
    gr['w_ff1'] = matmul(f"ff1_dw{l}", sv['h2'], du, 'tn', bf16, slots='cols')
    (dx1,), (gr['norm_ff_pre'],) = rowwise_bwd(f"norm_ff_pre_bwd{l}", fn_norm, [(sv['x1'], D_MODEL, 0)], [p['g_ff_pre']],
                                               [(dh2, D_MODEL, 0)], [f32], ROW_TILE, addto=(dx, D_MODEL, 0))
    (dy1,), (gr['norm_mix_post'],) = rowwise_bwd(f"res_mix_bwd{l}", fn_norm, [(sv['y1'], D_MODEL, 0)], [p['g_mix_post']],
                                                 [(dx1, D_MODEL, 0)], [bf16], ROW_TILE)
    dmix = matmul(f"out_proj_dx{l}", dy1, p['w_out'], 'nt', f32)
    gr['w_out'] = matmul(f"out_proj_dw{l}", sv['mix'], dy1, 'tn', bf16, slots='rows')
    proj = sv['proj']
    lb = lb_all[l:l + 1]
    d_a, (dlb,), (gr['hgrn_norm_w'],) = scan_bwd(
        f"hgrn_bwd{l}", fn_hgrn, [(proj, 0), (proj, 4), (proj, 8), (proj, 12)], [lb], [p['hgrn_norm_w']],
        sv['st_a'], (dmix, 0), [bf16] * 4, 1)
    d_b, _, (gr['gdn_norm_w'],) = scan_bwd(
        f"gdn_bwd{l}", fn_gdn, [(sv['q'], 0), (sv['k'], 0), (sv['v'], 0), (sv['beta'], 0), (sv['g'], 0), (proj, 28)],
        [], [p['gdn_norm_w']], sv['st_b'], (dmix, 4), [f32] * 5 + [bf16], NH, extras=[sv['tinv_b']])
    d_bp, (gr['gdn_conv_w'], dalog, ddtb) = halo_bwd(
        f"gdn_pre_bwd{l}", fn_gdn_pre,
        [(proj, GW, 4, True), (proj, GW, 5, True), (proj, GW, 6, True), (proj, 128, 48, False)],
        [p['gdn_conv_w'], p['alog'], p['dtb']], [(d_b[j], GW, 0) for j in range(5)], bf16, ROW_TILE, 8)
    gr['gdn_a_log'] = dalog[0, NH:2 * NH]
    gr['gdn_dt_bias'] = ddtb[0, NH:2 * NH]
    d_c, (gr['gmlp_ln_w'], gr['gmlp_ln_b'], gr['gmlp_w_s'], gr['gmlp_b_s']) = rowwise_bwd(
        f"gmlp_bwd{l}", fn_gmlp, [(proj, GW, 8), (proj, GW, 9)],
        [p['gmlp_ln_w'], p['gmlp_ln_b'], p['gmlp_w_s'], p['gmlp_b_s']], [(dmix, GW, 2)], [bf16, bf16], MIX_CHUNK)
    d_d, (gr['conv_dw_w'], gr['conv_dw_b'], gr['conv_ln_w'], gr['conv_ln_b']) = halo_bwd(
        f"conv_bwd{l}", fn_conv, [(proj, GW, 10, True), (proj, GW, 11, True)],
        [p['conv_dw_w'], p['conv_dw_b'], p['conv_ln_w'], p['conv_ln_b']], [(dmix, GW, 3)], bf16, ROW_TILE, 32)
    dproj = jnp.concatenate(list(d_a) + [d_bp[0], d_bp[1], d_bp[2], d_b[5]] + list(d_c) + list(d_d) + [d_bp[3]], axis=1)
    dh = matmul(f"proj_dx{l}", dproj, p['w_in'], 'nt', f32, tk=896)
    gr['w_in'] = matmul(f"proj_dw{l}", sv['h'], dproj, 'tn', bf16, tn=896)
    (dx0,), (gr['norm_mix_pre'],) = rowwise_bwd(f"norm_mix_pre_bwd{l}", fn_norm, [(sv['x0'], D_MODEL, 0)], [p['g_mix_pre']],
                                                [(dh, D_MODEL, 0)], [f32], ROW_TILE, addto=(dx1, D_MODEL, 0))
    return dx0, gr, dlb


def kernel(x, lower_bounds, norm_mix_pre, norm_mix_post, norm_ff_pre, norm_ff_post, w_in, w_out, hgrn_norm_w, gdn_conv_w, gdn_a_log, gdn_dt_bias, gdn_norm_w, gmlp_ln_w, gmlp_ln_b, gmlp_w_s, gmlp_b_s, conv_dw_w, conv_dw_b, conv_ln_w, conv_ln_b, w_ff1, w_ff2, loss_target, m_lower_bounds, m_norm_mix_pre, m_norm_mix_post, m_norm_ff_pre, m_norm_ff_post, m_w_in, m_w_out, m_hgrn_norm_w, m_gdn_conv_w, m_gdn_a_log, m_gdn_dt_bias, m_gdn_norm_w, m_gmlp_ln_w, m_gmlp_ln_b, m_gmlp_w_s, m_gmlp_b_s, m_conv_dw_w, m_conv_dw_b, m_conv_ln_w, m_conv_ln_b, m_w_ff1, m_w_ff2, v_lower_bounds, v_norm_mix_pre, v_norm_mix_post, v_norm_ff_pre, v_norm_ff_post, v_w_in, v_w_out, v_hgrn_norm_w, v_gdn_conv_w, v_gdn_a_log, v_gdn_dt_bias, v_gdn_norm_w, v_gmlp_ln_w, v_gmlp_ln_b, v_gmlp_w_s, v_gmlp_b_s, v_conv_dw_w, v_conv_dw_b, v_conv_ln_w, v_conv_ln_b, v_w_ff1, v_w_ff2):
    loc = dict(locals())
    W = {n: loc[n] for n in WEIGHTS}
    M = {n: loc['m_' + n] for n in WEIGHTS}
    V = {n: loc['v_' + n] for n in WEIGHTS}
    t = x.shape[1]
    me = 4 * lax.axis_index("x") + 2 * lax.axis_index("y") + lax.axis_index("c")

    shards = [w_in.astype(bf16), w_out.astype(bf16), w_ff1.astype(bf16), w_ff2.astype(bf16), gdn_conv_w, conv_dw_w]
    g_in, g_out, g_ff1, g_ff2, g_gconv, g_dconv = gather_two_level("gather_weights", shards)
    w_in_full = jnp.moveaxis(g_in, 0, 2).reshape(DEPTH, D_MODEL, D_IN)
    w_in_pad = jnp.concatenate([w_in_full[:, :, :8 * GW], w_in_full[:, :, 8 * GW + 2 * NH:],
                                w_in_full[:, :, 8 * GW:8 * GW + 2 * NH],
                                jnp.zeros((DEPTH, D_MODEL, 128 - 2 * NH), bf16)], axis=2)
    w_out_full = jnp.moveaxis(g_out, 0, 1).reshape(DEPTH, D_MODEL, D_MODEL)
    w_ff1_full = jnp.moveaxis(g_ff1, 0, 2).reshape(DEPTH, D_MODEL, D_FF)
    w_ff2_full = jnp.moveaxis(g_ff2, 0, 1).reshape(DEPTH, D_FF, D_MODEL)
    gconv_full = jnp.moveaxis(g_gconv, 0, 2).reshape(DEPTH, SHORT_CONV, 3 * GW)
    dconv_full = jnp.moveaxis(g_dconv, 0, 2).reshape(DEPTH, CONV_WIDTH, GW)

    (lb_all,) = rowwise("lower_bounds", fn_lb, [(lower_bounds, GW, 0)], [], [(GW, f32)], DEPTH)

    P = []
    for l in range(DEPTH):
        P.append(dict(
            g_mix_pre=_row(norm_mix_pre[l]), g_mix_post=_row(norm_mix_post[l]), g_ff_pre=_row(norm_ff_pre[l]),
            g_ff_post=_row(norm_ff_post[l]), w_in=w_in_pad[l], w_out=w_out_full[l], w_ff1=w_ff1_full[l],
            w_ff2=w_ff2_full[l], hgrn_norm_w=_row(hgrn_norm_w[l]), gdn_conv_w=gconv_full[l],
            alog=_pad_lanes(gdn_a_log[l], NH), dtb=_pad_lanes(gdn_dt_bias[l], NH), gdn_norm_w=_row(gdn_norm_w[l]),
            gmlp_ln_w=_row(gmlp_ln_w[l]), gmlp_ln_b=_row(gmlp_ln_b[l]), gmlp_w_s=gmlp_w_s[l].reshape(NH * MIX_CHUNK, MIX_CHUNK),
            gmlp_b_s=gmlp_b_s[l], conv_dw_w=dconv_full[l], conv_dw_b=_row(conv_dw_b[l]), conv_ln_w=_row(conv_ln_w[l]),
            conv_ln_b=_row(conv_ln_b[l])))

    xs = x[0]
    saved = []
    for l in range(DEPTH):
        sv = _layer_fwd(l, xs, P[l], lb_all)
        saved.append(sv)
        if l < DEPTH - 1:
            (xs,) = rowwise(f"res_ff{l}", lambda a, y, gg: (a + _rms(y, gg),), [(sv['x1'], D_MODEL, 0), (sv['y2'], D_MODEL, 0)],
                            [P[l]['g_ff_post']], [(D_MODEL, f32)], ROW_TILE)
    sv = saved[-1]
    dx, loss_loc = final_loss("final_loss", sv['x1'], sv['y2'], P[-1]['g_ff_post'], loss_target[0])
    loss = lax.psum(loss_loc[0, 0], ("x", "y", "c"))

    G = {}
    dlb_rows = []
    for l in reversed(range(DEPTH)):
        dx, gr, dlb = _layer_bwd(l, dx, saved[l], P[l], lb_all)
        G[l] = gr
        dlb_rows.append(dlb)
    dlb_all = jnp.concatenate(dlb_rows[::-1], axis=0)
    (g_lower_bounds,), _ = rowwise_bwd("lower_bounds_bwd", fn_lb, [(lower_bounds, GW, 0)], [], [(dlb_all, GW, 0)],
                                       [f32], DEPTH)
    grad_x = dx[None]

    def stack(name, f=lambda a: a):
        return jnp.stack([f(G[l][name]) for l in range(DEPTH)], axis=0)

    full = {
        'lower_bounds': g_lower_bounds,
        'norm_mix_pre': stack('norm_mix_pre', lambda a: a[0]), 'norm_mix_post': stack('norm_mix_post', lambda a: a[0]),
        'norm_ff_pre': stack('norm_ff_pre', lambda a: a[0]), 'norm_ff_post': stack('norm_ff_post', lambda a: a[0]),
        'hgrn_norm_w': stack('hgrn_norm_w', lambda a: a[0]), 'gdn_a_log': stack('gdn_a_log'), 'gdn_dt_bias': stack('gdn_dt_bias'),
        'gdn_norm_w': stack('gdn_norm_w', lambda a: a[0]), 'gmlp_ln_w': stack('gmlp_ln_w', lambda a: a[0]),
        'gmlp_ln_b': stack('gmlp_ln_b', lambda a: a[0]),
        'gmlp_w_s': stack('gmlp_w_s', lambda a: a.reshape(NH, MIX_CHUNK, MIX_CHUNK)), 'gmlp_b_s': stack('gmlp_b_s'),
        'conv_dw_b': stack('conv_dw_b', lambda a: a[0]), 'conv_ln_w': stack('conv_ln_w', lambda a: a[0]),
        'conv_ln_b': stack('conv_ln_b', lambda a: a[0]),
        'gdn_conv_w': stack('gdn_conv_w'), 'conv_dw_w': stack('conv_dw_w'),
    }

    small_names = list(REPLICATED) + ['gdn_conv_w', 'conv_dw_w']
    flat = jnp.concatenate([full[n].reshape(-1) for n in small_names])
    n_small = flat.shape[0]
    n_pad = -(-n_small // 1024) * 1024
    packed = jnp.pad(flat, (0, n_pad - n_small)).reshape(n_pad // 128, 128)

    def slots_in(l):
        gp = G[l]['w_in']
        gl = jnp.concatenate([gp[:, :8 * GW], gp[:, 12 * GW:12 * GW + 2 * NH], gp[:, 8 * GW:12 * GW]], axis=1)
        return jnp.transpose(gl.reshape(D_MODEL, N_DEV // 2, 2, D_IN // N_DEV), (2, 1, 0, 3))

    send = []
    for l in range(DEPTH):
        send += [slots_in(l), G[l]['w_out'], G[l]['w_ff1'], G[l]['w_ff2']]
    (small_slots,) = exchange("gather_small_grads", [packed], True, 'all')
    from_sibling = send_to_sibling("scatter_grads_d2d", send)
    partial = [chip_sum(f"chip_sum{i}", b, o) for i, (b, o) in enumerate(zip(send, from_sibling))]
    recv = exchange("scatter_grads_ici", partial, False, 'chips')

    out = {}
    for j, name in enumerate(('w_in', 'w_out', 'w_ff1', 'w_ff2')):
        out[name] = adamw(f"adamw_{name}", W[name], M[name], V[name], [recv[4 * l + j] for l in range(DEPTH)])

    def pack(d, fill):
        parts = [d[n].reshape(-1) for n in REPLICATED]
        parts.append(jnp.full((n_pad - sum(a.shape[0] for a in parts),), fill, f32))
        return jnp.concatenate(parts).reshape(1, n_pad // 128, 128)

    sm = adamw("adamw_small", pack(W, 0.0), pack(M, 0.0), pack(V, 1.0), [small_slots], tr=n_pad // 128)
    off = 0
    for n in REPLICATED:
        sz = W[n].size
        out[n] = tuple(a.reshape(-1)[off:off + sz].reshape(W[n].shape) for a in sm)
        off += sz
    gsum = sm[0].reshape(-1)
    for n, full_shape in (('gdn_conv_w', (DEPTH, SHORT_CONV, 3 * GW)), ('conv_dw_w', (DEPTH, CONV_WIDTH, GW))):
        sz = math.prod(full_shape)
        gfull = gsum[off:off + sz].reshape(full_shape)
        off += sz
        sh = W[n].shape
        gmine = lax.dynamic_slice_in_dim(gfull, me * sh[2], sh[2], axis=2)
        r = adamw(f"adamw_{n}", W[n].reshape(1, sh[0] * sh[1], sh[2]), M[n].reshape(1, sh[0] * sh[1], sh[2]),
                  V[n].reshape(1, sh[0] * sh[1], sh[2]), [gmine.reshape(1, sh[0] * sh[1], sh[2])], tr=sh[0] * sh[1])
        out[n] = tuple(a.reshape(sh) for a in r)

    return (loss, grad_x, *[out[n][0] for n in WEIGHTS], *[out[n][1] for n in WEIGHTS],
            *[out[n][2] for n in WEIGHTS], *[out[n][3] for n in WEIGHTS])
```

```python
import functools
import math

import jax
import jax.numpy as jnp
from jax import lax
from jax.experimental import pallas as pl
from jax.experimental.pallas import tpu as pltpu

f32 = jnp.float32
bf16 = jnp.bfloat16
HI = lax.Precision.HIGHEST

N_DEV = 8
DEPTH = 2
D_MODEL = 2048
GW = 512
HD = 128
NH = 4
CHUNK = 64
MIX_CHUNK = 128
CONV_WIDTH = 31
SHORT_CONV = 4
D_FF = 4 * D_MODEL
D_IN = 12 * GW + 2 * NH
D_IN_PAD = 12 * GW + 128
ROW_TILE = 256
HGRN_SUB = 16
EPS = 1e-6
TINY = 1e-30
ADAM_LR, ADAM_B1, ADAM_B2, ADAM_EPS, ADAM_WD, ADAM_STEP = 0.001, 0.9, 0.999, 1e-08, 0.01, 10
MESH = pl.DeviceIdType.MESH


def _dotb(a, b, ca, cb):
    return lax.dot_general(a.astype(bf16), b.astype(bf16), (((ca,), (cb,)), ((), ())),
                           preferred_element_type=f32)


@jax.custom_vjp
def mm(a, b):
    return _dotb(a, b, 1, 0)


def _mm_f(a, b):
    return mm(a, b), (a, b)


def _mm_b(res, ct):
    a, b = res
    return _dotb(ct, b, 1, 1), _dotb(a, ct, 0, 0)


mm.defvjp(_mm_f, _mm_b)


@jax.custom_vjp
def mm_nt(a, b):
    return _dotb(a, b, 1, 1)


def _mmnt_f(a, b):
    return mm_nt(a, b), (a, b)


def _mmnt_b(res, ct):
    a, b = res
    return _dotb(ct, b, 1, 0), _dotb(ct, a, 0, 0)


mm_nt.defvjp(_mmnt_f, _mmnt_b)


@jax.custom_vjp
def mm_tn(a, b):
    return _dotb(a, b, 0, 0)


def _mmtn_f(a, b):
    return mm_tn(a, b), (a, b)


def _mmtn_b(res, ct):
    a, b = res
    return _dotb(b, ct, 1, 1), _dotb(a, ct, 1, 0)


mm_tn.defvjp(_mmtn_f, _mmtn_b)


def mmh(a, b):
    return jnp.dot(a, b, precision=HI, preferred_element_type=f32)


def mm3(a, b):
    return jnp.dot(a, b, precision=lax.Precision.HIGH, preferred_element_type=f32)


def _rms(x, w):
    return x * lax.rsqrt(jnp.mean(x * x, axis=-1, keepdims=True) + EPS) * w


def _ln(x, w, b):
    mu = jnp.mean(x, axis=-1, keepdims=True)
    xc = x - mu
    var = jnp.mean(xc * xc, axis=-1, keepdims=True)
    return xc * lax.rsqrt(var + EPS) * w + b


def _gelu(x):
    return 0.5 * x * (1.0 + lax.erf(x * (2.0 ** -0.5)))


def _iota2(n, m, axis):
    return lax.broadcasted_iota(jnp.int32, (n, m), axis)


def _tri(n, strict=False):
    r, c = _iota2(n, n, 0), _iota2(n, n, 1)
    return (r > c) if strict else (r >= c)


def _eye(n):
    return (_iota2(n, n, 0) == _iota2(n, n, 1)).astype(f32)


def fn_norm(x, g):
    return (_rms(x, g),)


def fn_lb(lower_bounds):
    s = jax.nn.softmax(lower_bounds, axis=0)
    rows, cum = [], None
    for i in range(DEPTH):
        cum = s[i:i + 1] if cum is None else cum + s[i:i + 1]
        rows.append(cum - s[0:1])
    return (jnp.concatenate(rows, axis=0),)


def fn_hgrn(aq, af, ai, ag, lb, nw, st):
    c = aq.shape[0]
    sig = jax.nn.sigmoid(af)
    f = lb + (1.0 - lb) * sig
    logf = jnp.log(jnp.maximum(f, TINY))
    k = (1.0 - lb) * jax.nn.sigmoid(-af)
    q = jax.nn.silu(aq)
    v = ai
    b = mmh(_tri(c).astype(f32), logf)
    outs = []
    for lo in range(0, c, HGRN_SUB):
        qi, ki, vi, bi = (a[lo:lo + HGRN_SUB] for a in (q, k, v, b))
        rel = bi[:, None, :] - bi[None, :, :]
        dec = jnp.exp(jnp.minimum(rel, 0.0)) * qi[:, None, :] * ki[None, :, :]
        o_blk = mm(jnp.where(_tri(HGRN_SUB), jnp.sum(dec, axis=-1), 0.0), vi)
        if lo > 0:
            r = b[lo - 1:lo, :]
            o_blk = o_blk + mm(mm_nt(qi * jnp.exp(bi - r), k[:lo] * jnp.exp(r - b[:lo])), v[:lo])
        outs.append(o_blk)
    b_end = b[c - 1:c, :]
    out = jnp.concatenate(outs, axis=0) + mm_nt(q * jnp.exp(b), st)
    st_new = st * jnp.exp(b_end) + mm_tn(v, k * jnp.exp(b_end - b))
    o = _rms(out, nw) * jax.nn.silu(ag)
    return o, st_new


def fn_gdn_pre(tq, tk, tv, bq, bk, bv, p8, conv_w, alog, dtb):
    tile = bq.shape[0]
    h = tq.shape[0]
    outs = []
    for seg, (tl, cur) in enumerate(((tq, bq), (tk, bk), (tv, bv))):
        xe = jnp.concatenate([tl, cur], axis=0)
        acc = None
        for kk in range(SHORT_CONV):
            off = h - (SHORT_CONV - 1) + kk
            term = conv_w[kk:kk + 1, seg * GW:(seg + 1) * GW] * xe[off:off + tile, :]
            acc = term if acc is None else acc + term
        outs.append(jax.nn.silu(acc))
    sq, sk, sv = outs
    qh, kh = [], []
    for hh in range(NH):
        a = sq[:, hh * HD:(hh + 1) * HD]
        qh.append(a * lax.rsqrt(jnp.sum(a * a, axis=-1, keepdims=True) + EPS) * (HD ** -0.5))
        a = sk[:, hh * HD:(hh + 1) * HD]
        kh.append(a * lax.rsqrt(jnp.sum(a * a, axis=-1, keepdims=True) + EPS))
    q = jnp.concatenate(qh, axis=1)
    k = jnp.concatenate(kh, axis=1)
    beta = jax.nn.sigmoid(p8)
    g = -jnp.exp(alog) * jax.nn.softplus(p8 + dtb)
    r, cc = _iota2(128, GW, 0), _iota2(128, GW, 1) // HD
    e_beta = (r == cc).astype(f32)
    e_g = (r == cc + NH).astype(f32)
    return q, k, sv, mmh(beta, e_beta), mmh(g, e_g)


@jax.custom_vjp
def _inverse_given(m, tinv):
    return tinv


def _inverse_given_f(m, tinv):
    return tinv, tinv


def _inverse_given_b(tinv, ct):
    x = lax.dot_general(ct, tinv, (((1,), (1,)), ((), ())), precision=HI, preferred_element_type=f32)
    dm = -lax.dot_general(tinv, x, (((0,), (0,)), ((), ())), precision=HI, preferred_element_type=f32)
    return dm, jnp.zeros_like(tinv)


_inverse_given.defvjp(_inverse_given_f, _inverse_given_b)


def fn_gdn(q, k, v, beta, g, z, nw, s, tinv_saved=None):
    c = q.shape[0]
    gc = mmh(_tri(c).astype(f32), g)
    gcol = gc[:, 0:1]
    grow = jnp.sum(gcol * _eye(c), axis=0, keepdims=True)
    gamma = jnp.where(_tri(c), jnp.exp(jnp.minimum(gcol - grow, 0.0)), 0.0)
    kb = k * beta
    m = jnp.where(_tri(c, strict=True), mm_nt(kb, k) * gamma, 0.0)
    if tinv_saved is None:
        eye = _eye(c)
        tinv = eye - m
        p = m
        for _ in range(int(math.log2(c)) - 1):
            p = mm3(p, p)
            tinv = mm3(tinv, eye + p)
    else:
        tinv = _inverse_given(m, tinv_saved)
    egc = jnp.exp(gc)
    u = mmh(tinv, v * beta)
    w = mmh(tinv, kb * egc)
    qk = mm_nt(q, k) * gamma
    gc_end = gc[c - 1:c, :]
    q_dec = q * egc
    k_dec = k * jnp.exp(gc_end - gc)
    v_new = u - mm(w, s)
    out = mm(q_dec, s) + mm(qk, v_new)
    s_new = s * jnp.exp(gc_end) + mm_tn(k_dec, v_new)
    o = _rms(out, nw) * jax.nn.silu(z)
    return (o, s_new, tinv) if tinv_saved is None else (o, s_new)


def fn_gmlp(cu, cv, ln_w, ln_b, w_s, b_s):
    n = cu.shape[0]
    ug = _gelu(cu)
    vn = _ln(_gelu(cv), ln_w, ln_b)
    eye = _eye(n)
    cols = []
    for hh in range(NH):
        wc = jnp.where(_tri(n), w_s[hh * n:(hh + 1) * n, :], 0.0)
        bcol = jnp.sum(b_s[hh:hh + 1, :] * eye, axis=1, keepdims=True)
        cols.append(mm(wc, vn[:, hh * HD:(hh + 1) * HD]) + bcol)
    return (ug * jnp.concatenate(cols, axis=1),)


def fn_conv(ta, tg, a, gate, dw_w, dw_b, ln_w, ln_b):
    tile = a.shape[0]
    h = ta.shape[0]
    ya = jnp.concatenate([ta, a], axis=0)
    yg = jnp.concatenate([tg, gate], axis=0)
    y = ya * jax.nn.sigmoid(yg)
    acc = None
    for kk in range(CONV_WIDTH):
        off = h - (CONV_WIDTH - 1) + kk
        term = dw_w[kk:kk + 1, :] * y[off:off + tile, :]
        acc = term if acc is None else acc + term
    return (jax.nn.silu(_ln(acc + dw_b, ln_w, ln_b)),)


def _full_spec(arr):
    nd = arr.ndim
    return pl.BlockSpec(arr.shape, lambda *_: (0,) * nd)


def rowwise(name, fn, tiled, params, outs, tile):
    t = tiled[0][0].shape[0]
    tile = min(tile, t)
    nt, np_ = len(tiled), len(params)

    def body(*refs):
        vals = [r[...].astype(f32) for r in refs[:nt + np_]]
        res = fn(*vals)
        for o_ref, r in zip(refs[nt + np_:], res):
            o_ref[...] = r.astype(o_ref.dtype)

    in_specs = [pl.BlockSpec((tile, w), lambda i, c=c: (i, c)) for _, w, c in tiled]
    in_specs += [_full_spec(p) for p in params]
    out_specs = [pl.BlockSpec((tile, w), lambda i: (i, 0)) for w, _ in outs]
    out_shape = [jax.ShapeDtypeStruct((t, w), dt) for w, dt in outs]
    return pl.pallas_call(body, grid=(t // tile,), in_specs=in_specs, out_specs=out_specs,
                          out_shape=out_shape, name=name)(*[a for a, _, _ in tiled], *params)


def rowwise_bwd(name, fn, tiled, params, cots, gouts, tile, addto=None):
    t = tiled[0][0].shape[0]
    tile = min(tile, t)
    nt, np_, nc = len(tiled), len(params), len(cots)
    na = 0 if addto is None else 1
    gidx = [i for i, g in enumerate(gouts) if g is not None]

    def body(*refs):
        i = pl.program_id(0)
        vals = [r[...].astype(f32) for r in refs[:nt + np_]]
        cvals = tuple(r[...].astype(f32) for r in refs[nt + np_:nt + np_ + nc])
        _, vjp = jax.vjp(fn, *vals)
        grads = vjp(cvals)
        orefs = refs[nt + np_ + nc + na:]
        for n, j in enumerate(gidx):
            g = grads[j]
            if na and n == 0:
                g = g + refs[nt + np_ + nc][...].astype(f32)
            orefs[n][...] = g.astype(orefs[n].dtype)
        prefs = orefs[len(gidx):]

        @pl.when(i == 0)
        def _():
            for r in prefs:
                r[...] = jnp.zeros_like(r)

        for r, g in zip(prefs, grads[nt:]):
            r[...] += g

    in_specs = [pl.BlockSpec((tile, w), lambda i, c=c: (i, c)) for _, w, c in tiled]
    in_specs += [_full_spec(p) for p in params]
    in_specs += [pl.BlockSpec((tile, w), lambda i, c=c: (i, c)) for _, w, c in cots]
    args = [a for a, _, _ in tiled] + list(params) + [a for a, _, _ in cots]
    if na:
        in_specs.append(pl.BlockSpec((tile, addto[1]), lambda i, c=addto[2]: (i, c)))
        args.append(addto[0])
    out_specs = [pl.BlockSpec((tile, tiled[j][1]), lambda i: (i, 0)) for j in gidx]
    out_shape = [jax.ShapeDtypeStruct((t, tiled[j][1]), gouts[j]) for j in gidx]
    out_specs += [_full_spec(p) for p in params]
    out_shape += [jax.ShapeDtypeStruct(p.shape, f32) for p in params]
    res = pl.pallas_call(body, grid=(t // tile,), in_specs=in_specs, out_specs=out_specs,
                         out_shape=out_shape, name=name)(*args)
    return res[:len(gidx)], res[len(gidx):]


def halo_fwd(name, fn, tiled, params, outs, tile, halo):
    t = tiled[0][0].shape[0]
    tile = min(tile, t)
    hal = [j for j, x in enumerate(tiled) if x[3]]
    nt, nh, np_ = len(tiled), len(hal), len(params)
    per = tile // halo

    def body(*refs):
        i = pl.program_id(0)
        first = (i > 0).astype(f32)
        tails = [r[...].astype(f32) * first for r in refs[:nh]]
        vals = [r[...].astype(f32) for r in refs[nh:nh + nt + np_]]
        res = fn(*tails, *vals)
        for o_ref, r in zip(refs[nh + nt + np_:], res):
            o_ref[...] = r.astype(o_ref.dtype)

    in_specs = [pl.BlockSpec((halo, tiled[j][1]), lambda i, c=tiled[j][2]: (jnp.maximum(i * per - 1, 0), c))
                for j in hal]
    in_specs += [pl.BlockSpec((tile, w), lambda i, c=c: (i, c)) for _, w, c, _ in tiled]
    in_specs += [_full_spec(p) for p in params]
    out_specs = [pl.BlockSpec((tile, w), lambda i: (i, 0)) for w, _ in outs]
    out_shape = [jax.ShapeDtypeStruct((t, w), dt) for w, dt in outs]
    args = [tiled[j][0] for j in hal] + [x[0] for x in tiled] + list(params)
    return pl.pallas_call(body, grid=(t // tile,), in_specs=in_specs, out_specs=out_specs,
                          out_shape=out_shape, name=name)(*args)


def halo_bwd(name, fn, tiled, params, cots, gdtype, tile, halo):
    t = tiled[0][0].shape[0]
    tile = min(tile, t)
    hal = [j for j, x in enumerate(tiled) if x[3]]
    nt, nh, np_, nc = len(tiled), len(hal), len(params), len(cots)
    per = tile // halo
    n_tiles = t // tile

    def body(*refs):
        s = pl.program_id(0)
        i = n_tiles - 1 - s
        first = (i > 0).astype(f32)
        tails = [r[...].astype(f32) * first for r in refs[:nh]]
        vals = [r[...].astype(f32) for r in refs[nh:nh + nt + np_]]
        cvals = tuple(r[...].astype(f32) for r in refs[nh + nt + np_:nh + nt + np_ + nc])
        n_in = nh + nt + np_ + nc
        orefs = refs[n_in:n_in + nt]
        prefs = refs[n_in + nt:n_in + nt + np_]
        carries = refs[n_in + nt + np_:]

        @pl.when(s == 0)
        def _():
            for r in prefs:
                r[...] = jnp.zeros_like(r)
            for r in carries:
                r[...] = jnp.zeros_like(r)

        _, vjp = jax.vjp(fn, *tails, *vals)
        grads = vjp(cvals)
        for j in range(nt):
            g = grads[nh + j]
            if j in hal:
                cr = carries[hal.index(j)]
                g = jnp.concatenate([g[:tile - halo], g[tile - halo:] + cr[...]], axis=0)
            orefs[j][...] = g.astype(orefs[j].dtype)
        for n in range(nh):
            carries[n][...] = grads[n] * first
        for r, g in zip(prefs, grads[nh + nt:]):
            r[...] += g

    rev = lambda s: n_tiles - 1 - s
    in_specs = [pl.BlockSpec((halo, tiled[j][1]),
                             lambda s, c=tiled[j][2]: (jnp.maximum(rev(s) * per - 1, 0), c)) for j in hal]
    in_specs += [pl.BlockSpec((tile, w), lambda s, c=c: (rev(s), c)) for _, w, c, _ in tiled]
    in_specs += [_full_spec(p) for p in params]
    in_specs += [pl.BlockSpec((tile, w), lambda s, c=c: (rev(s), c)) for _, w, c in cots]
    out_specs = [pl.BlockSpec((tile, w), lambda s: (rev(s), 0)) for _, w, _, _ in tiled]
    out_shape = [jax.ShapeDtypeStruct((t, w), gdtype) for _, w, _, _ in tiled]
    out_specs += [_full_spec(p) for p in params]
    out_shape += [jax.ShapeDtypeStruct(p.shape, f32) for p in params]
    scratch = [pltpu.VMEM((halo, tiled[j][1]), f32) for j in hal]
    args = [tiled[j][0] for j in hal] + [x[0] for x in tiled] + list(params) + [a for a, _, _ in cots]
    res = pl.pallas_call(body, grid=(n_tiles,), in_specs=in_specs, out_specs=out_specs,
                         out_shape=out_shape, scratch_shapes=scratch, name=name)(*args)
    return res[:nt], res[nt:]


def scan_fwd(name, fn, tiled, pparams, sparams, out_dtype, hb, n_extra=0):
    t = tiled[0][0].shape[0]
    n = t // CHUNK
    nt, npp, nsp = len(tiled), len(pparams), len(sparams)
    w = HD * hb

    def body(*refs):
        c = pl.program_id(1)
        n_in = nt + npp + nsp
        o_ref, sv_ref = refs[n_in], refs[n_in + 1]
        ex_refs, st = refs[n_in + 2:n_in + 2 + n_extra], refs[n_in + 2 + n_extra]

        @pl.when(c == 0)
        def _():
            st[...] = jnp.zeros_like(st)

        vals = [r[...].astype(f32) for r in refs[:n_in]]
        outs = []
        for hh in range(hb):
            hv = [v[:, hh * HD:(hh + 1) * HD] for v in vals[:nt + npp]] + vals[nt + npp:]
            s_in = st[hh]
            sv_ref[hh] = s_in
            res = fn(*hv, s_in)
            st[hh] = res[1]
            outs.append(res[0])
            for e_ref, e in zip(ex_refs, res[2:]):
                e_ref[hh] = e
        o_ref[...] = (outs[0] if hb == 1 else jnp.concatenate(outs, axis=1)).astype(o_ref.dtype)

    in_specs = [pl.BlockSpec((CHUNK, w), lambda g, c, b=b: (c, b // hb + g)) for _, b in tiled]
    in_specs += [pl.BlockSpec((1, w), lambda g, c: (0, g)) for _ in pparams]
    in_specs += [_full_spec(p) for p in sparams]
    out_specs = [pl.BlockSpec((CHUNK, w), lambda g, c: (c, g)),
                 pl.BlockSpec((hb, None, HD, HD), lambda g, c: (g, c, 0, 0))]
    out_shape = [jax.ShapeDtypeStruct((t, GW), out_dtype), jax.ShapeDtypeStruct((NH, n, HD, HD), f32)]
    out_specs += [pl.BlockSpec((hb, None, CHUNK, CHUNK), lambda g, c: (g, c, 0, 0))] * n_extra
    out_shape += [jax.ShapeDtypeStruct((NH, n, CHUNK, CHUNK), f32)] * n_extra
    return pl.pallas_call(body, grid=(NH // hb, n), in_specs=in_specs, out_specs=out_specs, out_shape=out_shape,
                          scratch_shapes=[pltpu.VMEM((hb, HD, HD), f32)], name=name)(
        *[a for a, _ in tiled], *pparams, *sparams)


def scan_bwd(name, fn, tiled, pparams, sparams, states, cot, gdtypes, hb, extras=()):
    t = tiled[0][0].shape[0]
    n = t // CHUNK
    nt, npp, nsp, nex = len(tiled), len(pparams), len(sparams), len(extras)
    w = HD * hb

    def body(*refs):
        g, s = pl.program_id(0), pl.program_id(1)
        n_in = nt + npp + nsp
        vals = [r[...].astype(f32) for r in refs[:n_in]]
        st_ref = refs[n_in]
        do = refs[n_in + 1][...].astype(f32)
        ex_refs = refs[n_in + 2:n_in + 2 + nex]
        n_op = n_in + 2 + nex
        orefs = refs[n_op:n_op + nt]
        pprefs = refs[n_op + nt:n_op + nt + npp]
        sprefs = refs[n_op + nt + npp:n_op + nt + npp + nsp]
        ds = refs[n_op + nt + npp + nsp]

        @pl.when(s == 0)
        def _():
            ds[...] = jnp.zeros_like(ds)
            for r in pprefs:
                r[...] = jnp.zeros_like(r)

        @pl.when((s == 0) & (g == 0))
        def _():
            for r in sprefs:
                r[...] = jnp.zeros_like(r)

        per_head = []
        for hh in range(hb):
            sl = slice(hh * HD, (hh + 1) * HD)
            hv = [v[:, sl] for v in vals[:nt + npp]] + vals[nt + npp:]
            ex = [r[hh] for r in ex_refs]
            _, vjp = jax.vjp(lambda *a: fn(*a, *ex), *hv, st_ref[hh])
            grads = vjp((do[:, sl], ds[hh]))
            ds[hh] = grads[n_in]
            per_head.append(grads)
        cat = lambda j: per_head[0][j] if hb == 1 else jnp.concatenate([gr[j] for gr in per_head], axis=1)
        for j in range(nt):
            orefs[j][...] = cat(j).astype(orefs[j].dtype)
        for j, r in enumerate(pprefs):
            r[...] += cat(nt + j)
        for j, r in enumerate(sprefs):
            tot = per_head[0][nt + npp + j]
            for gr in per_head[1:]:
                tot = tot + gr[nt + npp + j]
            r[...] += tot

    rev = lambda s: n - 1 - s
    in_specs = [pl.BlockSpec((CHUNK, w), lambda g, s, b=b: (rev(s), b // hb + g)) for _, b in tiled]
    in_specs += [pl.BlockSpec((1, w), lambda g, s: (0, g)) for _ in pparams]
    in_specs += [_full_spec(p) for p in sparams]
    in_specs += [pl.BlockSpec((hb, None, HD, HD), lambda g, s: (g, rev(s), 0, 0)),
                 pl.BlockSpec((CHUNK, w), lambda g, s, b=cot[1]: (rev(s), b // hb + g))]
    in_specs += [pl.BlockSpec((hb, None, CHUNK, CHUNK), lambda g, s: (g, rev(s), 0, 0)) for _ in extras]
    out_specs = [pl.BlockSpec((CHUNK, w), lambda g, s: (rev(s), g)) for _ in tiled]
    out_shape = [jax.ShapeDtypeStruct((t, GW), dt) for dt in gdtypes]
    out_specs += [pl.BlockSpec((1, w), lambda g, s: (0, g)) for _ in pparams]
    out_shape += [jax.ShapeDtypeStruct(p.shape, f32) for p in pparams]
    out_specs += [_full_spec(p) for p in sparams]
    out_shape += [jax.ShapeDtypeStruct(p.shape, f32) for p in sparams]
    res = pl.pallas_call(body, grid=(NH // hb, n), in_specs=in_specs, out_specs=out_specs, out_shape=out_shape,
                         scratch_shapes=[pltpu.VMEM((hb, HD, HD), f32)], name=name)(
        *[a for a, _ in tiled], *pparams, *sparams, states, cot[0], *extras)
    return res[:nt], res[nt:nt + npp], res[nt + npp:]


def matmul(name, a, b, mode, out_dtype=f32, tm=1024, tn=1024, tk=2048, epilogue=None, extra=None, slots=None,
           b_gathered=None):
    if b_gathered is not None:
        cut, layer = b_gathered
        _, _, sr, sc = b.shape
        b_rows, b_cols = (N_DEV * sr, sc) if cut == 'rows' else (sr, N_DEV * sc)
    else:
        b_rows, b_cols = b.shape
    if mode == 'nn':
        (m, k), n = a.shape, b_cols
    elif mode == 'nt':
        (m, k), n = a.shape, b_rows
    else:
        (k, m), n = a.shape, b_cols
    tm, tn, tk = min(tm, m), min(tn, n), min(tk, k)
    if b_gathered is not None:
        if (mode == 'nn') == (cut == 'cols'):
            tn = min(tn, sc if cut == 'cols' else sr)
        else:
            tk = min(tk, sr if cut == 'rows' else sc)
    if slots == 'rows':
        tm = min(tm, m // N_DEV)
    if slots == 'cols':
        tn = min(tn, n // N_DEV)
    nk = k // tk
    ca, cb = {'nn': (1, 0), 'nt': (1, 1), 'tn': (0, 0)}[mode]

    def finish(refs, r):
        if epilogue == 'relu2':
            refs[2][...] = r
            refs[3][...] = jnp.square(jnp.maximum(r, 0.0)).astype(bf16)
        elif epilogue == 'relu2_bwd':
            refs[3][...] = (r * 2.0 * jnp.maximum(refs[2][...], 0.0)).astype(refs[3].dtype)
        else:
            refs[2][...] = r.astype(refs[2].dtype)

    def body(*refs):
        part = _dotb(refs[0][...], refs[1][...], ca, cb)
        if nk == 1:
            finish(refs, part)
            return
        acc = refs[-1]
        kk = pl.program_id(2)

        @pl.when(kk == 0)
        def _():
            acc[...] = part

        @pl.when(kk > 0)
        def _():
            acc[...] += part

        @pl.when(kk == nk - 1)
        def _():
            finish(refs, acc[...])

    if mode == 'nn':
        a_spec = pl.BlockSpec((tm, tk), lambda i, j, kk: (i, kk))
        b_spec = pl.BlockSpec((tk, tn), lambda i, j, kk: (kk, j))
    elif mode == 'nt':
        a_spec = pl.BlockSpec((tm, tk), lambda i, j, kk: (i, kk))
        b_spec = pl.BlockSpec((tn, tk), lambda i, j, kk: (j, kk))
    else:
        a_spec = pl.BlockSpec((tk, tm), lambda i, j, kk: (kk, i))
        b_spec = pl.BlockSpec((tk, tn), lambda i, j, kk: (kk, j))
    if b_gathered is not None:
        bshape = (None, None, tk, tn) if mode == 'nn' else (None, None, tn, tk)
        if mode == 'nn' and cut == 'cols':
            per = sc // tn
            b_spec = pl.BlockSpec(bshape, lambda i, j, kk: (j // per, layer, kk, j % per))
        elif mode == 'nn':
            per = sr // tk
            b_spec = pl.BlockSpec(bshape, lambda i, j, kk: (kk // per, layer, kk % per, j))
        elif cut == 'cols':
            per = sc // tk
            b_spec = pl.BlockSpec(bshape, lambda i, j, kk: (kk // per, layer, j, kk % per))
        else:
            per = sr // tn
            b_spec = pl.BlockSpec(bshape, lambda i, j, kk: (j // per, layer, j % per, kk))
    o_spec = pl.BlockSpec((tm, tn), lambda i, j, kk: (i, j))
    in_specs, args = [a_spec, b_spec], [a, b]
    if epilogue == 'relu2':
        out_specs = [o_spec, o_spec]
        out_shape = [jax.ShapeDtypeStruct((m, n), f32), jax.ShapeDtypeStruct((m, n), bf16)]
    elif slots == 'rows':
        per = (m // N_DEV) // tm
        out_specs = pl.BlockSpec((None, None, tm, tn), lambda i, j, kk: ((i // per) % 2, (i // per) // 2, i % per, j))
        out_shape = jax.ShapeDtypeStruct((2, N_DEV // 2, m // N_DEV, n), out_dtype)
    elif slots == 'cols':
        per = (n // N_DEV) // tn
        out_specs = pl.BlockSpec((None, None, tm, tn), lambda i, j, kk: ((j // per) % 2, (j // per) // 2, i, j % per))
        out_shape = jax.ShapeDtypeStruct((2, N_DEV // 2, m, n // N_DEV), out_dtype)
    else:
        out_specs, out_shape = o_spec, jax.ShapeDtypeStruct((m, n), out_dtype)
        if epilogue == 'relu2_bwd':
            in_specs.append(o_spec)
            args.append(extra)
    scratch = [pltpu.VMEM((tm, tn), f32)] if nk > 1 else []
    return pl.pallas_call(body, grid=(m // tm, n // tn, nk), in_specs=in_specs, out_specs=out_specs,
                          out_shape=out_shape, scratch_shapes=scratch, name=name)(*args)


def final_loss(name, x, y, g, target):
    t, d = x.shape
    tile = min(ROW_TILE, t)

    def body(x_ref, y_ref, g_ref, t_ref, dx_ref, l_ref):
        i = pl.program_id(0)

        @pl.when(i == 0)
        def _():
            l_ref[...] = jnp.zeros_like(l_ref)

        err = x_ref[...] + _rms(y_ref[...], g_ref[...]) - t_ref[...]
        dx_ref[...] = err * (1.0 / d)
        l_ref[...] += 0.5 * jnp.sum(jnp.mean(err * err, axis=-1, keepdims=True), axis=0, keepdims=True)

    row = pl.BlockSpec((tile, d), lambda i: (i, 0))
    return pl.pallas_call(
        body, grid=(t // tile,), in_specs=[row, row, _full_spec(g), row],
        out_specs=[row, pl.BlockSpec((1, 1), lambda i: (0, 0))],
        out_shape=[jax.ShapeDtypeStruct((t, d), f32), jax.ShapeDtypeStruct((1, 1), f32)], name=name)(x, y, g, target)


def adamw(name, w, m, v, gslots, tr=128):
    nl, r, c = w.shape
    tr = min(tr, r)
    nr = r // tr
    ns = gslots[0].shape[0]
    c1 = 1.0 / (1.0 - ADAM_B1 ** ADAM_STEP)
    c2 = 1.0 / (1.0 - ADAM_B2 ** ADAM_STEP)

    def body(*refs):
        w_ref, m_ref, v_ref = refs[:3]
        g_refs = refs[3:3 + nl]
        go_ref, d_ref, mo_ref, vo_ref = refs[3 + nl:]
        l = pl.program_id(0)
        g = None
        for li in range(nl):
            s = g_refs[li][0].astype(f32)
            for k in range(1, ns):
                s = s + g_refs[li][k].astype(f32)
            g = s if g is None else jnp.where(l == li, s, g)
        mn = ADAM_B1 * m_ref[...] + (1.0 - ADAM_B1) * g
        vn = ADAM_B2 * v_ref[...] + (1.0 - ADAM_B2) * jnp.square(g)
        go_ref[...] = g
        mo_ref[...] = mn
        vo_ref[...] = vn
        d_ref[...] = -ADAM_LR * ((mn * c1) / (jnp.sqrt(vn * c2) + ADAM_EPS) + ADAM_WD * w_ref[...])

    blk = pl.BlockSpec((None, tr, c), lambda l, i: (l, i, 0))

    def gspec(li):
        return pl.BlockSpec((ns, tr, c), lambda l, i: (0, jnp.where(l == li, i, jnp.where(l < li, 0, nr - 1)), 0))

    return pl.pallas_call(
        body, grid=(nl, nr), in_specs=[blk, blk, blk] + [gspec(li) for li in range(nl)],
        out_specs=[blk] * 4, out_shape=[jax.ShapeDtypeStruct(w.shape, f32)] * 4, name=name)(w, m, v, *gslots)


def exchange(name, arrays, gather, group):
    n = len(arrays)
    ns = {'all': 8, 'chips': 4, 'core': 2}[group]

    def body(*refs):
        ins, outs = refs[:n], refs[n:2 * n]
        send_sems, recv_sems, loc_sems = refs[2 * n:]
        x, y, c = lax.axis_index("x"), lax.axis_index("y"), lax.axis_index("c")

        def member(k):
            if group == 'all':
                px, py, pc = x ^ ((k >> 2) & 1), y ^ ((k >> 1) & 1), c ^ (k & 1)
                return (px, py, pc), 4 * px + 2 * py + pc
            if group == 'chips':
                px, py = x ^ ((k >> 1) & 1), y ^ (k & 1)
                return (px, py, c), 2 * px + py
            return (x, y, c ^ k), c ^ k

        _, me = member(0)
        sends, recvs, locs = [], [], []
        for a in range(n):
            lc = pltpu.make_async_copy(ins[a] if gather else ins[a].at[me], outs[a].at[me], loc_sems.at[a])
            lc.start()
            locs.append(lc)
            for k in range(1, ns):
                dev, peer = member(k)
                src = ins[a] if gather else ins[a].at[peer]
                cp = pltpu.make_async_remote_copy(src_ref=src, dst_ref=outs[a].at[me], send_sem=send_sems.at[a, k],
                                                  recv_sem=recv_sems.at[a, k], device_id=dev, device_id_type=MESH)
                cp.start()
                sends.append(cp)
                recvs.append(pltpu.make_async_remote_copy(src_ref=src, dst_ref=outs[a].at[peer], send_sem=send_sems.at[a, k],
                                                          recv_sem=recv_sems.at[a, k], device_id=dev, device_id_type=MESH))
        for cp in recvs:
            cp.wait_recv()
        for cp in sends:
            cp.wait_send()
        for lc in locs:
            lc.wait()

    anyspec = pl.BlockSpec(memory_space=pl.ANY)
    out_shape = [jax.ShapeDtypeStruct(((ns,) + a.shape) if gather else a.shape, a.dtype) for a in arrays]
    return pl.pallas_call(
        body, in_specs=[anyspec] * n, out_specs=[anyspec] * n, out_shape=out_shape,
        scratch_shapes=[pltpu.SemaphoreType.DMA((n, ns)), pltpu.SemaphoreType.DMA((n, ns)),
                        pltpu.SemaphoreType.DMA((n,))], name=name)(*arrays)


def gather_two_level(name, arrays):
    n = len(arrays)

    def body(*refs):
        ins, outs = refs[:n], refs[n:2 * n]
        send_sems, recv_sems, loc_sems = refs[2 * n:]
        x, y, c = lax.axis_index("x"), lax.axis_index("y"), lax.axis_index("c")
        sib = (x, y, 1 - c)

        def chip(k):
            px, py = x ^ ((k >> 1) & 1), y ^ (k & 1)
            return (px, py), 2 * px + py

        _, mine = chip(0)

        def copy(a, sem, src, slot, to):
            return pltpu.make_async_remote_copy(src_ref=src, dst_ref=outs[a].at[slot], send_sem=send_sems.at[a, sem],
                                                recv_sem=recv_sems.at[a, sem], device_id=to, device_id_type=MESH)

        sends, locs = [], []
        for a in range(n):
            lc = pltpu.make_async_copy(ins[a], outs[a].at[2 * mine + c], loc_sems.at[a])
            lc.start()
            locs.append(lc)
            sends.append(copy(a, 0, ins[a], 2 * mine + c, sib))
            for k in range(1, 4):
                (px, py), _ = chip(k)
                sends.append(copy(a, k, ins[a], 2 * mine + c, (px, py, c)))
        for cp in sends:
            cp.start()
        passed = []
        for a in range(n):
            for k in range(1, 4):
                _, other = chip(k)
                slot = 2 * other + c
                copy(a, k, outs[a].at[slot], slot, sib).wait_recv()
                fw = copy(a, 3 + k, outs[a].at[slot], slot, sib)
                fw.start()
                passed.append(fw)
        for a in range(n):
            copy(a, 0, ins[a], 2 * mine + 1 - c, sib).wait_recv()
            for k in range(1, 4):
                _, other = chip(k)
                slot = 2 * other + 1 - c
                copy(a, 3 + k, outs[a].at[slot], slot, sib).wait_recv()
        for cp in sends + passed:
            cp.wait_send()
        for lc in locs:
            lc.wait()

    anyspec = pl.BlockSpec(memory_space=pl.ANY)
    out_shape = [jax.ShapeDtypeStruct((N_DEV,) + a.shape, a.dtype) for a in arrays]
    return pl.pallas_call(
        body, in_specs=[anyspec] * n, out_specs=[anyspec] * n, out_shape=out_shape,
        scratch_shapes=[pltpu.SemaphoreType.DMA((n, 7)), pltpu.SemaphoreType.DMA((n, 7)),
                        pltpu.SemaphoreType.DMA((n,))], name=name)(*arrays)


def send_to_sibling(name, arrays):
    n = len(arrays)

    def body(*refs):
        ins, outs = refs[:n], refs[n:2 * n]
        send_sems, recv_sems = refs[2 * n:]
        x, y, c = lax.axis_index("x"), lax.axis_index("y"), lax.axis_index("c")
        cps = [pltpu.make_async_remote_copy(src_ref=ins[a].at[1 - c], dst_ref=outs[a], send_sem=send_sems.at[a],
                                            recv_sem=recv_sems.at[a], device_id=(x, y, 1 - c), device_id_type=MESH)
               for a in range(n)]
        for cp in cps:
            cp.start()
        for cp in cps:
            cp.wait()

    anyspec = pl.BlockSpec(memory_space=pl.ANY)
    out_shape = [jax.ShapeDtypeStruct(a.shape[1:], a.dtype) for a in arrays]
    return pl.pallas_call(
        body, in_specs=[anyspec] * n, out_specs=[anyspec] * n, out_shape=out_shape,
        scratch_shapes=[pltpu.SemaphoreType.DMA((n,)), pltpu.SemaphoreType.DMA((n,))], name=name)(*arrays)


def chip_sum(name, both, other):
    _, nc, r, c = both.shape
    tr = min(ROW_TILE, r)

    def body(b_ref, o_ref, s_ref):
        core = lax.axis_index("c")
        own = jnp.where(core == 0, b_ref[0], b_ref[1]).astype(f32)
        s_ref[...] = (own + o_ref[...].astype(f32)).astype(s_ref.dtype)

    return pl.pallas_call(
        body, grid=(nc, r // tr),
        in_specs=[pl.BlockSpec((2, None, tr, c), lambda s, i: (0, s, i, 0)),
                  pl.BlockSpec((None, tr, c), lambda s, i: (s, i, 0))],
        out_specs=pl.BlockSpec((None, tr, c), lambda s, i: (s, i, 0)),
        out_shape=jax.ShapeDtypeStruct((nc, r, c), both.dtype), name=name)(both, other)


REPLICATED = ('lower_bounds', 'norm_mix_pre', 'norm_mix_post', 'norm_ff_pre', 'norm_ff_post', 'hgrn_norm_w',
              'gdn_a_log', 'gdn_dt_bias', 'gdn_norm_w', 'gmlp_ln_w', 'gmlp_ln_b', 'gmlp_w_s', 'gmlp_b_s',
              'conv_dw_b', 'conv_ln_w', 'conv_ln_b')
WEIGHTS = ('lower_bounds', 'norm_mix_pre', 'norm_mix_post', 'norm_ff_pre', 'norm_ff_post', 'w_in', 'w_out',
           'hgrn_norm_w', 'gdn_conv_w', 'gdn_a_log', 'gdn_dt_bias', 'gdn_norm_w', 'gmlp_ln_w', 'gmlp_ln_b',
           'gmlp_w_s', 'gmlp_b_s', 'conv_dw_w', 'conv_dw_b', 'conv_ln_w', 'conv_ln_b', 'w_ff1', 'w_ff2')


def _row(v):
    return v.reshape(1, -1)


def _pad_lanes(v, offset):
    return jnp.pad(v, (offset, 128 - offset - v.shape[0])).reshape(1, 128)


def _layer_fwd(l, x0, p, lb_all):
    t = x0.shape[0]
    sv = {'x0': x0}
    (h,) = rowwise(f"norm_mix_pre{l}", fn_norm, [(x0, D_MODEL, 0)], [p['g_mix_pre']], [(D_MODEL, bf16)], ROW_TILE)
    proj = matmul(f"proj{l}", h, p['w_in'], 'nn', f32, tn=896)
    sv.update(h=h, proj=proj)
    lb = lb_all[l:l + 1]
    o_a, st_a = scan_fwd(f"hgrn{l}", fn_hgrn, [(proj, 0), (proj, 4), (proj, 8), (proj, 12)], [lb], [p['hgrn_norm_w']], bf16, NH)
    q, k, v, beta, g = halo_fwd(
        f"gdn_pre{l}", fn_gdn_pre,
        [(proj, GW, 4, True), (proj, GW, 5, True), (proj, GW, 6, True), (proj, 128, 48, False)],
        [p['gdn_conv_w'], p['alog'], p['dtb']], [(GW, f32)] * 5, ROW_TILE, 8)
    o_b, st_b, tinv_b = scan_fwd(f"gdn{l}", fn_gdn, [(q, 0), (k, 0), (v, 0), (beta, 0), (g, 0), (proj, 28)], [],
                                 [p['gdn_norm_w']], bf16, NH, n_extra=1)
    (o_c,) = rowwise(f"gmlp{l}", fn_gmlp, [(proj, GW, 8), (proj, GW, 9)],
                     [p['gmlp_ln_w'], p['gmlp_ln_b'], p['gmlp_w_s'], p['gmlp_b_s']], [(GW, bf16)], MIX_CHUNK)
    (o_d,) = halo_fwd(f"conv{l}", fn_conv, [(proj, GW, 10, True), (proj, GW, 11, True)],
                      [p['conv_dw_w'], p['conv_dw_b'], p['conv_ln_w'], p['conv_ln_b']], [(GW, bf16)], ROW_TILE, 32)
    mix = jnp.concatenate([o_a, o_b, o_c, o_d], axis=1)
    y1 = matmul(f"out_proj{l}", mix, p['w_out'], 'nn', f32)
    (x1,) = rowwise(f"res_mix{l}", lambda x, y, gg: (x + _rms(y, gg),), [(x0, D_MODEL, 0), (y1, D_MODEL, 0)],
                    [p['g_mix_post']], [(D_MODEL, f32)], ROW_TILE)
    (h2,) = rowwise(f"norm_ff_pre{l}", fn_norm, [(x1, D_MODEL, 0)], [p['g_ff_pre']], [(D_MODEL, bf16)], ROW_TILE)
    u, a = matmul(f"ff1_{l}", h2, p['w_ff1'], 'nn', epilogue='relu2', b_gathered=('cols', l))
    y2 = matmul(f"ff2_{l}", a, p['w_ff2'], 'nn', f32, b_gathered=('rows', l))
    sv.update(st_a=st_a, q=q, k=k, v=v, beta=beta, g=g, st_b=st_b, tinv_b=tinv_b, mix=mix, y1=y1, x1=x1, h2=h2, u=u, a=a, y2=y2)
    return sv


def _layer_bwd(l, dx, sv, p, lb_all):
    gr = {}
    t = dx.shape[0]
    (dy2,), (gr['norm_ff_post'],) = rowwise_bwd(f"res_ff_bwd{l}", fn_norm, [(sv['y2'], D_MODEL, 0)], [p['g_ff_post']],
                                                [(dx, D_MODEL, 0)], [bf16], ROW_TILE)
    du = matmul(f"ff2_dx{l}", dy2, p['w_ff2'], 'nt', bf16, epilogue='relu2_bwd', extra=sv['u'], b_gathered=('rows', l))
    gr['w_ff2'] = matmul(f"ff2_dw{l}", sv['a'], dy2, 'tn', bf16, slots='rows')
    dh2 = matmul(f"ff1_dx{l}", du, p['w_ff1'], 'nt', f32, b_gathered=('cols', l))
    gr['w_ff1'] = matmul(f"ff1_dw{l}", sv['h2'], du, 'tn', bf16, slots='cols')
    (dx1,), (gr['norm_ff_pre'],) = rowwise_bwd(f"norm_ff_pre_bwd{l}", fn_norm, [(sv['x1'], D_MODEL, 0)], [p['g_ff_pre']],
                                               [(dh2, D_MODEL, 0)], [f32], ROW_TILE, addto=(dx, D_MODEL, 0))
    (dy1,), (gr['norm_mix_post'],) = rowwise_bwd(f"res_mix_bwd{l}", fn_norm, [(sv['y1'], D_MODEL, 0)], [p['g_mix_post']],
                                                 [(dx1, D_MODEL, 0)], [bf16], ROW_TILE)
    dmix = matmul(f"out_proj_dx{l}", dy1, p['w_out'], 'nt', f32)
    gr['w_out'] = matmul(f"out_proj_dw{l}", sv['mix'], dy1, 'tn', bf16, slots='rows')
    proj = sv['proj']
    lb = lb_all[l:l + 1]
    d_a, (dlb,), (gr['hgrn_norm_w'],) = scan_bwd(
        f"hgrn_bwd{l}", fn_hgrn, [(proj, 0), (proj, 4), (proj, 8), (proj, 12)], [lb], [p['hgrn_norm_w']],
        sv['st_a'], (dmix, 0), [bf16] * 4, NH)
    d_b, _, (gr['gdn_norm_w'],) = scan_bwd(
        f"gdn_bwd{l}", fn_gdn, [(sv['q'], 0), (sv['k'], 0), (sv['v'], 0), (sv['beta'], 0), (sv['g'], 0), (proj, 28)],
        [], [p['gdn_norm_w']], sv['st_b'], (dmix, 4), [f32] * 5 + [bf16], NH, extras=[sv['tinv_b']])
    d_bp, (gr['gdn_conv_w'], dalog, ddtb) = halo_bwd(
        f"gdn_pre_bwd{l}", fn_gdn_pre,
        [(proj, GW, 4, True), (proj, GW, 5, True), (proj, GW, 6, True), (proj, 128, 48, False)],
        [p['gdn_conv_w'], p['alog'], p['dtb']], [(d_b[j], GW, 0) for j in range(5)], bf16, ROW_TILE, 8)
    gr['gdn_a_log'] = dalog[0, NH:2 * NH]
    gr['gdn_dt_bias'] = ddtb[0, NH:2 * NH]
    d_c, (gr['gmlp_ln_w'], gr['gmlp_ln_b'], gr['gmlp_w_s'], gr['gmlp_b_s']) = rowwise_bwd(
        f"gmlp_bwd{l}", fn_gmlp, [(proj, GW, 8), (proj, GW, 9)],
        [p['gmlp_ln_w'], p['gmlp_ln_b'], p['gmlp_w_s'], p['gmlp_b_s']], [(dmix, GW, 2)], [bf16, bf16], MIX_CHUNK)
    d_d, (gr['conv_dw_w'], gr['conv_dw_b'], gr['conv_ln_w'], gr['conv_ln_b']) = halo_bwd(
        f"conv_bwd{l}", fn_conv, [(proj, GW, 10, True), (proj, GW, 11, True)],
        [p['conv_dw_w'], p['conv_dw_b'], p['conv_ln_w'], p['conv_ln_b']], [(dmix, GW, 3)], bf16, ROW_TILE, 32)
    dproj = jnp.concatenate(list(d_a) + [d_bp[0], d_bp[1], d_bp[2], d_b[5]] + list(d_c) + list(d_d) + [d_bp[3]], axis=1)
    dh = matmul(f"proj_dx{l}", dproj, p['w_in'], 'nt', f32, tk=896)
    gr['w_in'] = matmul(f"proj_dw{l}", sv['h'], dproj, 'tn', bf16, tn=896)
    (dx0,), (gr['norm_mix_pre'],) = rowwise_bwd(f"norm_mix_pre_bwd{l}", fn_norm, [(sv['x0'], D_MODEL, 0)], [p['g_mix_pre']],
                                                [(dh, D_MODEL, 0)], [f32], ROW_TILE, addto=(dx1, D_MODEL, 0))
    return dx0, gr, dlb


def kernel(x, lower_bounds, norm_mix_pre, norm_mix_post, norm_ff_pre, norm_ff_post, w_in, w_out, hgrn_norm_w, gdn_conv_w, gdn_a_log, gdn_dt_bias, gdn_norm_w, gmlp_ln_w, gmlp_ln_b, gmlp_w_s, gmlp_b_s, conv_dw_w, conv_dw_b, conv_ln_w, conv_ln_b, w_ff1, w_ff2, loss_target, m_lower_bounds, m_norm_mix_pre, m_norm_mix_post, m_norm_ff_pre, m_norm_ff_post, m_w_in, m_w_out, m_hgrn_norm_w, m_gdn_conv_w, m_gdn_a_log, m_gdn_dt_bias, m_gdn_norm_w, m_gmlp_ln_w, m_gmlp_ln_b, m_gmlp_w_s, m_gmlp_b_s, m_conv_dw_w, m_conv_dw_b, m_conv_ln_w, m_conv_ln_b, m_w_ff1, m_w_ff2, v_lower_bounds, v_norm_mix_pre, v_norm_mix_post, v_norm_ff_pre, v_norm_ff_post, v_w_in, v_w_out, v_hgrn_norm_w, v_gdn_conv_w, v_gdn_a_log, v_gdn_dt_bias, v_gdn_norm_w, v_gmlp_ln_w, v_gmlp_ln_b, v_gmlp_w_s, v_gmlp_b_s, v_conv_dw_w, v_conv_dw_b, v_conv_ln_w, v_conv_ln_b, v_w_ff1, v_w_ff2):
    loc = dict(locals())
    W = {n: loc[n] for n in WEIGHTS}
    M = {n: loc['m_' + n] for n in WEIGHTS}
    V = {n: loc['v_' + n] for n in WEIGHTS}
    t = x.shape[1]
    me = 4 * lax.axis_index("x") + 2 * lax.axis_index("y") + lax.axis_index("c")

    shards = [w_in.astype(bf16), w_out.astype(bf16), w_ff1.astype(bf16), w_ff2.astype(bf16), gdn_conv_w, conv_dw_w]
    g_in, g_out, g_ff1, g_ff2, g_gconv, g_dconv = gather_two_level("gather_weights", shards)
    w_in_full = jnp.moveaxis(g_in, 0, 2).reshape(DEPTH, D_MODEL, D_IN)
    w_in_pad = jnp.concatenate([w_in_full[:, :, :8 * GW], w_in_full[:, :, 8 * GW + 2 * NH:],
                                w_in_full[:, :, 8 * GW:8 * GW + 2 * NH],
                                jnp.zeros((DEPTH, D_MODEL, 128 - 2 * NH), bf16)], axis=2)
    w_out_full = jnp.moveaxis(g_out, 0, 1).reshape(DEPTH, D_MODEL, D_MODEL)
    gconv_full = jnp.moveaxis(g_gconv, 0, 2).reshape(DEPTH, SHORT_CONV, 3 * GW)
    dconv_full = jnp.moveaxis(g_dconv, 0, 2).reshape(DEPTH, CONV_WIDTH, GW)

    (lb_all,) = rowwise("lower_bounds", fn_lb, [(lower_bounds, GW, 0)], [], [(GW, f32)], DEPTH)

    P = []
    for l in range(DEPTH):
        P.append(dict(
            g_mix_pre=_row(norm_mix_pre[l]), g_mix_post=_row(norm_mix_post[l]), g_ff_pre=_row(norm_ff_pre[l]),
            g_ff_post=_row(norm_ff_post[l]), w_in=w_in_pad[l], w_out=w_out_full[l], w_ff1=g_ff1,
            w_ff2=g_ff2, hgrn_norm_w=_row(hgrn_norm_w[l]), gdn_conv_w=gconv_full[l],
            alog=_pad_lanes(gdn_a_log[l], NH), dtb=_pad_lanes(gdn_dt_bias[l], NH), gdn_norm_w=_row(gdn_norm_w[l]),
            gmlp_ln_w=_row(gmlp_ln_w[l]), gmlp_ln_b=_row(gmlp_ln_b[l]), gmlp_w_s=gmlp_w_s[l].reshape(NH * MIX_CHUNK, MIX_CHUNK),
            gmlp_b_s=gmlp_b_s[l], conv_dw_w=dconv_full[l], conv_dw_b=_row(conv_dw_b[l]), conv_ln_w=_row(conv_ln_w[l]),
            conv_ln_b=_row(conv_ln_b[l])))

    xs = x[0]
    saved = []
    for l in range(DEPTH):
        sv = _layer_fwd(l, xs, P[l], lb_all)
        saved.append(sv)
        if l < DEPTH - 1:
            (xs,) = rowwise(f"res_ff{l}", lambda a, y, gg: (a + _rms(y, gg),), [(sv['x1'], D_MODEL, 0), (sv['y2'], D_MODEL, 0)],
                            [P[l]['g_ff_post']], [(D_MODEL, f32)], ROW_TILE)
    sv = saved[-1]
    dx, loss_loc = final_loss("final_loss", sv['x1'], sv['y2'], P[-1]['g_ff_post'], loss_target[0])
    loss = lax.psum(loss_loc[0, 0], ("x", "y", "c"))

    G = {}
    dlb_rows = []
    for l in reversed(range(DEPTH)):
        dx, gr, dlb = _layer_bwd(l, dx, saved[l], P[l], lb_all)
        G[l] = gr
        dlb_rows.append(dlb)
    dlb_all = jnp.concatenate(dlb_rows[::-1], axis=0)
    (g_lower_bounds,), _ = rowwise_bwd("lower_bounds_bwd", fn_lb, [(lower_bounds, GW, 0)], [], [(dlb_all, GW, 0)],
                                       [f32], DEPTH)
    grad_x = dx[None]

    def stack(name, f=lambda a: a):
        return jnp.stack([f(G[l][name]) for l in range(DEPTH)], axis=0)

    full = {
        'lower_bounds': g_lower_bounds,
        'norm_mix_pre': stack('norm_mix_pre', lambda a: a[0]), 'norm_mix_post': stack('norm_mix_post', lambda a: a[0]),
        'norm_ff_pre': stack('norm_ff_pre', lambda a: a[0]), 'norm_ff_post': stack('norm_ff_post', lambda a: a[0]),
        'hgrn_norm_w': stack('hgrn_norm_w', lambda a: a[0]), 'gdn_a_log': stack('gdn_a_log'), 'gdn_dt_bias': stack('gdn_dt_bias'),
        'gdn_norm_w': stack('gdn_norm_w', lambda a: a[0]), 'gmlp_ln_w': stack('gmlp_ln_w', lambda a: a[0]),
        'gmlp_ln_b': stack('gmlp_ln_b', lambda a: a[0]),
        'gmlp_w_s': stack('gmlp_w_s', lambda a: a.reshape(NH, MIX_CHUNK, MIX_CHUNK)), 'gmlp_b_s': stack('gmlp_b_s'),
        'conv_dw_b': stack('conv_dw_b', lambda a: a[0]), 'conv_ln_w': stack('conv_ln_w', lambda a: a[0]),
        'conv_ln_b': stack('conv_ln_b', lambda a: a[0]),
        'gdn_conv_w': stack('gdn_conv_w'), 'conv_dw_w': stack('conv_dw_w'),
    }

    small_names = list(REPLICATED) + ['gdn_conv_w', 'conv_dw_w']
    flat = jnp.concatenate([full[n].reshape(-1) for n in small_names])
    n_small = flat.shape[0]
    n_pad = -(-n_small // 1024) * 1024
    packed = jnp.pad(flat, (0, n_pad - n_small)).reshape(n_pad // 128, 128)

    def slots_in(l):
        gp = G[l]['w_in']
        gl = jnp.concatenate([gp[:, :8 * GW], gp[:, 12 * GW:12 * GW + 2 * NH], gp[:, 8 * GW:12 * GW]], axis=1)
        return jnp.transpose(gl.reshape(D_MODEL, N_DEV // 2, 2, D_IN // N_DEV), (2, 1, 0, 3))

    send = []
    for l in range(DEPTH):
        send += [slots_in(l), G[l]['w_out'], G[l]['w_ff1'], G[l]['w_ff2']]
    (small_slots,) = exchange("gather_small_grads", [packed], True, 'all')
    from_sibling = send_to_sibling("scatter_grads_d2d", send)
    partial = [chip_sum(f"chip_sum{i}", b, o) for i, (b, o) in enumerate(zip(send, from_sibling))]
    recv = exchange("scatter_grads_ici", partial, False, 'chips')

    out = {}
    for j, name in enumerate(('w_in', 'w_out', 'w_ff1', 'w_ff2')):
        out[name] = adamw(f"adamw_{name}", W[name], M[name], V[name], [recv[4 * l + j] for l in range(DEPTH)])

    def pack(d, fill):
        parts = [d[n].reshape(-1) for n in REPLICATED]
        parts.append(jnp.full((n_pad - sum(a.shape[0] for a in parts),), fill, f32))
        return jnp.concatenate(parts).reshape(1, n_pad // 128, 128)

    sm = adamw("adamw_small", pack(W, 0.0), pack(M, 0.0), pack(V, 1.0), [small_slots], tr=n_pad // 128)
    off = 0
    for n in REPLICATED:
        sz = W[n].size
        out[n] = tuple(a.reshape(-1)[off:off + sz].reshape(W[n].shape) for a in sm)
        off += sz
    gsum = sm[0].reshape(-1)
    for n, full_shape in (('gdn_conv_w', (DEPTH, SHORT_CONV, 3 * GW)), ('conv_dw_w', (DEPTH, CONV_WIDTH, GW))):
        sz = math.prod(full_shape)
        gfull = gsum[off:off + sz].reshape(full_shape)
        off += sz
        sh = W[n].shape
        gmine = lax.dynamic_slice_in_dim(gfull, me * sh[2], sh[2], axis=2)
        r = adamw(f"adamw_{n}", W[n].reshape(1, sh[0] * sh[1], sh[2]), M[n].reshape(1, sh[0] * sh[1], sh[2]),
                  V[n].reshape(1, sh[0] * sh[1], sh[2]), [gmine.reshape(1, sh[0] * sh[1], sh[2])], tr=sh[0] * sh[1])
        out[n] = tuple(a.reshape(sh) for a in r)

    return (loss, grad_x, *[out[n][0] for n in WEIGHTS], *[out[n][1] for n in WEIGHTS],
            *[out[n][2] for n in WEIGHTS], *[out[n][3] for n in WEIGHTS])
```

```python
import functools
import math

import jax
import jax.numpy as jnp
from jax import lax
from jax.experimental import pallas as pl
from jax.experimental.pallas import tpu as pltpu

f32 = jnp.float32
bf16 = jnp.bfloat16
HI = lax.Precision.HIGHEST

N_DEV = 8
DEPTH = 2
D_MODEL = 2048
GW = 512
HD = 128
NH = 4
CHUNK = 64
MIX_CHUNK = 128
CONV_WIDTH = 31
SHORT_CONV = 4
D_FF = 4 * D_MODEL
D_IN = 12 * GW + 2 * NH
D_IN_PAD = 12 * GW + 128
ROW_TILE = 256
HGRN_SUB = 16
EPS = 1e-6
TINY = 1e-30
ADAM_LR, ADAM_B1, ADAM_B2, ADAM_EPS, ADAM_WD, ADAM_STEP = 0.001, 0.9, 0.999, 1e-08, 0.01, 10
MESH = pl.DeviceIdType.MESH


def _dotb(a, b, ca, cb):
    return lax.dot_general(a.astype(bf16), b.astype(bf16), (((ca,), (cb,)), ((), ())),
                           preferred_element_type=f32)


@jax.custom_vjp
def mm(a, b):
    return _dotb(a, b, 1, 0)


def _mm_f(a, b):
    return mm(a, b), (a, b)


def _mm_b(res, ct):
    a, b = res
    return _dotb(ct, b, 1, 1), _dotb(a, ct, 0, 0)


mm.defvjp(_mm_f, _mm_b)


@jax.custom_vjp
def mm_nt(a, b):
    return _dotb(a, b, 1, 1)


def _mmnt_f(a, b):
    return mm_nt(a, b), (a, b)


def _mmnt_b(res, ct):
    a, b = res
    return _dotb(ct, b, 1, 0), _dotb(ct, a, 0, 0)


mm_nt.defvjp(_mmnt_f, _mmnt_b)


@jax.custom_vjp
def mm_tn(a, b):
    return _dotb(a, b, 0, 0)


def _mmtn_f(a, b):
    return mm_tn(a, b), (a, b)


def _mmtn_b(res, ct):
    a, b = res
    return _dotb(b, ct, 1, 1), _dotb(a, ct, 1, 0)


mm_tn.defvjp(_mmtn_f, _mmtn_b)


def mmh(a, b):
    return jnp.dot(a, b, precision=HI, preferred_element_type=f32)


def mm3(a, b):
    return jnp.dot(a, b, precision=lax.Precision.HIGH, preferred_element_type=f32)


def _rms(x, w):
    return x * lax.rsqrt(jnp.mean(x * x, axis=-1, keepdims=True) + EPS) * w


def _ln(x, w, b):
    mu = jnp.mean(x, axis=-1, keepdims=True)
    xc = x - mu
    var = jnp.mean(xc * xc, axis=-1, keepdims=True)
    return xc * lax.rsqrt(var + EPS) * w + b


def _gelu(x):
    return 0.5 * x * (1.0 + lax.erf(x * (2.0 ** -0.5)))


def _iota2(n, m, axis):
    return lax.broadcasted_iota(jnp.int32, (n, m), axis)


def _tri(n, strict=False):
    r, c = _iota2(n, n, 0), _iota2(n, n, 1)
    return (r > c) if strict else (r >= c)


def _eye(n):
    return (_iota2(n, n, 0) == _iota2(n, n, 1)).astype(f32)


def fn_norm(x, g):
    return (_rms(x, g),)


def fn_lb(lower_bounds):
    s = jax.nn.softmax(lower_bounds, axis=0)
    rows, cum = [], None
    for i in range(DEPTH):
        cum = s[i:i + 1] if cum is None else cum + s[i:i + 1]
        rows.append(cum - s[0:1])
    return (jnp.concatenate(rows, axis=0),)


def fn_hgrn(aq, af, ai, ag, lb, nw, st):
    c = aq.shape[0]
    sig = jax.nn.sigmoid(af)
    f = lb + (1.0 - lb) * sig
    logf = jnp.log(jnp.maximum(f, TINY))
    k = (1.0 - lb) * jax.nn.sigmoid(-af)
    q = jax.nn.silu(aq)
    v = ai
    b = mmh(_tri(c).astype(f32), logf)
    outs = []
    for lo in range(0, c, HGRN_SUB):
        qi, ki, vi, bi = (a[lo:lo + HGRN_SUB] for a in (q, k, v, b))
        rel = bi[:, None, :] - bi[None, :, :]
        dec = jnp.exp(jnp.minimum(rel, 0.0)) * qi[:, None, :] * ki[None, :, :]
        o_blk = mm(jnp.where(_tri(HGRN_SUB), jnp.sum(dec, axis=-1), 0.0), vi)
        if lo > 0:
            r = b[lo - 1:lo, :]
            o_blk = o_blk + mm(mm_nt(qi * jnp.exp(bi - r), k[:lo] * jnp.exp(r - b[:lo])), v[:lo])
        outs.append(o_blk)
    b_end = b[c - 1:c, :]
    out = jnp.concatenate(outs, axis=0) + mm_nt(q * jnp.exp(b), st)
    st_new = st * jnp.exp(b_end) + mm_tn(v, k * jnp.exp(b_end - b))
    o = _rms(out, nw) * jax.nn.silu(ag)
    return o, st_new


def fn_gdn_pre(tq, tk, tv, bq, bk, bv, p8, conv_w, alog, dtb):
    tile = bq.shape[0]
    h = tq.shape[0]
    outs = []
    for seg, (tl, cur) in enumerate(((tq, bq), (tk, bk), (tv, bv))):
        xe = jnp.concatenate([tl, cur], axis=0)
        acc = None
        for kk in range(SHORT_CONV):
            off = h - (SHORT_CONV - 1) + kk
            term = conv_w[kk:kk + 1, seg * GW:(seg + 1) * GW] * xe[off:off + tile, :]
            acc = term if acc is None else acc + term
        outs.append(jax.nn.silu(acc))
    sq, sk, sv = outs
    qh, kh = [], []
    for hh in range(NH):
        a = sq[:, hh * HD:(hh + 1) * HD]
        qh.append(a * lax.rsqrt(jnp.sum(a * a, axis=-1, keepdims=True) + EPS) * (HD ** -0.5))
        a = sk[:, hh * HD:(hh + 1) * HD]
        kh.append(a * lax.rsqrt(jnp.sum(a * a, axis=-1, keepdims=True) + EPS))
    q = jnp.concatenate(qh, axis=1)
    k = jnp.concatenate(kh, axis=1)
    beta = jax.nn.sigmoid(p8)
    g = -jnp.exp(alog) * jax.nn.softplus(p8 + dtb)
    r, cc = _iota2(128, GW, 0), _iota2(128, GW, 1) // HD
    e_beta = (r == cc).astype(f32)
    e_g = (r == cc + NH).astype(f32)
    return q, k, sv, mmh(beta, e_beta), mmh(g, e_g)


@jax.custom_vjp
def _inverse_given(m, tinv):
    return tinv


def _inverse_given_f(m, tinv):
    return tinv, tinv


def _inverse_given_b(tinv, ct):
    x = lax.dot_general(ct, tinv, (((1,), (1,)), ((), ())), precision=HI, preferred_element_type=f32)
    dm = -lax.dot_general(tinv, x, (((0,), (0,)), ((), ())), precision=HI, preferred_element_type=f32)
    return dm, jnp.zeros_like(tinv)


_inverse_given.defvjp(_inverse_given_f, _inverse_given_b)


def fn_gdn(q, k, v, beta, g, z, nw, s, tinv_saved=None):
    c = q.shape[0]
    gc = mmh(_tri(c).astype(f32), g)
    gcol = gc[:, 0:1]
    grow = jnp.sum(gcol * _eye(c), axis=0, keepdims=True)
    gamma = jnp.where(_tri(c), jnp.exp(jnp.minimum(gcol - grow, 0.0)), 0.0)
    kb = k * beta
    m = jnp.where(_tri(c, strict=True), mm_nt(kb, k) * gamma, 0.0)
    if tinv_saved is None:
        eye = _eye(c)
        tinv = eye - m
        p = m
        for _ in range(int(math.log2(c)) - 1):
            p = mm3(p, p)
            tinv = mm3(tinv, eye + p)
    else:
        tinv = _inverse_given(m, tinv_saved)
    egc = jnp.exp(gc)
    u = mmh(tinv, v * beta)
    w = mmh(tinv, kb * egc)
    qk = mm_nt(q, k) * gamma
    gc_end = gc[c - 1:c, :]
    q_dec = q * egc
    k_dec = k * jnp.exp(gc_end - gc)
    v_new = u - mm(w, s)
    out = mm(q_dec, s) + mm(qk, v_new)
    s_new = s * jnp.exp(gc_end) + mm_tn(k_dec, v_new)
    o = _rms(out, nw) * jax.nn.silu(z)
    return (o, s_new, tinv) if tinv_saved is None else (o, s_new)


def fn_gmlp(cu, cv, ln_w, ln_b, w_s, b_s):
    n = cu.shape[0]
    ug = _gelu(cu)
    vn = _ln(_gelu(cv), ln_w, ln_b)
    eye = _eye(n)
    cols = []
    for hh in range(NH):
        wc = jnp.where(_tri(n), w_s[hh * n:(hh + 1) * n, :], 0.0)
        bcol = jnp.sum(b_s[hh:hh + 1, :] * eye, axis=1, keepdims=True)
        cols.append(mm(wc, vn[:, hh * HD:(hh + 1) * HD]) + bcol)
    return (ug * jnp.concatenate(cols, axis=1),)


def fn_conv(ta, tg, a, gate, dw_w, dw_b, ln_w, ln_b):
    tile = a.shape[0]
    h = ta.shape[0]
    ya = jnp.concatenate([ta, a], axis=0)
    yg = jnp.concatenate([tg, gate], axis=0)
    y = ya * jax.nn.sigmoid(yg)
    acc = None
    for kk in range(CONV_WIDTH):
        off = h - (CONV_WIDTH - 1) + kk
        term = dw_w[kk:kk + 1, :] * y[off:off + tile, :]
        acc = term if acc is None else acc + term
    return (jax.nn.silu(_ln(acc + dw_b, ln_w, ln_b)),)


def _full_spec(arr):
    nd = arr.ndim
    return pl.BlockSpec(arr.shape, lambda *_: (0,) * nd)


def rowwise(name, fn, tiled, params, outs, tile):
    t = tiled[0][0].shape[0]
    tile = min(tile, t)
    nt, np_ = len(tiled), len(params)

    def body(*refs):
        vals = [r[...].astype(f32) for r in refs[:nt + np_]]
        res = fn(*vals)
        for o_ref, r in zip(refs[nt + np_:], res):
            o_ref[...] = r.astype(o_ref.dtype)

    in_specs = [pl.BlockSpec((tile, w), lambda i, c=c: (i, c)) for _, w, c in tiled]
    in_specs += [_full_spec(p) for p in params]
    out_specs = [pl.BlockSpec((tile, w), lambda i: (i, 0)) for w, _ in outs]
    out_shape = [jax.ShapeDtypeStruct((t, w), dt) for w, dt in outs]
    return pl.pallas_call(body, grid=(t // tile,), in_specs=in_specs, out_specs=out_specs,
                          out_shape=out_shape, name=name)(*[a for a, _, _ in tiled], *params)


def rowwise_bwd(name, fn, tiled, params, cots, gouts, tile, addto=None):
    t = tiled[0][0].shape[0]
    tile = min(tile, t)
    nt, np_, nc = len(tiled), len(params), len(cots)
    na = 0 if addto is None else 1
    gidx = [i for i, g in enumerate(gouts) if g is not None]

    def body(*refs):
        i = pl.program_id(0)
        vals = [r[...].astype(f32) for r in refs[:nt + np_]]
        cvals = tuple(r[...].astype(f32) for r in refs[nt + np_:nt + np_ + nc])
        _, vjp = jax.vjp(fn, *vals)
        grads = vjp(cvals)
        orefs = refs[nt + np_ + nc + na:]
        for n, j in enumerate(gidx):
            g = grads[j]
            if na and n == 0:
                g = g + refs[nt + np_ + nc][...].astype(f32)
            orefs[n][...] = g.astype(orefs[n].dtype)
        prefs = orefs[len(gidx):]

        @pl.when(i == 0)
        def _():
            for r in prefs:
                r[...] = jnp.zeros_like(r)

        for r, g in zip(prefs, grads[nt:]):
            r[...] += g

    in_specs = [pl.BlockSpec((tile, w), lambda i, c=c: (i, c)) for _, w, c in tiled]
    in_specs += [_full_spec(p) for p in params]
    in_specs += [pl.BlockSpec((tile, w), lambda i, c=c: (i, c)) for _, w, c in cots]
    args = [a for a, _, _ in tiled] + list(params) + [a for a, _, _ in cots]
    if na:
        in_specs.append(pl.BlockSpec((tile, addto[1]), lambda i, c=addto[2]: (i, c)))
        args.append(addto[0])
    out_specs = [pl.BlockSpec((tile, tiled[j][1]), lambda i: (i, 0)) for j in gidx]
    out_shape = [jax.ShapeDtypeStruct((t, tiled[j][1]), gouts[j]) for j in gidx]
    out_specs += [_full_spec(p) for p in params]
    out_shape += [jax.ShapeDtypeStruct(p.shape, f32) for p in params]
    res = pl.pallas_call(body, grid=(t // tile,), in_specs=in_specs, out_specs=out_specs,
                         out_shape=out_shape, name=name)(*args)
    return res[:len(gidx)], res[len(gidx):]


def halo_fwd(name, fn, tiled, params, outs, tile, halo):
    t = tiled[0][0].shape[0]
    tile = min(tile, t)
    hal = [j for j, x in enumerate(tiled) if x[3]]
    nt, nh, np_ = len(tiled), len(hal), len(params)
    per = tile // halo

    def body(*refs):
        i = pl.program_id(0)
        first = (i > 0).astype(f32)
        tails = [r[...].astype(f32) * first for r in refs[:nh]]
        vals = [r[...].astype(f32) for r in refs[nh:nh + nt + np_]]
        res = fn(*tails, *vals)
        for o_ref, r in zip(refs[nh + nt + np_:], res):
            o_ref[...] = r.astype(o_ref.dtype)

    in_specs = [pl.BlockSpec((halo, tiled[j][1]), lambda i, c=tiled[j][2]: (jnp.maximum(i * per - 1, 0), c))
                for j in hal]
    in_specs += [pl.BlockSpec((tile, w), lambda i, c=c: (i, c)) for _, w, c, _ in tiled]
    in_specs += [_full_spec(p) for p in params]
    out_specs = [pl.BlockSpec((tile, w), lambda i: (i, 0)) for w, _ in outs]
    out_shape = [jax.ShapeDtypeStruct((t, w), dt) for w, dt in outs]
    args = [tiled[j][0] for j in hal] + [x[0] for x in tiled] + list(params)
    return pl.pallas_call(body, grid=(t // tile,), in_specs=in_specs, out_specs=out_specs,
                          out_shape=out_shape, name=name)(*args)


def halo_bwd(name, fn, tiled, params, cots, gdtype, tile, halo):
    t = tiled[0][0].shape[0]
    tile = min(tile, t)
    hal = [j for j, x in enumerate(tiled) if x[3]]
    nt, nh, np_, nc = len(tiled), len(hal), len(params), len(cots)
    per = tile // halo
    n_tiles = t // tile

    def body(*refs):
        s = pl.program_id(0)
        i = n_tiles - 1 - s
        first = (i > 0).astype(f32)
        tails = [r[...].astype(f32) * first for r in refs[:nh]]
        vals = [r[...].astype(f32) for r in refs[nh:nh + nt + np_]]
        cvals = tuple(r[...].astype(f32) for r in refs[nh + nt + np_:nh + nt + np_ + nc])
        n_in = nh + nt + np_ + nc
        orefs = refs[n_in:n_in + nt]
        prefs = refs[n_in + nt:n_in + nt + np_]
        carries = refs[n_in + nt + np_:]

        @pl.when(s == 0)
        def _():
            for r in prefs:
                r[...] = jnp.zeros_like(r)
            for r in carries:
                r[...] = jnp.zeros_like(r)

        _, vjp = jax.vjp(fn, *tails, *vals)
        grads = vjp(cvals)
        for j in range(nt):
            g = grads[nh + j]
            if j in hal:
                cr = carries[hal.index(j)]
                g = jnp.concatenate([g[:tile - halo], g[tile - halo:] + cr[...]], axis=0)
            orefs[j][...] = g.astype(orefs[j].dtype)
        for n in range(nh):
            carries[n][...] = grads[n] * first
        for r, g in zip(prefs, grads[nh + nt:]):
            r[...] += g

    rev = lambda s: n_tiles - 1 - s
    in_specs = [pl.BlockSpec((halo, tiled[j][1]),
                             lambda s, c=tiled[j][2]: (jnp.maximum(rev(s) * per - 1, 0), c)) for j in hal]
    in_specs += [pl.BlockSpec((tile, w), lambda s, c=c: (rev(s), c)) for _, w, c, _ in tiled]
    in_specs += [_full_spec(p) for p in params]
    in_specs += [pl.BlockSpec((tile, w), lambda s, c=c: (rev(s), c)) for _, w, c in cots]
    out_specs = [pl.BlockSpec((tile, w), lambda s: (rev(s), 0)) for _, w, _, _ in tiled]
    out_shape = [jax.ShapeDtypeStruct((t, w), gdtype) for _, w, _, _ in tiled]
    out_specs += [_full_spec(p) for p in params]
    out_shape += [jax.ShapeDtypeStruct(p.shape, f32) for p in params]
    scratch = [pltpu.VMEM((halo, tiled[j][1]), f32) for j in hal]
    args = [tiled[j][0] for j in hal] + [x[0] for x in tiled] + list(params) + [a for a, _, _ in cots]
    res = pl.pallas_call(body, grid=(n_tiles,), in_specs=in_specs, out_specs=out_specs,
                         out_shape=out_shape, scratch_shapes=scratch, name=name)(*args)
    return res[:nt], res[nt:]


def _call_with_carry(name, body, grid, in_specs, out_specs, out_shape, scratch, args, carry):
    if carry is None:
        res = pl.pallas_call(body, grid=grid, in_specs=in_specs, out_specs=out_specs, out_shape=out_shape,
                             scratch_shapes=scratch, name=name)(*args)
        return list(res), []
    kind, arrays = carry
    nc, n_in, n_out, n_scr = len(arrays), len(in_specs), len(out_shape), len(scratch)

    def carried(*refs):
        cut = [n_in, nc, n_out, nc, n_scr]
        parts, pos = [], 0
        for k in cut:
            parts.append(refs[pos:pos + k])
            pos += k
        ins, cins, outs, couts, scr = parts
        start, finish = _carry_parts(kind, cins, couts, refs[pos:])
        ids = [pl.program_id(d) for d in range(len(grid))]
        first, last = ids[0] == 0, ids[0] == grid[0] - 1
        for d in range(1, len(grid)):
            first, last = first & (ids[d] == 0), last & (ids[d] == grid[d] - 1)
        pl.when(first)(start)
        body(*ins, *outs, *scr)
        pl.when(last)(finish)

    anyspec = pl.BlockSpec(memory_space=pl.ANY)
    res = pl.pallas_call(
        carried, grid=grid, in_specs=list(in_specs) + [anyspec] * nc, out_specs=list(out_specs) + [anyspec] * nc,
        out_shape=list(out_shape) + _carry_out_shape(kind, arrays),
        scratch_shapes=list(scratch) + _carry_sems(kind, nc), name=name)(*args, *arrays)
    return list(res[:n_out]), list(res[n_out:])


def scan_fwd(name, fn, tiled, pparams, sparams, out_dtype, hb, n_extra=0, carry=None):
    t = tiled[0][0].shape[0]
    n = t // CHUNK
    nt, npp, nsp = len(tiled), len(pparams), len(sparams)
    w = HD * hb

    def body(*refs):
        c = pl.program_id(1)
        n_in = nt + npp + nsp
        o_ref, sv_ref = refs[n_in], refs[n_in + 1]
        ex_refs, st = refs[n_in + 2:n_in + 2 + n_extra], refs[n_in + 2 + n_extra]

        @pl.when(c == 0)
        def _():
            st[...] = jnp.zeros_like(st)

        vals = [r[...].astype(f32) for r in refs[:n_in]]
        outs = []
        for hh in range(hb):
            hv = [v[:, hh * HD:(hh + 1) * HD] for v in vals[:nt + npp]] + vals[nt + npp:]
            s_in = st[hh]
            sv_ref[hh] = s_in
            res = fn(*hv, s_in)
            st[hh] = res[1]
            outs.append(res[0])
            for e_ref, e in zip(ex_refs, res[2:]):
                e_ref[hh] = e
        o_ref[...] = (outs[0] if hb == 1 else jnp.concatenate(outs, axis=1)).astype(o_ref.dtype)

    in_specs = [pl.BlockSpec((CHUNK, w), lambda g, c, b=b: (c, b // hb + g)) for _, b in tiled]
    in_specs += [pl.BlockSpec((1, w), lambda g, c: (0, g)) for _ in pparams]
    in_specs += [_full_spec(p) for p in sparams]
    out_specs = [pl.BlockSpec((CHUNK, w), lambda g, c: (c, g)),
                 pl.BlockSpec((hb, None, HD, HD), lambda g, c: (g, c, 0, 0))]
    out_shape = [jax.ShapeDtypeStruct((t, GW), out_dtype), jax.ShapeDtypeStruct((NH, n, HD, HD), f32)]
    out_specs += [pl.BlockSpec((hb, None, CHUNK, CHUNK), lambda g, c: (g, c, 0, 0))] * n_extra
    out_shape += [jax.ShapeDtypeStruct((NH, n, CHUNK, CHUNK), f32)] * n_extra
    res, carried = _call_with_carry(name, body, (NH // hb, n), in_specs, out_specs, out_shape,
                                    [pltpu.VMEM((hb, HD, HD), f32)], [a for a, _ in tiled] + list(pparams) + list(sparams),
                                    carry)
    return res + [carried]


def scan_bwd(name, fn, tiled, pparams, sparams, states, cot, gdtypes, hb, extras=(), carry=None):
    t = tiled[0][0].shape[0]
    n = t // CHUNK
    nt, npp, nsp, nex = len(tiled), len(pparams), len(sparams), len(extras)
    w = HD * hb

    def body(*refs):
        g, s = pl.program_id(0), pl.program_id(1)
        n_in = nt + npp + nsp
        vals = [r[...].astype(f32) for r in refs[:n_in]]
        st_ref = refs[n_in]
        do = refs[n_in + 1][...].astype(f32)
        ex_refs = refs[n_in + 2:n_in + 2 + nex]
        n_op = n_in + 2 + nex
        orefs = refs[n_op:n_op + nt]
        pprefs = refs[n_op + nt:n_op + nt + npp]
        sprefs = refs[n_op + nt + npp:n_op + nt + npp + nsp]
        ds = refs[n_op + nt + npp + nsp]

        @pl.when(s == 0)
        def _():
            ds[...] = jnp.zeros_like(ds)
            for r in pprefs:
                r[...] = jnp.zeros_like(r)

        @pl.when((s == 0) & (g == 0))
        def _():
            for r in sprefs:
                r[...] = jnp.zeros_like(r)

        per_head = []
        for hh in range(hb):
            sl = slice(hh * HD, (hh + 1) * HD)
            hv = [v[:, sl] for v in vals[:nt + npp]] + vals[nt + npp:]
            ex = [r[hh] for r in ex_refs]
            _, vjp = jax.vjp(lambda *a: fn(*a, *ex), *hv, st_ref[hh])
            grads = vjp((do[:, sl], ds[hh]))
            ds[hh] = grads[n_in]
            per_head.append(grads)
        cat = lambda j: per_head[0][j] if hb == 1 else jnp.concatenate([gr[j] for gr in per_head], axis=1)
        for j in range(nt):
            orefs[j][...] = cat(j).astype(orefs[j].dtype)
        for j, r in enumerate(pprefs):
            r[...] += cat(nt + j)
        for j, r in enumerate(sprefs):
            tot = per_head[0][nt + npp + j]
            for gr in per_head[1:]:
                tot = tot + gr[nt + npp + j]
            r[...] += tot

    rev = lambda s: n - 1 - s
    in_specs = [pl.BlockSpec((CHUNK, w), lambda g, s, b=b: (rev(s), b // hb + g)) for _, b in tiled]
    in_specs += [pl.BlockSpec((1, w), lambda g, s: (0, g)) for _ in pparams]
    in_specs += [_full_spec(p) for p in sparams]
    in_specs += [pl.BlockSpec((hb, None, HD, HD), lambda g, s: (g, rev(s), 0, 0)),
                 pl.BlockSpec((CHUNK, w), lambda g, s, b=cot[1]: (rev(s), b // hb + g))]
    in_specs += [pl.BlockSpec((hb, None, CHUNK, CHUNK), lambda g, s: (g, rev(s), 0, 0)) for _ in extras]
    out_specs = [pl.BlockSpec((CHUNK, w), lambda g, s: (rev(s), g)) for _ in tiled]
    out_shape = [jax.ShapeDtypeStruct((t, GW), dt) for dt in gdtypes]
    out_specs += [pl.BlockSpec((1, w), lambda g, s: (0, g)) for _ in pparams]
    out_shape += [jax.ShapeDtypeStruct(p.shape, f32) for p in pparams]
    out_specs += [_full_spec(p) for p in sparams]
    out_shape += [jax.ShapeDtypeStruct(p.shape, f32) for p in sparams]
    res, carried = _call_with_carry(
        name, body, (NH // hb, n), in_specs, out_specs, out_shape, [pltpu.VMEM((hb, HD, HD), f32)],
        [a for a, _ in tiled] + list(pparams) + list(sparams) + [states, cot[0]] + list(extras), carry)
    return res[:nt], res[nt:nt + npp], res[nt + npp:], carried


def matmul(name, a, b, mode, out_dtype=f32, tm=1024, tn=1024, tk=2048, epilogue=None, extra=None, slots=None,
           b_gathered=None):
    if b_gathered is not None:
        cut, layer = b_gathered
        _, _, sr, sc = b.shape
        b_rows, b_cols = (N_DEV * sr, sc) if cut == 'rows' else (sr, N_DEV * sc)
    else:
        b_rows, b_cols = b.shape
    if mode == 'nn':
        (m, k), n = a.shape, b_cols
    elif mode == 'nt':
        (m, k), n = a.shape, b_rows
    else:
        (k, m), n = a.shape, b_cols
    tm, tn, tk = min(tm, m), min(tn, n), min(tk, k)
    if b_gathered is not None:
        if (mode == 'nn') == (cut == 'cols'):
            tn = min(tn, sc if cut == 'cols' else sr)
        else:
            tk = min(tk, sr if cut == 'rows' else sc)
    if slots == 'rows':
        tm = min(tm, m // N_DEV)
    if slots == 'cols':
        tn = min(tn, n // N_DEV)
    nk = k // tk
    ca, cb = {'nn': (1, 0), 'nt': (1, 1), 'tn': (0, 0)}[mode]

    def finish(refs, r):
        if epilogue == 'relu2':
            refs[2][...] = r
            refs[3][...] = jnp.square(jnp.maximum(r, 0.0)).astype(bf16)
        elif epilogue == 'relu2_bwd':
            refs[3][...] = (r * 2.0 * jnp.maximum(refs[2][...], 0.0)).astype(refs[3].dtype)
        else:
            refs[2][...] = r.astype(refs[2].dtype)

    def body(*refs):
        part = _dotb(refs[0][...], refs[1][...], ca, cb)
        if nk == 1:
            finish(refs, part)
            return
        acc = refs[-1]
        kk = pl.program_id(2)

        @pl.when(kk == 0)
        def _():
            acc[...] = part

        @pl.when(kk > 0)
        def _():
            acc[...] += part

        @pl.when(kk == nk - 1)
        def _():
            finish(refs, acc[...])

    if mode == 'nn':
        a_spec = pl.BlockSpec((tm, tk), lambda i, j, kk: (i, kk))
        b_spec = pl.BlockSpec((tk, tn), lambda i, j, kk: (kk, j))
    elif mode == 'nt':
        a_spec = pl.BlockSpec((tm, tk), lambda i, j, kk: (i, kk))
        b_spec = pl.BlockSpec((tn, tk), lambda i, j, kk: (j, kk))
    else:
        a_spec = pl.BlockSpec((tk, tm), lambda i, j, kk: (kk, i))
        b_spec = pl.BlockSpec((tk, tn), lambda i, j, kk: (kk, j))
    if b_gathered is not None:
        bshape = (None, None, tk, tn) if mode == 'nn' else (None, None, tn, tk)
        if mode == 'nn' and cut == 'cols':
            per = sc // tn
            b_spec = pl.BlockSpec(bshape, lambda i, j, kk: (j // per, layer, kk, j % per))
        elif mode == 'nn':
            per = sr // tk
            b_spec = pl.BlockSpec(bshape, lambda i, j, kk: (kk // per, layer, kk % per, j))
        elif cut == 'cols':
            per = sc // tk
            b_spec = pl.BlockSpec(bshape, lambda i, j, kk: (kk // per, layer, j, kk % per))
        else:
            per = sr // tn
            b_spec = pl.BlockSpec(bshape, lambda i, j, kk: (j // per, layer, j % per, kk))
    o_spec = pl.BlockSpec((tm, tn), lambda i, j, kk: (i, j))
    in_specs, args = [a_spec, b_spec], [a, b]
    if epilogue == 'relu2':
        out_specs = [o_spec, o_spec]
        out_shape = [jax.ShapeDtypeStruct((m, n), f32), jax.ShapeDtypeStruct((m, n), bf16)]
    elif slots == 'rows':
        per = (m // N_DEV) // tm
        out_specs = pl.BlockSpec((None, None, tm, tn), lambda i, j, kk: ((i // per) % 2, (i // per) // 2, i % per, j))
        out_shape = jax.ShapeDtypeStruct((2, N_DEV // 2, m // N_DEV, n), out_dtype)
    elif slots == 'cols':
        per = (n // N_DEV) // tn
        out_specs = pl.BlockSpec((None, None, tm, tn), lambda i, j, kk: ((j // per) % 2, (j // per) // 2, i, j % per))
        out_shape = jax.ShapeDtypeStruct((2, N_DEV // 2, m, n // N_DEV), out_dtype)
    else:
        out_specs, out_shape = o_spec, jax.ShapeDtypeStruct((m, n), out_dtype)
        if epilogue == 'relu2_bwd':
            in_specs.append(o_spec)
            args.append(extra)
    scratch = [pltpu.VMEM((tm, tn), f32)] if nk > 1 else []
    return pl.pallas_call(body, grid=(m // tm, n // tn, nk), in_specs=in_specs, out_specs=out_specs,
                          out_shape=out_shape, scratch_shapes=scratch, name=name)(*args)


def final_loss(name, x, y, g, target):
    t, d = x.shape
    tile = min(ROW_TILE, t)

    def body(x_ref, y_ref, g_ref, t_ref, dx_ref, l_ref):
        i = pl.program_id(0)

        @pl.when(i == 0)
        def _():
            l_ref[...] = jnp.zeros_like(l_ref)

        err = x_ref[...] + _rms(y_ref[...], g_ref[...]) - t_ref[...]
        dx_ref[...] = err * (1.0 / d)
        l_ref[...] += 0.5 * jnp.sum(jnp.mean(err * err, axis=-1, keepdims=True), axis=0, keepdims=True)

    row = pl.BlockSpec((tile, d), lambda i: (i, 0))
    return pl.pallas_call(
        body, grid=(t // tile,), in_specs=[row, row, _full_spec(g), row],
        out_specs=[row, pl.BlockSpec((1, 1), lambda i: (0, 0))],
        out_shape=[jax.ShapeDtypeStruct((t, d), f32), jax.ShapeDtypeStruct((1, 1), f32)], name=name)(x, y, g, target)


def adamw(name, w, m, v, gslots, tr=128):
    nl, r, c = w.shape
    tr = min(tr, r)
    nr = r // tr
    ns = gslots[0].shape[0]
    c1 = 1.0 / (1.0 - ADAM_B1 ** ADAM_STEP)
    c2 = 1.0 / (1.0 - ADAM_B2 ** ADAM_STEP)

    def body(*refs):
        w_ref, m_ref, v_ref = refs[:3]
        g_refs = refs[3:3 + nl]
        go_ref, d_ref, mo_ref, vo_ref = refs[3 + nl:]
        l = pl.program_id(0)
        g = None
        for li in range(nl):
            s = g_refs[li][0].astype(f32)
            for k in range(1, ns):
                s = s + g_refs[li][k].astype(f32)
            g = s if g is None else jnp.where(l == li, s, g)
        mn = ADAM_B1 * m_ref[...] + (1.0 - ADAM_B1) * g
        vn = ADAM_B2 * v_ref[...] + (1.0 - ADAM_B2) * jnp.square(g)
        go_ref[...] = g
        mo_ref[...] = mn
        vo_ref[...] = vn
        d_ref[...] = -ADAM_LR * ((mn * c1) / (jnp.sqrt(vn * c2) + ADAM_EPS) + ADAM_WD * w_ref[...])

    blk = pl.BlockSpec((None, tr, c), lambda l, i: (l, i, 0))

    def gspec(li):
        return pl.BlockSpec((ns, tr, c), lambda l, i: (0, jnp.where(l == li, i, jnp.where(l < li, 0, nr - 1)), 0))

    return pl.pallas_call(
        body, grid=(nl, nr), in_specs=[blk, blk, blk] + [gspec(li) for li in range(nl)],
        out_specs=[blk] * 4, out_shape=[jax.ShapeDtypeStruct(w.shape, f32)] * 4, name=name)(w, m, v, *gslots)


def exchange(name, arrays, gather, group):
    n = len(arrays)
    ns = {'all': 8, 'chips': 4, 'core': 2}[group]

    def body(*refs):
        ins, outs = refs[:n], refs[n:2 * n]
        send_sems, recv_sems, loc_sems = refs[2 * n:]
        x, y, c = lax.axis_index("x"), lax.axis_index("y"), lax.axis_index("c")

        def member(k):
            if group == 'all':
                px, py, pc = x ^ ((k >> 2) & 1), y ^ ((k >> 1) & 1), c ^ (k & 1)
                return (px, py, pc), 4 * px + 2 * py + pc
            if group == 'chips':
                px, py = x ^ ((k >> 1) & 1), y ^ (k & 1)
                return (px, py, c), 2 * px + py
            return (x, y, c ^ k), c ^ k

        _, me = member(0)
        sends, recvs, locs = [], [], []
        for a in range(n):
            lc = pltpu.make_async_copy(ins[a] if gather else ins[a].at[me], outs[a].at[me], loc_sems.at[a])
            lc.start()
            locs.append(lc)
            for k in range(1, ns):
                dev, peer = member(k)
                src = ins[a] if gather else ins[a].at[peer]
                cp = pltpu.make_async_remote_copy(src_ref=src, dst_ref=outs[a].at[me], send_sem=send_sems.at[a, k],
                                                  recv_sem=recv_sems.at[a, k], device_id=dev, device_id_type=MESH)
                cp.start()
                sends.append(cp)
                recvs.append(pltpu.make_async_remote_copy(src_ref=src, dst_ref=outs[a].at[peer], send_sem=send_sems.at[a, k],
                                                          recv_sem=recv_sems.at[a, k], device_id=dev, device_id_type=MESH))
        for cp in recvs:
            cp.wait_recv()
        for cp in sends:
            cp.wait_send()
        for lc in locs:
            lc.wait()

    anyspec = pl.BlockSpec(memory_space=pl.ANY)
    out_shape = [jax.ShapeDtypeStruct(((ns,) + a.shape) if gather else a.shape, a.dtype) for a in arrays]
    return pl.pallas_call(
        body, in_specs=[anyspec] * n, out_specs=[anyspec] * n, out_shape=out_shape,
        scratch_shapes=[pltpu.SemaphoreType.DMA((n, ns)), pltpu.SemaphoreType.DMA((n, ns)),
                        pltpu.SemaphoreType.DMA((n,))], name=name)(*arrays)


def gather_two_level(name, arrays):
    n = len(arrays)

    def body(*refs):
        start, finish = _gather_parts(refs[:n], refs[n:2 * n], *refs[2 * n:])
        start()
        finish()

    anyspec = pl.BlockSpec(memory_space=pl.ANY)
    return pl.pallas_call(
        body, in_specs=[anyspec] * n, out_specs=[anyspec] * n, out_shape=_carry_out_shape('gather', arrays),
        scratch_shapes=_carry_sems('gather', n), name=name)(*arrays)


def _gather_parts(ins, outs, send_sems, recv_sems, loc_sems):
    n = len(ins)
    x, y, c = lax.axis_index("x"), lax.axis_index("y"), lax.axis_index("c")
    sib = (x, y, 1 - c)

    def chip(k):
        px, py = x ^ ((k >> 1) & 1), y ^ (k & 1)
        return (px, py), 2 * px + py

    _, mine = chip(0)

    def copy(a, sem, src, slot, to):
        return pltpu.make_async_remote_copy(src_ref=src, dst_ref=outs[a].at[slot], send_sem=send_sems.at[a, sem],
                                            recv_sem=recv_sems.at[a, sem], device_id=to, device_id_type=MESH)

    def local(a):
        return pltpu.make_async_copy(ins[a], outs[a].at[2 * mine + c], loc_sems.at[a])

    def own_sends(a):
        cps = [copy(a, 0, ins[a], 2 * mine + c, sib)]
        for k in range(1, 4):
            (px, py), _ = chip(k)
            cps.append(copy(a, k, ins[a], 2 * mine + c, (px, py, c)))
        return cps

    def start():
        for a in range(n):
            local(a).start()
            for cp in own_sends(a):
                cp.start()

    def finish():
        passed = []
        for a in range(n):
            for k in range(1, 4):
                _, other = chip(k)
                slot = 2 * other + c
                copy(a, k, outs[a].at[slot], slot, sib).wait_recv()
                fw = copy(a, 3 + k, outs[a].at[slot], slot, sib)
                fw.start()
                passed.append(fw)
        for a in range(n):
            copy(a, 0, ins[a], 2 * mine + 1 - c, sib).wait_recv()
            for k in range(1, 4):
                _, other = chip(k)
                slot = 2 * other + 1 - c
                copy(a, 3 + k, outs[a].at[slot], slot, sib).wait_recv()
        for a in range(n):
            for cp in own_sends(a):
                cp.wait_send()
        for cp in passed:
            cp.wait_send()
        for a in range(n):
            local(a).wait()

    return start, finish


def _scatter_parts(ins, outs, send_sems, recv_sems, loc_sems):
    n = len(ins)
    x, y, c = lax.axis_index("x"), lax.axis_index("y"), lax.axis_index("c")
    mine = 2 * x + y

    def local(a):
        return pltpu.make_async_copy(ins[a].at[mine], outs[a].at[mine], loc_sems.at[a])

    def remote(a, k, slot):
        px, py = x ^ ((k >> 1) & 1), y ^ (k & 1)
        return pltpu.make_async_remote_copy(src_ref=ins[a].at[2 * px + py], dst_ref=outs[a].at[slot],
                                            send_sem=send_sems.at[a, k], recv_sem=recv_sems.at[a, k],
                                            device_id=(px, py, c), device_id_type=MESH)

    def start():
        for a in range(n):
            local(a).start()
            for k in range(1, 4):
                remote(a, k, mine).start()

    def finish():
        for a in range(n):
            for k in range(1, 4):
                remote(a, k, 2 * (x ^ ((k >> 1) & 1)) + (y ^ (k & 1))).wait_recv()
        for a in range(n):
            for k in range(1, 4):
                remote(a, k, mine).wait_send()
            local(a).wait()

    return start, finish


def _carry_out_shape(kind, arrays):
    if kind == 'gather':
        return [jax.ShapeDtypeStruct((N_DEV,) + a.shape, a.dtype) for a in arrays]
    return [jax.ShapeDtypeStruct(a.shape, a.dtype) for a in arrays]


def _carry_sems(kind, n):
    k = 7 if kind == 'gather' else 4
    return [pltpu.SemaphoreType.DMA((n, k)), pltpu.SemaphoreType.DMA((n, k)), pltpu.SemaphoreType.DMA((n,))]


def _carry_parts(kind, ins, outs, sems):
    return (_gather_parts if kind == 'gather' else _scatter_parts)(ins, outs, *sems)


def scatter_chips(name, arrays):
    n = len(arrays)

    def body(*refs):
        start, finish = _scatter_parts(refs[:n], refs[n:2 * n], *refs[2 * n:])
        start()
        finish()

    anyspec = pl.BlockSpec(memory_space=pl.ANY)
    return pl.pallas_call(
        body, in_specs=[anyspec] * n, out_specs=[anyspec] * n, out_shape=_carry_out_shape('scatter', arrays),
        scratch_shapes=_carry_sems('scatter', n), name=name)(*arrays)


def chip_partials(tag, slots):
    from_sibling = send_to_sibling(f"to_sibling_{tag}", slots)
    return [chip_sum(f"chip_sum_{tag}{i}", b, o) for i, (b, o) in enumerate(zip(slots, from_sibling))]


def send_to_sibling(name, arrays):
    n = len(arrays)

    def body(*refs):
        ins, outs = refs[:n], refs[n:2 * n]
        send_sems, recv_sems = refs[2 * n:]
        x, y, c = lax.axis_index("x"), lax.axis_index("y"), lax.axis_index("c")
        cps = [pltpu.make_async_remote_copy(src_ref=ins[a].at[1 - c], dst_ref=outs[a], send_sem=send_sems.at[a],
                                            recv_sem=recv_sems.at[a], device_id=(x, y, 1 - c), device_id_type=MESH)
               for a in range(n)]
        for cp in cps:
            cp.start()
        for cp in cps:
            cp.wait()

    anyspec = pl.BlockSpec(memory_space=pl.ANY)
    out_shape = [jax.ShapeDtypeStruct(a.shape[1:], a.dtype) for a in arrays]
    return pl.pallas_call(
        body, in_specs=[anyspec] * n, out_specs=[anyspec] * n, out_shape=out_shape,
        scratch_shapes=[pltpu.SemaphoreType.DMA((n,)), pltpu.SemaphoreType.DMA((n,))], name=name)(*arrays)


def chip_sum(name, both, other):
    _, nc, r, c = both.shape
    tr = min(ROW_TILE, r)

    def body(b_ref, o_ref, s_ref):
        core = lax.axis_index("c")
        own = jnp.where(core == 0, b_ref[0], b_ref[1]).astype(f32)
        s_ref[...] = (own + o_ref[...].astype(f32)).astype(s_ref.dtype)

    return pl.pallas_call(
        body, grid=(nc, r // tr),
        in_specs=[pl.BlockSpec((2, None, tr, c), lambda s, i: (0, s, i, 0)),
                  pl.BlockSpec((None, tr, c), lambda s, i: (s, i, 0))],
        out_specs=pl.BlockSpec((None, tr, c), lambda s, i: (s, i, 0)),
        out_shape=jax.ShapeDtypeStruct((nc, r, c), both.dtype), name=name)(both, other)


REPLICATED = ('lower_bounds', 'norm_mix_pre', 'norm_mix_post', 'norm_ff_pre', 'norm_ff_post', 'hgrn_norm_w',
              'gdn_a_log', 'gdn_dt_bias', 'gdn_norm_w', 'gmlp_ln_w', 'gmlp_ln_b', 'gmlp_w_s', 'gmlp_b_s',
              'conv_dw_b', 'conv_ln_w', 'conv_ln_b')
WEIGHTS = ('lower_bounds', 'norm_mix_pre', 'norm_mix_post', 'norm_ff_pre', 'norm_ff_post', 'w_in', 'w_out',
           'hgrn_norm_w', 'gdn_conv_w', 'gdn_a_log', 'gdn_dt_bias', 'gdn_norm_w', 'gmlp_ln_w', 'gmlp_ln_b',
           'gmlp_w_s', 'gmlp_b_s', 'conv_dw_w', 'conv_dw_b', 'conv_ln_w', 'conv_ln_b', 'w_ff1', 'w_ff2')


def _row(v):
    return v.reshape(1, -1)


def _pad_lanes(v, offset):
    return jnp.pad(v, (offset, 128 - offset - v.shape[0])).reshape(1, 128)


def _relayout_w_in(g):
    full = jnp.moveaxis(g[:, 0], 0, 1).reshape(D_MODEL, D_IN)
    return jnp.concatenate([full[:, :8 * GW], full[:, 8 * GW + 2 * NH:], full[:, 8 * GW:8 * GW + 2 * NH],
                            jnp.zeros((D_MODEL, 128 - 2 * NH), bf16)], axis=1)


def _layer_fwd(l, x0, params, lb_all, shards):
    p = params[l]
    sv = {'x0': x0}
    (h,) = rowwise(f"norm_mix_pre{l}", fn_norm, [(x0, D_MODEL, 0)], [p['g_mix_pre']], [(D_MODEL, bf16)], ROW_TILE)
    proj = matmul(f"proj{l}", h, p['w_in'], 'nn', f32, tn=896)
    sv.update(h=h, proj=proj)
    lb = lb_all[l:l + 1]
    o_a, st_a, (p['w_ff1'],) = scan_fwd(f"hgrn{l}", fn_hgrn, [(proj, 0), (proj, 4), (proj, 8), (proj, 12)], [lb],
                                        [p['hgrn_norm_w']], bf16, NH, carry=('gather', [shards['w_ff1', l]]))
    q, k, v, beta, g = halo_fwd(
        f"gdn_pre{l}", fn_gdn_pre,
        [(proj, GW, 4, True), (proj, GW, 5, True), (proj, GW, 6, True), (proj, 128, 48, False)],
        [p['gdn_conv_w'], p['alog'], p['dtb']], [(GW, f32)] * 5, ROW_TILE, 8)
    wanted = [shards['w_ff2', l]] + ([shards['w_out']] if l == 0 else []) + ([shards['w_in', l + 1]] if l + 1 < DEPTH else [])
    o_b, st_b, tinv_b, got = scan_fwd(f"gdn{l}", fn_gdn, [(q, 0), (k, 0), (v, 0), (beta, 0), (g, 0), (proj, 28)], [],
                                      [p['gdn_norm_w']], bf16, NH, n_extra=1, carry=('gather', wanted))
    p['w_ff2'] = got.pop(0)
    if l == 0:
        w_out_full = jnp.moveaxis(got.pop(0), 0, 1).reshape(DEPTH, D_MODEL, D_MODEL)
        for ll in range(DEPTH):
            params[ll]['w_out'] = w_out_full[ll]
    if l + 1 < DEPTH:
        params[l + 1]['w_in'] = _relayout_w_in(got.pop(0))
    (o_c,) = rowwise(f"gmlp{l}", fn_gmlp, [(proj, GW, 8), (proj, GW, 9)],
                     [p['gmlp_ln_w'], p['gmlp_ln_b'], p['gmlp_w_s'], p['gmlp_b_s']], [(GW, bf16)], MIX_CHUNK)
    (o_d,) = halo_fwd(f"conv{l}", fn_conv, [(proj, GW, 10, True), (proj, GW, 11, True)],
                      [p['conv_dw_w'], p['conv_dw_b'], p['conv_ln_w'], p['conv_ln_b']], [(GW, bf16)], ROW_TILE, 32)
    mix = jnp.concatenate([o_a, o_b, o_c, o_d], axis=1)
    y1 = matmul(f"out_proj{l}", mix, p['w_out'], 'nn', f32)
    (x1,) = rowwise(f"res_mix{l}", lambda x, y, gg: (x + _rms(y, gg),), [(x0, D_MODEL, 0), (y1, D_MODEL, 0)],
                    [p['g_mix_post']], [(D_MODEL, f32)], ROW_TILE)
    (h2,) = rowwise(f"norm_ff_pre{l}", fn_norm, [(x1, D_MODEL, 0)], [p['g_ff_pre']], [(D_MODEL, bf16)], ROW_TILE)
    u, a = matmul(f"ff1_{l}", h2, p['w_ff1'], 'nn', epilogue='relu2', b_gathered=('cols', 0))
    y2 = matmul(f"ff2_{l}", a, p['w_ff2'], 'nn', f32, b_gathered=('rows', 0))
    sv.update(st_a=st_a, q=q, k=k, v=v, beta=beta, g=g, st_b=st_b, tinv_b=tinv_b, mix=mix, y1=y1, x1=x1, h2=h2, u=u, a=a, y2=y2)
    return sv


def _slots_w_in(gp):
    gl = jnp.concatenate([gp[:, :8 * GW], gp[:, 12 * GW:12 * GW + 2 * NH], gp[:, 8 * GW:12 * GW]], axis=1)
    return jnp.transpose(gl.reshape(D_MODEL, N_DEV // 2, 2, D_IN // N_DEV), (2, 1, 0, 3))


def _layer_bwd(l, dx, sv, p, lb_all, pending, received):
    gr = {}
    (dy2,), (gr['norm_ff_post'],) = rowwise_bwd(f"res_ff_bwd{l}", fn_norm, [(sv['y2'], D_MODEL, 0)], [p['g_ff_post']],
                                                [(dx, D_MODEL, 0)], [bf16], ROW_TILE)
    du = matmul(f"ff2_dx{l}", dy2, p['w_ff2'], 'nt', bf16, epilogue='relu2_bwd', extra=sv['u'], b_gathered=('rows', 0))
    g_ff2 = matmul(f"ff2_dw{l}", sv['a'], dy2, 'tn', bf16, slots='rows')
    dh2 = matmul(f"ff1_dx{l}", du, p['w_ff1'], 'nt', f32, b_gathered=('cols', 0))
    g_ff1 = matmul(f"ff1_dw{l}", sv['h2'], du, 'tn', bf16, slots='cols')
    (dx1,), (gr['norm_ff_pre'],) = rowwise_bwd(f"norm_ff_pre_bwd{l}", fn_norm, [(sv['x1'], D_MODEL, 0)], [p['g_ff_pre']],
                                               [(dh2, D_MODEL, 0)], [f32], ROW_TILE, addto=(dx, D_MODEL, 0))
    (dy1,), (gr['norm_mix_post'],) = rowwise_bwd(f"res_mix_bwd{l}", fn_norm, [(sv['y1'], D_MODEL, 0)], [p['g_mix_post']],
                                                 [(dx1, D_MODEL, 0)], [bf16], ROW_TILE)
    dmix = matmul(f"out_proj_dx{l}", dy1, p['w_out'], 'nt', f32)
    g_out = matmul(f"out_proj_dw{l}", sv['mix'], dy1, 'tn', bf16, slots='rows')
    part_ff2, part_ff1, part_out = chip_partials(f"l{l}_", [g_ff2, g_ff1, g_out])
    proj = sv['proj']
    lb = lb_all[l:l + 1]
    going = [(('w_ff2', l), part_ff2)] + pending
    d_a, (dlb,), (gr['hgrn_norm_w'],), got = scan_bwd(
        f"hgrn_bwd{l}", fn_hgrn, [(proj, 0), (proj, 4), (proj, 8), (proj, 12)], [lb], [p['hgrn_norm_w']],
        sv['st_a'], (dmix, 0), [bf16] * 4, NH, carry=('scatter', [a for _, a in going]))
    received.update({key: r for (key, _), r in zip(going, got)})
    going = [(('w_ff1', l), part_ff1), (('w_out', l), part_out)]
    d_b, _, (gr['gdn_norm_w'],), got = scan_bwd(
        f"gdn_bwd{l}", fn_gdn, [(sv['q'], 0), (sv['k'], 0), (sv['v'], 0), (sv['beta'], 0), (sv['g'], 0), (proj, 28)],
        [], [p['gdn_norm_w']], sv['st_b'], (dmix, 4), [f32] * 5 + [bf16], NH, extras=[sv['tinv_b']],
        carry=('scatter', [a for _, a in going]))
    received.update({key: r for (key, _), r in zip(going, got)})
    d_bp, (gr['gdn_conv_w'], dalog, ddtb) = halo_bwd(
        f"gdn_pre_bwd{l}", fn_gdn_pre,
        [(proj, GW, 4, True), (proj, GW, 5, True), (proj, GW, 6, True), (proj, 128, 48, False)],
        [p['gdn_conv_w'], p['alog'], p['dtb']], [(d_b[j], GW, 0) for j in range(5)], bf16, ROW_TILE, 8)
    gr['gdn_a_log'] = dalog[0, NH:2 * NH]
    gr['gdn_dt_bias'] = ddtb[0, NH:2 * NH]
    d_c, (gr['gmlp_ln_w'], gr['gmlp_ln_b'], gr['gmlp_w_s'], gr['gmlp_b_s']) = rowwise_bwd(
        f"gmlp_bwd{l}", fn_gmlp, [(proj, GW, 8), (proj, GW, 9)],
        [p['gmlp_ln_w'], p['gmlp_ln_b'], p['gmlp_w_s'], p['gmlp_b_s']], [(dmix, GW, 2)], [bf16, bf16], MIX_CHUNK)
    d_d, (gr['conv_dw_w'], gr['conv_dw_b'], gr['conv_ln_w'], gr['conv_ln_b']) = halo_bwd(
        f"conv_bwd{l}", fn_conv, [(proj, GW, 10, True), (proj, GW, 11, True)],
        [p['conv_dw_w'], p['conv_dw_b'], p['conv_ln_w'], p['conv_ln_b']], [(dmix, GW, 3)], bf16, ROW_TILE, 32)
    dproj = jnp.concatenate(list(d_a) + [d_bp[0], d_bp[1], d_bp[2], d_b[5]] + list(d_c) + list(d_d) + [d_bp[3]], axis=1)
    dh = matmul(f"proj_dx{l}", dproj, p['w_in'], 'nt', f32, tk=896)
    g_in = matmul(f"proj_dw{l}", sv['h'], dproj, 'tn', bf16, tn=896)
    (part_in,) = chip_partials(f"l{l}_in", [_slots_w_in(g_in)])
    (dx0,), (gr['norm_mix_pre'],) = rowwise_bwd(f"norm_mix_pre_bwd{l}", fn_norm, [(sv['x0'], D_MODEL, 0)], [p['g_mix_pre']],
                                                [(dh, D_MODEL, 0)], [f32], ROW_TILE, addto=(dx1, D_MODEL, 0))
    return dx0, gr, dlb, [(('w_in', l), part_in)]


def kernel(x, lower_bounds, norm_mix_pre, norm_mix_post, norm_ff_pre, norm_ff_post, w_in, w_out, hgrn_norm_w, gdn_conv_w, gdn_a_log, gdn_dt_bias, gdn_norm_w, gmlp_ln_w, gmlp_ln_b, gmlp_w_s, gmlp_b_s, conv_dw_w, conv_dw_b, conv_ln_w, conv_ln_b, w_ff1, w_ff2, loss_target, m_lower_bounds, m_norm_mix_pre, m_norm_mix_post, m_norm_ff_pre, m_norm_ff_post, m_w_in, m_w_out, m_hgrn_norm_w, m_gdn_conv_w, m_gdn_a_log, m_gdn_dt_bias, m_gdn_norm_w, m_gmlp_ln_w, m_gmlp_ln_b, m_gmlp_w_s, m_gmlp_b_s, m_conv_dw_w, m_conv_dw_b, m_conv_ln_w, m_conv_ln_b, m_w_ff1, m_w_ff2, v_lower_bounds, v_norm_mix_pre, v_norm_mix_post, v_norm_ff_pre, v_norm_ff_post, v_w_in, v_w_out, v_hgrn_norm_w, v_gdn_conv_w, v_gdn_a_log, v_gdn_dt_bias, v_gdn_norm_w, v_gmlp_ln_w, v_gmlp_ln_b, v_gmlp_w_s, v_gmlp_b_s, v_conv_dw_w, v_conv_dw_b, v_conv_ln_w, v_conv_ln_b, v_w_ff1, v_w_ff2):
    loc = dict(locals())
    W = {n: loc[n] for n in WEIGHTS}
    M = {n: loc['m_' + n] for n in WEIGHTS}
    V = {n: loc['v_' + n] for n in WEIGHTS}
    t = x.shape[1]
    me = 4 * lax.axis_index("x") + 2 * lax.axis_index("y") + lax.axis_index("c")

    shards = {'w_out': w_out.astype(bf16)}
    for l in range(DEPTH):
        shards['w_in', l] = w_in[l:l + 1].astype(bf16)
        shards['w_ff1', l] = w_ff1[l:l + 1].astype(bf16)
        shards['w_ff2', l] = w_ff2[l:l + 1].astype(bf16)
    g_in0, g_gconv, g_dconv = gather_two_level("gather_weights", [shards['w_in', 0], gdn_conv_w, conv_dw_w])
    gconv_full = jnp.moveaxis(g_gconv, 0, 2).reshape(DEPTH, SHORT_CONV, 3 * GW)
    dconv_full = jnp.moveaxis(g_dconv, 0, 2).reshape(DEPTH, CONV_WIDTH, GW)

    (lb_all,) = rowwise("lower_bounds", fn_lb, [(lower_bounds, GW, 0)], [], [(GW, f32)], DEPTH)

    P = []
    for l in range(DEPTH):
        P.append(dict(
            g_mix_pre=_row(norm_mix_pre[l]), g_mix_post=_row(norm_mix_post[l]), g_ff_pre=_row(norm_ff_pre[l]),
            g_ff_post=_row(norm_ff_post[l]), hgrn_norm_w=_row(hgrn_norm_w[l]), gdn_conv_w=gconv_full[l],
            alog=_pad_lanes(gdn_a_log[l], NH), dtb=_pad_lanes(gdn_dt_bias[l], NH), gdn_norm_w=_row(gdn_norm_w[l]),
            gmlp_ln_w=_row(gmlp_ln_w[l]), gmlp_ln_b=_row(gmlp_ln_b[l]), gmlp_w_s=gmlp_w_s[l].reshape(NH * MIX_CHUNK, MIX_CHUNK),
            gmlp_b_s=gmlp_b_s[l], conv_dw_w=dconv_full[l], conv_dw_b=_row(conv_dw_b[l]), conv_ln_w=_row(conv_ln_w[l]),
            conv_ln_b=_row(conv_ln_b[l])))

    P[0]['w_in'] = _relayout_w_in(g_in0)
    xs = x[0]
    saved = []
    for l in range(DEPTH):
        sv = _layer_fwd(l, xs, P, lb_all, shards)
        saved.append(sv)
        if l < DEPTH - 1:
            (xs,) = rowwise(f"res_ff{l}", lambda a, y, gg: (a + _rms(y, gg),), [(sv['x1'], D_MODEL, 0), (sv['y2'], D_MODEL, 0)],
                            [P[l]['g_ff_post']], [(D_MODEL, f32)], ROW_TILE)
    sv = saved[-1]
    dx, loss_loc = final_loss("final_loss", sv['x1'], sv['y2'], P[-1]['g_ff_post'], loss_target[0])
    loss = lax.psum(loss_loc[0, 0], ("x", "y", "c"))

    G = {}
    dlb_rows = []
    pending, received = [], {}
    for l in reversed(range(DEPTH)):
        dx, gr, dlb, pending = _layer_bwd(l, dx, saved[l], P[l], lb_all, pending, received)
        G[l] = gr
        dlb_rows.append(dlb)
    got = scatter_chips("scatter_grads_tail", [a for _, a in pending])
    received.update({key: r for (key, _), r in zip(pending, got)})
    dlb_all = jnp.concatenate(dlb_rows[::-1], axis=0)
    (g_lower_bounds,), _ = rowwise_bwd("lower_bounds_bwd", fn_lb, [(lower_bounds, GW, 0)], [], [(dlb_all, GW, 0)],
                                       [f32], DEPTH)
    grad_x = dx[None]

    def stack(name, f=lambda a: a):
        return jnp.stack([f(G[l][name]) for l in range(DEPTH)], axis=0)

    full = {
        'lower_bounds': g_lower_bounds,
        'norm_mix_pre': stack('norm_mix_pre', lambda a: a[0]), 'norm_mix_post': stack('norm_mix_post', lambda a: a[0]),
        'norm_ff_pre': stack('norm_ff_pre', lambda a: a[0]), 'norm_ff_post': stack('norm_ff_post', lambda a: a[0]),
        'hgrn_norm_w': stack('hgrn_norm_w', lambda a: a[0]), 'gdn_a_log': stack('gdn_a_log'), 'gdn_dt_bias': stack('gdn_dt_bias'),
        'gdn_norm_w': stack('gdn_norm_w', lambda a: a[0]), 'gmlp_ln_w': stack('gmlp_ln_w', lambda a: a[0]),
        'gmlp_ln_b': stack('gmlp_ln_b', lambda a: a[0]),
        'gmlp_w_s': stack('gmlp_w_s', lambda a: a.reshape(NH, MIX_CHUNK, MIX_CHUNK)), 'gmlp_b_s': stack('gmlp_b_s'),
        'conv_dw_b': stack('conv_dw_b', lambda a: a[0]), 'conv_ln_w': stack('conv_ln_w', lambda a: a[0]),
        'conv_ln_b': stack('conv_ln_b', lambda a: a[0]),
        'gdn_conv_w': stack('gdn_conv_w'), 'conv_dw_w': stack('conv_dw_w'),
    }

    small_names = list(REPLICATED) + ['gdn_conv_w', 'conv_dw_w']
    flat = jnp.concatenate([full[n].reshape(-1) for n in small_names])
    n_small = flat.shape[0]
    n_pad = -(-n_small // 1024) * 1024
    packed = jnp.pad(flat, (0, n_pad - n_small)).reshape(n_pad // 128, 128)

    (small_slots,) = exchange("gather_small_grads", [packed], True, 'all')

    out = {}
    for name in ('w_in', 'w_out', 'w_ff1', 'w_ff2'):
        out[name] = adamw(f"adamw_{name}", W[name], M[name], V[name], [received[name, l] for l in range(DEPTH)])

    def pack(d, fill):
        parts = [d[n].reshape(-1) for n in REPLICATED]
        parts.append(jnp.full((n_pad - sum(a.shape[0] for a in parts),), fill, f32))
        return jnp.concatenate(parts).reshape(1, n_pad // 128, 128)

    sm = adamw("adamw_small", pack(W, 0.0), pack(M, 0.0), pack(V, 1.0), [small_slots], tr=n_pad // 128)
    off = 0
    for n in REPLICATED:
        sz = W[n].size
        out[n] = tuple(a.reshape(-1)[off:off + sz].reshape(W[n].shape) for a in sm)
        off += sz
    gsum = sm[0].reshape(-1)
    for n, full_shape in (('gdn_conv_w', (DEPTH, SHORT_CONV, 3 * GW)), ('conv_dw_w', (DEPTH, CONV_WIDTH, GW))):
        sz = math.prod(full_shape)
        gfull = gsum[off:off + sz].reshape(full_shape)
        off += sz
        sh = W[n].shape
        gmine = lax.dynamic_slice_in_dim(gfull, me * sh[2], sh[2], axis=2)
        r = adamw(f"adamw_{n}", W[n].reshape(1, sh[0] * sh[1], sh[2]), M[n].reshape(1, sh[0] * sh[1], sh[2]),
                  V[n].reshape(1, sh[0] * sh[1], sh[2]), [gmine.reshape(1, sh[0] * sh[1], sh[2])], tr=sh[0] * sh[1])
        out[n] = tuple(a.reshape(sh) for a in r)

    return (loss, grad_x, *[out[n][0] for n in WEIGHTS], *[out[n][1] for n in WEIGHTS],
            *[out[n][2] for n in WEIGHTS], *[out[n][3] for n in WEIGHTS])
```

```python
import functools
import math

import jax
import jax.numpy as jnp
from jax import lax
from jax.experimental import pallas as pl
from jax.experimental.pallas import tpu as pltpu

f32 = jnp.float32
bf16 = jnp.bfloat16
HI = lax.Precision.HIGHEST

N_DEV = 8
DEPTH = 2
D_MODEL = 2048
GW = 512
HD = 128
NH = 4
CHUNK = 64
MIX_CHUNK = 128
CONV_WIDTH = 31
SHORT_CONV = 4
D_FF = 4 * D_MODEL
D_IN = 12 * GW + 2 * NH
D_IN_PAD = 12 * GW + 128
ROW_TILE = 256
HGRN_SUB = 16
SCAN_CHUNKS = 2
EPS = 1e-6
TINY = 1e-30
ADAM_LR, ADAM_B1, ADAM_B2, ADAM_EPS, ADAM_WD, ADAM_STEP = 0.001, 0.9, 0.999, 1e-08, 0.01, 10
MESH = pl.DeviceIdType.MESH


def _dotb(a, b, ca, cb):
    return lax.dot_general(a.astype(bf16), b.astype(bf16), (((ca,), (cb,)), ((), ())),
                           preferred_element_type=f32)


@jax.custom_vjp
def mm(a, b):
    return _dotb(a, b, 1, 0)


def _mm_f(a, b):
    return mm(a, b), (a, b)


def _mm_b(res, ct):
    a, b = res
    return _dotb(ct, b, 1, 1), _dotb(a, ct, 0, 0)


mm.defvjp(_mm_f, _mm_b)


@jax.custom_vjp
def mm_nt(a, b):
    return _dotb(a, b, 1, 1)


def _mmnt_f(a, b):
    return mm_nt(a, b), (a, b)


def _mmnt_b(res, ct):
    a, b = res
    return _dotb(ct, b, 1, 0), _dotb(ct, a, 0, 0)


mm_nt.defvjp(_mmnt_f, _mmnt_b)


@jax.custom_vjp
def mm_tn(a, b):
    return _dotb(a, b, 0, 0)


def _mmtn_f(a, b):
    return mm_tn(a, b), (a, b)


def _mmtn_b(res, ct):
    a, b = res
    return _dotb(b, ct, 1, 1), _dotb(a, ct, 1, 0)


mm_tn.defvjp(_mmtn_f, _mmtn_b)


def mmh(a, b):
    return jnp.dot(a, b, precision=HI, preferred_element_type=f32)


def mm3(a, b):
    return jnp.dot(a, b, precision=lax.Precision.HIGH, preferred_element_type=f32)


def _rms(x, w):
    return x * lax.rsqrt(jnp.mean(x * x, axis=-1, keepdims=True) + EPS) * w


def _ln(x, w, b):
    mu = jnp.mean(x, axis=-1, keepdims=True)
    xc = x - mu
    var = jnp.mean(xc * xc, axis=-1, keepdims=True)
    return xc * lax.rsqrt(var + EPS) * w + b


def _gelu(x):
    return 0.5 * x * (1.0 + lax.erf(x * (2.0 ** -0.5)))


def _iota2(n, m, axis):
    return lax.broadcasted_iota(jnp.int32, (n, m), axis)


def _tri(n, strict=False):
    r, c = _iota2(n, n, 0), _iota2(n, n, 1)
    return (r > c) if strict else (r >= c)


def _eye(n):
    return (_iota2(n, n, 0) == _iota2(n, n, 1)).astype(f32)


def fn_norm(x, g):
    return (_rms(x, g),)


def fn_lb(lower_bounds):
    s = jax.nn.softmax(lower_bounds, axis=0)
    rows, cum = [], None
    for i in range(DEPTH):
        cum = s[i:i + 1] if cum is None else cum + s[i:i + 1]
        rows.append(cum - s[0:1])
    return (jnp.concatenate(rows, axis=0),)


def fn_hgrn(aq, af, ai, ag, lb, nw, st):
    c = aq.shape[0]
    sig = jax.nn.sigmoid(af)
    f = lb + (1.0 - lb) * sig
    logf = jnp.log(jnp.maximum(f, TINY))
    k = (1.0 - lb) * jax.nn.sigmoid(-af)
    q = jax.nn.silu(aq)
    v = ai
    b = mmh(_tri(c).astype(f32), logf)
    outs = []
    for lo in range(0, c, HGRN_SUB):
        qi, ki, vi, bi = (a[lo:lo + HGRN_SUB] for a in (q, k, v, b))
        rel = bi[:, None, :] - bi[None, :, :]
        dec = jnp.exp(jnp.minimum(rel, 0.0)) * qi[:, None, :] * ki[None, :, :]
        o_blk = mm(jnp.where(_tri(HGRN_SUB), jnp.sum(dec, axis=-1), 0.0), vi)
        if lo > 0:
            r = b[lo - 1:lo, :]
            o_blk = o_blk + mm(mm_nt(qi * jnp.exp(bi - r), k[:lo] * jnp.exp(r - b[:lo])), v[:lo])
        outs.append(o_blk)
    b_end = b[c - 1:c, :]
    out = jnp.concatenate(outs, axis=0) + mm_nt(q * jnp.exp(b), st)
    st_new = st * jnp.exp(b_end) + mm_tn(v, k * jnp.exp(b_end - b))
    o = _rms(out, nw) * jax.nn.silu(ag)
    return o, st_new


def fn_gdn_pre(tq, tk, tv, bq, bk, bv, p8, conv_w, alog, dtb):
    tile = bq.shape[0]
    h = tq.shape[0]
    outs = []
    for seg, (tl, cur) in enumerate(((tq, bq), (tk, bk), (tv, bv))):
        xe = jnp.concatenate([tl, cur], axis=0)
        acc = None
        for kk in range(SHORT_CONV):
            off = h - (SHORT_CONV - 1) + kk
            term = conv_w[kk:kk + 1, seg * GW:(seg + 1) * GW] * xe[off:off + tile, :]
            acc = term if acc is None else acc + term
        outs.append(jax.nn.silu(acc))
    sq, sk, sv = outs
    qh, kh = [], []
    for hh in range(NH):
        a = sq[:, hh * HD:(hh + 1) * HD]
        qh.append(a * lax.rsqrt(jnp.sum(a * a, axis=-1, keepdims=True) + EPS) * (HD ** -0.5))
        a = sk[:, hh * HD:(hh + 1) * HD]
        kh.append(a * lax.rsqrt(jnp.sum(a * a, axis=-1, keepdims=True) + EPS))
    q = jnp.concatenate(qh, axis=1)
    k = jnp.concatenate(kh, axis=1)
    beta = jax.nn.sigmoid(p8)
    g = -jnp.exp(alog) * jax.nn.softplus(p8 + dtb)
    r, cc = _iota2(128, GW, 0), _iota2(128, GW, 1) // HD
    e_beta = (r == cc).astype(f32)
    e_g = (r == cc + NH).astype(f32)
    return q, k, sv, mmh(beta, e_beta), mmh(g, e_g)


@jax.custom_vjp
def _inverse_given(m, tinv):
    return tinv


def _inverse_given_f(m, tinv):
    return tinv, tinv


def _inverse_given_b(tinv, ct):
    x = lax.dot_general(ct, tinv, (((1,), (1,)), ((), ())), precision=HI, preferred_element_type=f32)
    dm = -lax.dot_general(tinv, x, (((0,), (0,)), ((), ())), precision=HI, preferred_element_type=f32)
    return dm, jnp.zeros_like(tinv)


_inverse_given.defvjp(_inverse_given_f, _inverse_given_b)


def fn_gdn(q, k, v, beta, g, z, nw, s, tinv_saved=None):
    c = q.shape[0]
    gc = mmh(_tri(c).astype(f32), g)
    gcol = gc[:, 0:1]
    grow = jnp.sum(gcol * _eye(c), axis=0, keepdims=True)
    gamma = jnp.where(_tri(c), jnp.exp(jnp.minimum(gcol - grow, 0.0)), 0.0)
    kb = k * beta
    m = jnp.where(_tri(c, strict=True), mm_nt(kb, k) * gamma, 0.0)
    if tinv_saved is None:
        eye = _eye(c)
        tinv = eye - m
        p = m
        for _ in range(int(math.log2(c)) - 1):
            p = mm3(p, p)
            tinv = mm3(tinv, eye + p)
    else:
        tinv = _inverse_given(m, tinv_saved)
    egc = jnp.exp(gc)
    u = mmh(tinv, v * beta)
    w = mmh(tinv, kb * egc)
    qk = mm_nt(q, k) * gamma
    gc_end = gc[c - 1:c, :]
    q_dec = q * egc
    k_dec = k * jnp.exp(gc_end - gc)
    v_new = u - mm(w, s)
    out = mm(q_dec, s) + mm(qk, v_new)
    s_new = s * jnp.exp(gc_end) + mm_tn(k_dec, v_new)
    o = _rms(out, nw) * jax.nn.silu(z)
    return (o, s_new, tinv) if tinv_saved is None else (o, s_new)


def fn_gmlp(cu, cv, ln_w, ln_b, w_s, b_s):
    n = cu.shape[0]
    ug = _gelu(cu)
    vn = _ln(_gelu(cv), ln_w, ln_b)
    eye = _eye(n)
    cols = []
    for hh in range(NH):
        wc = jnp.where(_tri(n), w_s[hh * n:(hh + 1) * n, :], 0.0)
        bcol = jnp.sum(b_s[hh:hh + 1, :] * eye, axis=1, keepdims=True)
        cols.append(mm(wc, vn[:, hh * HD:(hh + 1) * HD]) + bcol)
    return (ug * jnp.concatenate(cols, axis=1),)


def fn_conv(ta, tg, a, gate, dw_w, dw_b, ln_w, ln_b):
    tile = a.shape[0]
    h = ta.shape[0]
    ya = jnp.concatenate([ta, a], axis=0)
    yg = jnp.concatenate([tg, gate], axis=0)
    y = ya * jax.nn.sigmoid(yg)
    acc = None
    for kk in range(CONV_WIDTH):
        off = h - (CONV_WIDTH - 1) + kk
        term = dw_w[kk:kk + 1, :] * y[off:off + tile, :]
        acc = term if acc is None else acc + term
    return (jax.nn.silu(_ln(acc + dw_b, ln_w, ln_b)),)


def _full_spec(arr):
    nd = arr.ndim
    return pl.BlockSpec(arr.shape, lambda *_: (0,) * nd)


def rowwise(name, fn, tiled, params, outs, tile):
    t = tiled[0][0].shape[0]
    tile = min(tile, t)
    nt, np_ = len(tiled), len(params)

    def body(*refs):
        vals = [r[...].astype(f32) for r in refs[:nt + np_]]
        res = fn(*vals)
        for o_ref, r in zip(refs[nt + np_:], res):
            o_ref[...] = r.astype(o_ref.dtype)

    in_specs = [pl.BlockSpec((tile, w), lambda i, c=c: (i, c)) for _, w, c in tiled]
    in_specs += [_full_spec(p) for p in params]
    out_specs = [pl.BlockSpec((tile, w), lambda i: (i, 0)) for w, _ in outs]
    out_shape = [jax.ShapeDtypeStruct((t, w), dt) for w, dt in outs]
    return pl.pallas_call(body, grid=(t // tile,), in_specs=in_specs, out_specs=out_specs,
                          out_shape=out_shape, name=name)(*[a for a, _, _ in tiled], *params)


def rowwise_bwd(name, fn, tiled, params, cots, gouts, tile, addto=None):
    t = tiled[0][0].shape[0]
    tile = min(tile, t)
    nt, np_, nc = len(tiled), len(params), len(cots)
    na = 0 if addto is None else 1
    gidx = [i for i, g in enumerate(gouts) if g is not None]

    def body(*refs):
        i = pl.program_id(0)
        vals = [r[...].astype(f32) for r in refs[:nt + np_]]
        cvals = tuple(r[...].astype(f32) for r in refs[nt + np_:nt + np_ + nc])
        _, vjp = jax.vjp(fn, *vals)
        grads = vjp(cvals)
        orefs = refs[nt + np_ + nc + na:]
        for n, j in enumerate(gidx):
            g = grads[j]
            if na and n == 0:
                g = g + refs[nt + np_ + nc][...].astype(f32)
            orefs[n][...] = g.astype(orefs[n].dtype)
        prefs = orefs[len(gidx):]

        @pl.when(i == 0)
        def _():
            for r in prefs:
                r[...] = jnp.zeros_like(r)

        for r, g in zip(prefs, grads[nt:]):
            r[...] += g

    in_specs = [pl.BlockSpec((tile, w), lambda i, c=c: (i, c)) for _, w, c in tiled]
    in_specs += [_full_spec(p) for p in params]
    in_specs += [pl.BlockSpec((tile, w), lambda i, c=c: (i, c)) for _, w, c in cots]
    args = [a for a, _, _ in tiled] + list(params) + [a for a, _, _ in cots]
    if na:
        in_specs.append(pl.BlockSpec((tile, addto[1]), lambda i, c=addto[2]: (i, c)))
        args.append(addto[0])
    out_specs = [pl.BlockSpec((tile, tiled[j][1]), lambda i: (i, 0)) for j in gidx]
    out_shape = [jax.ShapeDtypeStruct((t, tiled[j][1]), gouts[j]) for j in gidx]
    out_specs += [_full_spec(p) for p in params]
    out_shape += [jax.ShapeDtypeStruct(p.shape, f32) for p in params]
    res = pl.pallas_call(body, grid=(t // tile,), in_specs=in_specs, out_specs=out_specs,
                         out_shape=out_shape, name=name)(*args)
    return res[:len(gidx)], res[len(gidx):]


def halo_fwd(name, fn, tiled, params, outs, tile, halo):
    t = tiled[0][0].shape[0]
    tile = min(tile, t)
    hal = [j for j, x in enumerate(tiled) if x[3]]
    nt, nh, np_ = len(tiled), len(hal), len(params)
    per = tile // halo

    def body(*refs):
        i = pl.program_id(0)
        first = (i > 0).astype(f32)
        tails = [r[...].astype(f32) * first for r in refs[:nh]]
        vals = [r[...].astype(f32) for r in refs[nh:nh + nt + np_]]
        res = fn(*tails, *vals)
        for o_ref, r in zip(refs[nh + nt + np_:], res):
            o_ref[...] = r.astype(o_ref.dtype)

    in_specs = [pl.BlockSpec((halo, tiled[j][1]), lambda i, c=tiled[j][2]: (jnp.maximum(i * per - 1, 0), c))
                for j in hal]
    in_specs += [pl.BlockSpec((tile, w), lambda i, c=c: (i, c)) for _, w, c, _ in tiled]
    in_specs += [_full_spec(p) for p in params]
    out_specs = [pl.BlockSpec((tile, w), lambda i: (i, 0)) for w, _ in outs]
    out_shape = [jax.ShapeDtypeStruct((t, w), dt) for w, dt in outs]
    args = [tiled[j][0] for j in hal] + [x[0] for x in tiled] + list(params)
    return pl.pallas_call(body, grid=(t // tile,), in_specs=in_specs, out_specs=out_specs,
                          out_shape=out_shape, name=name)(*args)


def halo_bwd(name, fn, tiled, params, cots, gdtype, tile, halo):
    t = tiled[0][0].shape[0]
    tile = min(tile, t)
    hal = [j for j, x in enumerate(tiled) if x[3]]
    nt, nh, np_, nc = len(tiled), len(hal), len(params), len(cots)
    per = tile // halo
    n_tiles = t // tile

    def body(*refs):
        s = pl.program_id(0)
        i = n_tiles - 1 - s
        first = (i > 0).astype(f32)
        tails = [r[...].astype(f32) * first for r in refs[:nh]]
        vals = [r[...].astype(f32) for r in refs[nh:nh + nt + np_]]
        cvals = tuple(r[...].astype(f32) for r in refs[nh + nt + np_:nh + nt + np_ + nc])
        n_in = nh + nt + np_ + nc
        orefs = refs[n_in:n_in + nt]
        prefs = refs[n_in + nt:n_in + nt + np_]
        carries = refs[n_in + nt + np_:]

        @pl.when(s == 0)
        def _():
            for r in prefs:
                r[...] = jnp.zeros_like(r)
            for r in carries:
                r[...] = jnp.zeros_like(r)

        _, vjp = jax.vjp(fn, *tails, *vals)
        grads = vjp(cvals)
        for j in range(nt):
            g = grads[nh + j]
            if j in hal:
                cr = carries[hal.index(j)]
                g = jnp.concatenate([g[:tile - halo], g[tile - halo:] + cr[...]], axis=0)
            orefs[j][...] = g.astype(orefs[j].dtype)
        for n in range(nh):
            carries[n][...] = grads[n] * first
        for r, g in zip(prefs, grads[nh + nt:]):
            r[...] += g

    rev = lambda s: n_tiles - 1 - s
    in_specs = [pl.BlockSpec((halo, tiled[j][1]),
                             lambda s, c=tiled[j][2]: (jnp.maximum(rev(s) * per - 1, 0), c)) for j in hal]
    in_specs += [pl.BlockSpec((tile, w), lambda s, c=c: (rev(s), c)) for _, w, c, _ in tiled]
    in_specs += [_full_spec(p) for p in params]
    in_specs += [pl.BlockSpec((tile, w), lambda s, c=c: (rev(s), c)) for _, w, c in cots]
    out_specs = [pl.BlockSpec((tile, w), lambda s: (rev(s), 0)) for _, w, _, _ in tiled]
    out_shape = [jax.ShapeDtypeStruct((t, w), gdtype) for _, w, _, _ in tiled]
    out_specs += [_full_spec(p) for p in params]
    out_shape += [jax.ShapeDtypeStruct(p.shape, f32) for p in params]
    scratch = [pltpu.VMEM((halo, tiled[j][1]), f32) for j in hal]
    args = [tiled[j][0] for j in hal] + [x[0] for x in tiled] + list(params) + [a for a, _, _ in cots]
    res = pl.pallas_call(body, grid=(n_tiles,), in_specs=in_specs, out_specs=out_specs,
                         out_shape=out_shape, scratch_shapes=scratch, name=name)(*args)
    return res[:nt], res[nt:]


def _call_with_carry(name, body, grid, in_specs, out_specs, out_shape, scratch, args, carry):
    if carry is None:
        res = pl.pallas_call(body, grid=grid, in_specs=in_specs, out_specs=out_specs, out_shape=out_shape,
                             scratch_shapes=scratch, name=name)(*args)
        return list(res), []
    kind, arrays = carry
    nc, n_in, n_out, n_scr = len(arrays), len(in_specs), len(out_shape), len(scratch)

    def carried(*refs):
        cut = [n_in, nc, n_out, nc, n_scr]
        parts, pos = [], 0
        for k in cut:
            parts.append(refs[pos:pos + k])
            pos += k
        ins, cins, outs, couts, scr = parts
        start, finish = _carry_parts(kind, cins, couts, refs[pos:])
        ids = [pl.program_id(d) for d in range(len(grid))]
        first, last = ids[0] == 0, ids[0] == grid[0] - 1
        for d in range(1, len(grid)):
            first, last = first & (ids[d] == 0), last & (ids[d] == grid[d] - 1)
        pl.when(first)(start)
        body(*ins, *outs, *scr)
        pl.when(last)(finish)

    anyspec = pl.BlockSpec(memory_space=pl.ANY)
    res = pl.pallas_call(
        carried, grid=grid, in_specs=list(in_specs) + [anyspec] * nc, out_specs=list(out_specs) + [anyspec] * nc,
        out_shape=list(out_shape) + _carry_out_shape(kind, arrays),
        scratch_shapes=list(scratch) + _carry_sems(kind, nc), name=name)(*args, *arrays)
    return list(res[:n_out]), list(res[n_out:])


def scan_fwd(name, fn, tiled, pparams, sparams, out_dtype, hb, n_extra=0, carry=None):
    t = tiled[0][0].shape[0]
    n = t // CHUNK
    cpb = min(SCAN_CHUNKS, n)
    nt, npp, nsp = len(tiled), len(pparams), len(sparams)
    w = HD * hb

    def body(*refs):
        c = pl.program_id(1)
        n_in = nt + npp + nsp
        o_ref, sv_ref = refs[n_in], refs[n_in + 1]
        ex_refs, st = refs[n_in + 2:n_in + 2 + n_extra], refs[n_in + 2 + n_extra]

        @pl.when(c == 0)
        def _():
            st[...] = jnp.zeros_like(st)

        vals = [r[...].astype(f32) for r in refs[:n_in]]
        state = [st[hh] for hh in range(hb)]
        rows_out = []
        for j in range(cpb):
            rows = slice(j * CHUNK, (j + 1) * CHUNK)
            outs = []
            for hh in range(hb):
                sl = slice(hh * HD, (hh + 1) * HD)
                hv = [v[rows, sl] for v in vals[:nt]] + [v[:, sl] for v in vals[nt:nt + npp]] + vals[nt + npp:]
                sv_ref[hh, j] = state[hh]
                res = fn(*hv, state[hh])
                state[hh] = res[1]
                outs.append(res[0])
                for e_ref, e in zip(ex_refs, res[2:]):
                    e_ref[hh, j] = e
            rows_out.append(outs[0] if hb == 1 else jnp.concatenate(outs, axis=1))
        for hh in range(hb):
            st[hh] = state[hh]
        o_ref[...] = (rows_out[0] if cpb == 1 else jnp.concatenate(rows_out, axis=0)).astype(o_ref.dtype)

    in_specs = [pl.BlockSpec((CHUNK * cpb, w), lambda g, c, b=b: (c, b // hb + g)) for _, b in tiled]
    in_specs += [pl.BlockSpec((1, w), lambda g, c: (0, g)) for _ in pparams]
    in_specs += [_full_spec(p) for p in sparams]
    out_specs = [pl.BlockSpec((CHUNK * cpb, w), lambda g, c: (c, g)),
                 pl.BlockSpec((hb, cpb, HD, HD), lambda g, c: (g, c, 0, 0))]
    out_shape = [jax.ShapeDtypeStruct((t, GW), out_dtype), jax.ShapeDtypeStruct((NH, n, HD, HD), f32)]
    out_specs += [pl.BlockSpec((hb, cpb, CHUNK, CHUNK), lambda g, c: (g, c, 0, 0))] * n_extra
    out_shape += [jax.ShapeDtypeStruct((NH, n, CHUNK, CHUNK), f32)] * n_extra
    res, carried = _call_with_carry(name, body, (NH // hb, n // cpb), in_specs, out_specs, out_shape,
                                    [pltpu.VMEM((hb, HD, HD), f32)], [a for a, _ in tiled] + list(pparams) + list(sparams),
                                    carry)
    return res + [carried]


def scan_bwd(name, fn, tiled, pparams, sparams, states, cot, gdtypes, hb, extras=(), carry=None):
    t = tiled[0][0].shape[0]
    n = t // CHUNK
    cpb = min(SCAN_CHUNKS, n)
    nb = n // cpb
    nt, npp, nsp, nex = len(tiled), len(pparams), len(sparams), len(extras)
    w = HD * hb

    def body(*refs):
        g, s = pl.program_id(0), pl.program_id(1)
        n_in = nt + npp + nsp
        vals = [r[...].astype(f32) for r in refs[:n_in]]
        st_ref = refs[n_in]
        do = refs[n_in + 1][...].astype(f32)
        ex_refs = refs[n_in + 2:n_in + 2 + nex]
        n_op = n_in + 2 + nex
        orefs = refs[n_op:n_op + nt]
        pprefs = refs[n_op + nt:n_op + nt + npp]
        sprefs = refs[n_op + nt + npp:n_op + nt + npp + nsp]
        ds = refs[n_op + nt + npp + nsp]

        @pl.when(s == 0)
        def _():
            ds[...] = jnp.zeros_like(ds)
            for r in pprefs:
                r[...] = jnp.zeros_like(r)

        @pl.when((s == 0) & (g == 0))
        def _():
            for r in sprefs:
                r[...] = jnp.zeros_like(r)

        dstate = [ds[hh] for hh in range(hb)]
        by_chunk = [None] * cpb
        for j in reversed(range(cpb)):
            rows = slice(j * CHUNK, (j + 1) * CHUNK)
            per_head = []
            for hh in range(hb):
                sl = slice(hh * HD, (hh + 1) * HD)
                hv = [v[rows, sl] for v in vals[:nt]] + [v[:, sl] for v in vals[nt:nt + npp]] + vals[nt + npp:]
                ex = [r[hh, j] for r in ex_refs]
                _, vjp = jax.vjp(lambda *a: fn(*a, *ex), *hv, st_ref[hh, j])
                grads = vjp((do[rows, sl], dstate[hh]))
                dstate[hh] = grads[n_in]
                per_head.append(grads)
            by_chunk[j] = per_head
        for hh in range(hb):
            ds[hh] = dstate[hh]

        def lanes(j, k):
            return by_chunk[j][0][k] if hb == 1 else jnp.concatenate([gr[k] for gr in by_chunk[j]], axis=1)

        for k in range(nt):
            blk = lanes(0, k) if cpb == 1 else jnp.concatenate([lanes(j, k) for j in range(cpb)], axis=0)
            orefs[k][...] = blk.astype(orefs[k].dtype)
        for k, r in enumerate(pprefs):
            tot = lanes(0, nt + k)
            for j in range(1, cpb):
                tot = tot + lanes(j, nt + k)
            r[...] += tot
        for k, r in enumerate(sprefs):
            tot = None
            for j in range(cpb):
                for gr in by_chunk[j]:
                    tot = gr[nt + npp + k] if tot is None else tot + gr[nt + npp + k]
            r[...] += tot

    rev = lambda s: nb - 1 - s
    in_specs = [pl.BlockSpec((CHUNK * cpb, w), lambda g, s, b=b: (rev(s), b // hb + g)) for _, b in tiled]
    in_specs += [pl.BlockSpec((1, w), lambda g, s: (0, g)) for _ in pparams]
    in_specs += [_full_spec(p) for p in sparams]
    in_specs += [pl.BlockSpec((hb, cpb, HD, HD), lambda g, s: (g, rev(s), 0, 0)),
                 pl.BlockSpec((CHUNK * cpb, w), lambda g, s, b=cot[1]: (rev(s), b // hb + g))]
    in_specs += [pl.BlockSpec((hb, cpb, CHUNK, CHUNK), lambda g, s: (g, rev(s), 0, 0)) for _ in extras]
    out_specs = [pl.BlockSpec((CHUNK * cpb, w), lambda g, s: (rev(s), g)) for _ in tiled]
    out_shape = [jax.ShapeDtypeStruct((t, GW), dt) for dt in gdtypes]
    out_specs += [pl.BlockSpec((1, w), lambda g, s: (0, g)) for _ in pparams]
    out_shape += [jax.ShapeDtypeStruct(p.shape, f32) for p in pparams]
    out_specs += [_full_spec(p) for p in sparams]
    out_shape += [jax.ShapeDtypeStruct(p.shape, f32) for p in sparams]
    res, carried = _call_with_carry(
        name, body, (NH // hb, nb), in_specs, out_specs, out_shape, [pltpu.VMEM((hb, HD, HD), f32)],
        [a for a, _ in tiled] + list(pparams) + list(sparams) + [states, cot[0]] + list(extras), carry)
    return res[:nt], res[nt:nt + npp], res[nt + npp:], carried


def matmul(name, a, b, mode, out_dtype=f32, tm=1024, tn=1024, tk=2048, epilogue=None, extra=None, slots=None,
           b_gathered=None, carry=None):
    if b_gathered is not None:
        cut, layer = b_gathered
        _, _, sr, sc = b.shape
        b_rows, b_cols = (N_DEV * sr, sc) if cut == 'rows' else (sr, N_DEV * sc)
    else:
        b_rows, b_cols = b.shape
    if mode == 'nn':
        (m, k), n = a.shape, b_cols
    elif mode == 'nt':
        (m, k), n = a.shape, b_rows
    else:
        (k, m), n = a.shape, b_cols
    tm, tn, tk = min(tm, m), min(tn, n), min(tk, k)
    if b_gathered is not None:
        if (mode == 'nn') == (cut == 'cols'):
            tn = min(tn, sc if cut == 'cols' else sr)
        else:
            tk = min(tk, sr if cut == 'rows' else sc)
    if slots == 'rows':
        tm = min(tm, m // N_DEV)
    if slots == 'cols':
        tn = min(tn, n // N_DEV)
    nk = k // tk
    ca, cb = {'nn': (1, 0), 'nt': (1, 1), 'tn': (0, 0)}[mode]

    def finish(refs, r):
        if epilogue == 'relu2':
            refs[2][...] = r
            refs[3][...] = jnp.square(jnp.maximum(r, 0.0)).astype(bf16)
        elif epilogue == 'relu2_bwd':
            refs[3][...] = (r * 2.0 * jnp.maximum(refs[2][...], 0.0)).astype(refs[3].dtype)
        else:
            refs[2][...] = r.astype(refs[2].dtype)

    def body(*refs):
        part = _dotb(refs[0][...], refs[1][...], ca, cb)
        if nk == 1:
            finish(refs, part)
            return
        acc = refs[-1]
        kk = pl.program_id(2)

        @pl.when(kk == 0)
        def _():
            acc[...] = part

        @pl.when(kk > 0)
        def _():
            acc[...] += part

        @pl.when(kk == nk - 1)
        def _():
            finish(refs, acc[...])

    if mode == 'nn':
        a_spec = pl.BlockSpec((tm, tk), lambda i, j, kk: (i, kk))
        b_spec = pl.BlockSpec((tk, tn), lambda i, j, kk: (kk, j))
    elif mode == 'nt':
        a_spec = pl.BlockSpec((tm, tk), lambda i, j, kk: (i, kk))
        b_spec = pl.BlockSpec((tn, tk), lambda i, j, kk: (j, kk))
    else:
        a_spec = pl.BlockSpec((tk, tm), lambda i, j, kk: (kk, i))
        b_spec = pl.BlockSpec((tk, tn), lambda i, j, kk: (kk, j))
    if b_gathered is not None:
        bshape = (None, None, tk, tn) if mode == 'nn' else (None, None, tn, tk)
        if mode == 'nn' and cut == 'cols':
            per = sc // tn
            b_spec = pl.BlockSpec(bshape, lambda i, j, kk: (j // per, layer, kk, j % per))
        elif mode == 'nn':
            per = sr // tk
            b_spec = pl.BlockSpec(bshape, lambda i, j, kk: (kk // per, layer, kk % per, j))
        elif cut == 'cols':
            per = sc // tk
            b_spec = pl.BlockSpec(bshape, lambda i, j, kk: (kk // per, layer, j, kk % per))
        else:
            per = sr // tn
            b_spec = pl.BlockSpec(bshape, lambda i, j, kk: (j // per, layer, j % per, kk))
    o_spec = pl.BlockSpec((tm, tn), lambda i, j, kk: (i, j))
    in_specs, args = [a_spec, b_spec], [a, b]
    if epilogue == 'relu2':
        out_specs = [o_spec, o_spec]
        out_shape = [jax.ShapeDtypeStruct((m, n), f32), jax.ShapeDtypeStruct((m, n), bf16)]
    elif slots == 'rows':
        per = (m // N_DEV) // tm
        out_specs = pl.BlockSpec((None, None, tm, tn), lambda i, j, kk: ((i // per) % 2, (i // per) // 2, i % per, j))
        out_shape = jax.ShapeDtypeStruct((2, N_DEV // 2, m // N_DEV, n), out_dtype)
    elif slots == 'cols':
        per = (n // N_DEV) // tn
        out_specs = pl.BlockSpec((None, None, tm, tn), lambda i, j, kk: ((j // per) % 2, (j // per) // 2, i, j % per))
        out_shape = jax.ShapeDtypeStruct((2, N_DEV // 2, m, n // N_DEV), out_dtype)
    else:
        out_specs, out_shape = o_spec, jax.ShapeDtypeStruct((m, n), out_dtype)
        if epilogue == 'relu2_bwd':
            in_specs.append(o_spec)
            args.append(extra)
    scratch = [pltpu.VMEM((tm, tn), f32)] if nk > 1 else []
    if carry is None:
        return pl.pallas_call(body, grid=(m // tm, n // tn, nk), in_specs=in_specs, out_specs=out_specs,
                              out_shape=out_shape, scratch_shapes=scratch, name=name)(*args)
    single = not isinstance(out_shape, list)
    res, carried = _call_with_carry(name, body, (m // tm, n // tn, nk), in_specs, [out_specs] if single else out_specs,
                                    [out_shape] if single else out_shape, scratch, args, carry)
    return (res[0] if single else res), carried


def final_loss(name, x, y, g, target):
    t, d = x.shape
    tile = min(ROW_TILE, t)

    def body(x_ref, y_ref, g_ref, t_ref, dx_ref, l_ref):
        i = pl.program_id(0)

        @pl.when(i == 0)
        def _():
            l_ref[...] = jnp.zeros_like(l_ref)

        err = x_ref[...] + _rms(y_ref[...], g_ref[...]) - t_ref[...]
        dx_ref[...] = err * (1.0 / d)
        l_ref[...] += 0.5 * jnp.sum(jnp.mean(err * err, axis=-1, keepdims=True), axis=0, keepdims=True)

    row = pl.BlockSpec((tile, d), lambda i: (i, 0))
    return pl.pallas_call(
        body, grid=(t // tile,), in_specs=[row, row, _full_spec(g), row],
        out_specs=[row, pl.BlockSpec((1, 1), lambda i: (0, 0))],
        out_shape=[jax.ShapeDtypeStruct((t, d), f32), jax.ShapeDtypeStruct((1, 1), f32)], name=name)(x, y, g, target)


def adamw(name, w, m, v, gslots, tr=128):
    nl, r, c = w.shape
    tr = min(tr, r)
    nr = r // tr
    ns = gslots[0].shape[0]
    c1 = 1.0 / (1.0 - ADAM_B1 ** ADAM_STEP)
    c2 = 1.0 / (1.0 - ADAM_B2 ** ADAM_STEP)

    def body(*refs):
        w_ref, m_ref, v_ref = refs[:3]
        g_refs = refs[3:3 + nl]
        go_ref, d_ref, mo_ref, vo_ref = refs[3 + nl:]
        l = pl.program_id(0)
        g = None
        for li in range(nl):
            s = g_refs[li][0].astype(f32)
            for k in range(1, ns):
                s = s + g_refs[li][k].astype(f32)
            g = s if g is None else jnp.where(l == li, s, g)
        mn = ADAM_B1 * m_ref[...] + (1.0 - ADAM_B1) * g
        vn = ADAM_B2 * v_ref[...] + (1.0 - ADAM_B2) * jnp.square(g)
        go_ref[...] = g
        mo_ref[...] = mn
        vo_ref[...] = vn
        d_ref[...] = -ADAM_LR * ((mn * c1) / (jnp.sqrt(vn * c2) + ADAM_EPS) + ADAM_WD * w_ref[...])

    blk = pl.BlockSpec((None, tr, c), lambda l, i: (l, i, 0))

    def gspec(li):
        return pl.BlockSpec((ns, tr, c), lambda l, i: (0, jnp.where(l == li, i, jnp.where(l < li, 0, nr - 1)), 0))

    return pl.pallas_call(
        body, grid=(nl, nr), in_specs=[blk, blk, blk] + [gspec(li) for li in range(nl)],
        out_specs=[blk] * 4, out_shape=[jax.ShapeDtypeStruct(w.shape, f32)] * 4, name=name)(w, m, v, *gslots)


def exchange(name, arrays, gather, group):
    n = len(arrays)
    ns = {'all': 8, 'chips': 4, 'core': 2}[group]

    def body(*refs):
        ins, outs = refs[:n], refs[n:2 * n]
        send_sems, recv_sems, loc_sems = refs[2 * n:]
        x, y, c = lax.axis_index("x"), lax.axis_index("y"), lax.axis_index("c")

        def member(k):
            if group == 'all':
                px, py, pc = x ^ ((k >> 2) & 1), y ^ ((k >> 1) & 1), c ^ (k & 1)
                return (px, py, pc), 4 * px + 2 * py + pc
            if group == 'chips':
                px, py = x ^ ((k >> 1) & 1), y ^ (k & 1)
                return (px, py, c), 2 * px + py
            return (x, y, c ^ k), c ^ k

        _, me = member(0)
        sends, recvs, locs = [], [], []
        for a in range(n):
            lc = pltpu.make_async_copy(ins[a] if gather else ins[a].at[me], outs[a].at[me], loc_sems.at[a])
            lc.start()
            locs.append(lc)
            for k in range(1, ns):
                dev, peer = member(k)
                src = ins[a] if gather else ins[a].at[peer]
                cp = pltpu.make_async_remote_copy(src_ref=src, dst_ref=outs[a].at[me], send_sem=send_sems.at[a, k],
                                                  recv_sem=recv_sems.at[a, k], device_id=dev, device_id_type=MESH)
                cp.start()
                sends.append(cp)
                recvs.append(pltpu.make_async_remote_copy(src_ref=src, dst_ref=outs[a].at[peer], send_sem=send_sems.at[a, k],
                                                          recv_sem=recv_sems.at[a, k], device_id=dev, device_id_type=MESH))
        for cp in recvs:
            cp.wait_recv()
        for cp in sends:
            cp.wait_send()
        for lc in locs:
            lc.wait()

    anyspec = pl.BlockSpec(memory_space=pl.ANY)
    out_shape = [jax.ShapeDtypeStruct(((ns,) + a.shape) if gather else a.shape, a.dtype) for a in arrays]
    return pl.pallas_call(
        body, in_specs=[anyspec] * n, out_specs=[anyspec] * n, out_shape=out_shape,
        scratch_shapes=[pltpu.SemaphoreType.DMA((n, ns)), pltpu.SemaphoreType.DMA((n, ns)),
                        pltpu.SemaphoreType.DMA((n,))], name=name)(*arrays)


def gather_two_level(name, arrays):
    n = len(arrays)

    def body(*refs):
        start, finish = _gather_parts(refs[:n], refs[n:2 * n], *refs[2 * n:])
        start()
        finish()

    anyspec = pl.BlockSpec(memory_space=pl.ANY)
    return pl.pallas_call(
        body, in_specs=[anyspec] * n, out_specs=[anyspec] * n, out_shape=_carry_out_shape('gather', arrays),
        scratch_shapes=_carry_sems('gather', n), name=name)(*arrays)


def _gather_parts(ins, outs, send_sems, recv_sems, loc_sems):
    n = len(ins)
    x, y, c = lax.axis_index("x"), lax.axis_index("y"), lax.axis_index("c")
    sib = (x, y, 1 - c)

    def chip(k):
        px, py = x ^ ((k >> 1) & 1), y ^ (k & 1)
        return (px, py), 2 * px + py

    _, mine = chip(0)

    def copy(a, sem, src, slot, to):
        return pltpu.make_async_remote_copy(src_ref=src, dst_ref=outs[a].at[slot], send_sem=send_sems.at[a, sem],
                                            recv_sem=recv_sems.at[a, sem], device_id=to, device_id_type=MESH)

    def local(a):
        return pltpu.make_async_copy(ins[a], outs[a].at[2 * mine + c], loc_sems.at[a])

    def own_sends(a):
        cps = [copy(a, 0, ins[a], 2 * mine + c, sib)]
        for k in range(1, 4):
            (px, py), _ = chip(k)
            cps.append(copy(a, k, ins[a], 2 * mine + c, (px, py, c)))
        return cps

    def start():
        for a in range(n):
            local(a).start()
            for cp in own_sends(a):
                cp.start()

    def finish():
        passed = []
        for a in range(n):
            for k in range(1, 4):
                _, other = chip(k)
                slot = 2 * other + c
                copy(a, k, outs[a].at[slot], slot, sib).wait_recv()
                fw = copy(a, 3 + k, outs[a].at[slot], slot, sib)
                fw.start()
                passed.append(fw)
        for a in range(n):
            copy(a, 0, ins[a], 2 * mine + 1 - c, sib).wait_recv()
            for k in range(1, 4):
                _, other = chip(k)
                slot = 2 * other + 1 - c
                copy(a, 3 + k, outs[a].at[slot], slot, sib).wait_recv()
        for a in range(n):
            for cp in own_sends(a):
                cp.wait_send()
        for cp in passed:
            cp.wait_send()
        for a in range(n):
            local(a).wait()

    return start, finish


def _scatter_parts(ins, outs, send_sems, recv_sems, loc_sems):
    n = len(ins)
    x, y, c = lax.axis_index("x"), lax.axis_index("y"), lax.axis_index("c")
    mine = 2 * x + y

    def local(a):
        return pltpu.make_async_copy(ins[a].at[mine], outs[a].at[mine], loc_sems.at[a])

    def remote(a, k, slot):
        px, py = x ^ ((k >> 1) & 1), y ^ (k & 1)
        return pltpu.make_async_remote_copy(src_ref=ins[a].at[2 * px + py], dst_ref=outs[a].at[slot],
                                            send_sem=send_sems.at[a, k], recv_sem=recv_sems.at[a, k],
                                            device_id=(px, py, c), device_id_type=MESH)

    def start():
        for a in range(n):
            local(a).start()
            for k in range(1, 4):
                remote(a, k, mine).start()

    def finish():
        for a in range(n):
            for k in range(1, 4):
                remote(a, k, 2 * (x ^ ((k >> 1) & 1)) + (y ^ (k & 1))).wait_recv()
        for a in range(n):
            for k in range(1, 4):
                remote(a, k, mine).wait_send()
            local(a).wait()

    return start, finish


def _carry_out_shape(kind, arrays):
    if kind == 'gather':
        return [jax.ShapeDtypeStruct((N_DEV,) + a.shape, a.dtype) for a in arrays]
    return [jax.ShapeDtypeStruct(a.shape, a.dtype) for a in arrays]


def _carry_sems(kind, n):
    k = 7 if kind == 'gather' else 4
    return [pltpu.SemaphoreType.DMA((n, k)), pltpu.SemaphoreType.DMA((n, k)), pltpu.SemaphoreType.DMA((n,))]


def _carry_parts(kind, ins, outs, sems):
    return (_gather_parts if kind == 'gather' else _scatter_parts)(ins, outs, *sems)


def chip_partials(tag, slots):
    from_sibling = send_to_sibling(f"to_sibling_{tag}", slots)
    return [chip_sum(f"chip_sum_{tag}{i}", b, o) for i, (b, o) in enumerate(zip(slots, from_sibling))]


def send_to_sibling(name, arrays):
    n = len(arrays)

    def body(*refs):
        ins, outs = refs[:n], refs[n:2 * n]
        send_sems, recv_sems = refs[2 * n:]
        x, y, c = lax.axis_index("x"), lax.axis_index("y"), lax.axis_index("c")
        cps = [pltpu.make_async_remote_copy(src_ref=ins[a].at[1 - c], dst_ref=outs[a], send_sem=send_sems.at[a],
                                            recv_sem=recv_sems.at[a], device_id=(x, y, 1 - c), device_id_type=MESH)
               for a in range(n)]
        for cp in cps:
            cp.start()
        for cp in cps:
            cp.wait()

    anyspec = pl.BlockSpec(memory_space=pl.ANY)
    out_shape = [jax.ShapeDtypeStruct(a.shape[1:], a.dtype) for a in arrays]
    return pl.pallas_call(
        body, in_specs=[anyspec] * n, out_specs=[anyspec] * n, out_shape=out_shape,
        scratch_shapes=[pltpu.SemaphoreType.DMA((n,)), pltpu.SemaphoreType.DMA((n,))], name=name)(*arrays)


def chip_sum(name, both, other):
    _, nc, r, c = both.shape
    tr = min(ROW_TILE, r)

    def body(b_ref, o_ref, s_ref):
        core = lax.axis_index("c")
        own = jnp.where(core == 0, b_ref[0], b_ref[1]).astype(f32)
        s_ref[...] = (own + o_ref[...].astype(f32)).astype(s_ref.dtype)

    return pl.pallas_call(
        body, grid=(nc, r // tr),
        in_specs=[pl.BlockSpec((2, None, tr, c), lambda s, i: (0, s, i, 0)),
                  pl.BlockSpec((None, tr, c), lambda s, i: (s, i, 0))],
        out_specs=pl.BlockSpec((None, tr, c), lambda s, i: (s, i, 0)),
        out_shape=jax.ShapeDtypeStruct((nc, r, c), both.dtype), name=name)(both, other)


REPLICATED = ('lower_bounds', 'norm_mix_pre', 'norm_mix_post', 'norm_ff_pre', 'norm_ff_post', 'hgrn_norm_w',
              'gdn_a_log', 'gdn_dt_bias', 'gdn_norm_w', 'gmlp_ln_w', 'gmlp_ln_b', 'gmlp_w_s', 'gmlp_b_s',
              'conv_dw_b', 'conv_ln_w', 'conv_ln_b')
WEIGHTS = ('lower_bounds', 'norm_mix_pre', 'norm_mix_post', 'norm_ff_pre', 'norm_ff_post', 'w_in', 'w_out',
           'hgrn_norm_w', 'gdn_conv_w', 'gdn_a_log', 'gdn_dt_bias', 'gdn_norm_w', 'gmlp_ln_w', 'gmlp_ln_b',
           'gmlp_w_s', 'gmlp_b_s', 'conv_dw_w', 'conv_dw_b', 'conv_ln_w', 'conv_ln_b', 'w_ff1', 'w_ff2')


def _row(v):
    return v.reshape(1, -1)


def _pad_lanes(v, offset):
    return jnp.pad(v, (offset, 128 - offset - v.shape[0])).reshape(1, 128)


def _relayout_w_in(g):
    full = jnp.moveaxis(g[:, 0], 0, 1).reshape(D_MODEL, D_IN)
    return jnp.concatenate([full[:, :8 * GW], full[:, 8 * GW + 2 * NH:], full[:, 8 * GW:8 * GW + 2 * NH],
                            jnp.zeros((D_MODEL, 128 - 2 * NH), bf16)], axis=1)


def _layer_fwd(l, x0, params, lb_all, shards):
    p = params[l]
    sv = {'x0': x0}
    (h,) = rowwise(f"norm_mix_pre{l}", fn_norm, [(x0, D_MODEL, 0)], [p['g_mix_pre']], [(D_MODEL, bf16)], ROW_TILE)
    proj = matmul(f"proj{l}", h, p['w_in'], 'nn', f32, tn=896)
    sv.update(h=h, proj=proj)
    lb = lb_all[l:l + 1]
    o_a, st_a, (p['w_ff1'],) = scan_fwd(f"hgrn{l}", fn_hgrn, [(proj, 0), (proj, 4), (proj, 8), (proj, 12)], [lb],
                                        [p['hgrn_norm_w']], bf16, NH, carry=('gather', [shards['w_ff1', l]]))
    q, k, v, beta, g = halo_fwd(
        f"gdn_pre{l}", fn_gdn_pre,
        [(proj, GW, 4, True), (proj, GW, 5, True), (proj, GW, 6, True), (proj, 128, 48, False)],
        [p['gdn_conv_w'], p['alog'], p['dtb']], [(GW, f32)] * 5, ROW_TILE, 8)
    wanted = [shards['w_ff2', l]] + ([shards['w_out']] if l == 0 else []) + ([shards['w_in', l + 1]] if l + 1 < DEPTH else [])
    o_b, st_b, tinv_b, got = scan_fwd(f"gdn{l}", fn_gdn, [(q, 0), (k, 0), (v, 0), (beta, 0), (g, 0), (proj, 28)], [],
                                      [p['gdn_norm_w']], bf16, NH, n_extra=1, carry=('gather', wanted))
    p['w_ff2'] = got.pop(0)
    if l == 0:
        w_out_full = jnp.moveaxis(got.pop(0), 0, 1).reshape(DEPTH, D_MODEL, D_MODEL)
        for ll in range(DEPTH):
            params[ll]['w_out'] = w_out_full[ll]
    if l + 1 < DEPTH:
        params[l + 1]['w_in'] = _relayout_w_in(got.pop(0))
    (o_c,) = rowwise(f"gmlp{l}", fn_gmlp, [(proj, GW, 8), (proj, GW, 9)],
                     [p['gmlp_ln_w'], p['gmlp_ln_b'], p['gmlp_w_s'], p['gmlp_b_s']], [(GW, bf16)], MIX_CHUNK)
    (o_d,) = halo_fwd(f"conv{l}", fn_conv, [(proj, GW, 10, True), (proj, GW, 11, True)],
                      [p['conv_dw_w'], p['conv_dw_b'], p['conv_ln_w'], p['conv_ln_b']], [(GW, bf16)], ROW_TILE, 32)
    mix = jnp.concatenate([o_a, o_b, o_c, o_d], axis=1)
    y1 = matmul(f"out_proj{l}", mix, p['w_out'], 'nn', f32)
    (x1,) = rowwise(f"res_mix{l}", lambda x, y, gg: (x + _rms(y, gg),), [(x0, D_MODEL, 0), (y1, D_MODEL, 0)],
                    [p['g_mix_post']], [(D_MODEL, f32)], ROW_TILE)
    (h2,) = rowwise(f"norm_ff_pre{l}", fn_norm, [(x1, D_MODEL, 0)], [p['g_ff_pre']], [(D_MODEL, bf16)], ROW_TILE)
    u, a = matmul(f"ff1_{l}", h2, p['w_ff1'], 'nn', epilogue='relu2', b_gathered=('cols', 0))
    y2 = matmul(f"ff2_{l}", a, p['w_ff2'], 'nn', f32, b_gathered=('rows', 0))
    sv.update(st_a=st_a, q=q, k=k, v=v, beta=beta, g=g, st_b=st_b, tinv_b=tinv_b, mix=mix, y1=y1, x1=x1, h2=h2, u=u, a=a, y2=y2)
    return sv


def _slots_w_in(gp):
    gl = jnp.concatenate([gp[:, :8 * GW], gp[:, 12 * GW:12 * GW + 2 * NH], gp[:, 8 * GW:12 * GW]], axis=1)
    return jnp.transpose(gl.reshape(D_MODEL, N_DEV // 2, 2, D_IN // N_DEV), (2, 1, 0, 3))


def _layer_bwd(l, dx, sv, p, lb_all, received):
    gr = {}
    (dy2,), (gr['norm_ff_post'],) = rowwise_bwd(f"res_ff_bwd{l}", fn_norm, [(sv['y2'], D_MODEL, 0)], [p['g_ff_post']],
                                                [(dx, D_MODEL, 0)], [bf16], ROW_TILE)
    du = matmul(f"ff2_dx{l}", dy2, p['w_ff2'], 'nt', bf16, epilogue='relu2_bwd', extra=sv['u'], b_gathered=('rows', 0))
    g_ff2 = matmul(f"ff2_dw{l}", sv['a'], dy2, 'tn', bf16, slots='rows')
    dh2 = matmul(f"ff1_dx{l}", du, p['w_ff1'], 'nt', f32, b_gathered=('cols', 0))
    g_ff1 = matmul(f"ff1_dw{l}", sv['h2'], du, 'tn', bf16, slots='cols')
    (dx1,), (gr['norm_ff_pre'],) = rowwise_bwd(f"norm_ff_pre_bwd{l}", fn_norm, [(sv['x1'], D_MODEL, 0)], [p['g_ff_pre']],
                                               [(dh2, D_MODEL, 0)], [f32], ROW_TILE, addto=(dx, D_MODEL, 0))
    (dy1,), (gr['norm_mix_post'],) = rowwise_bwd(f"res_mix_bwd{l}", fn_norm, [(sv['y1'], D_MODEL, 0)], [p['g_mix_post']],
                                                 [(dx1, D_MODEL, 0)], [bf16], ROW_TILE)
    dmix = matmul(f"out_proj_dx{l}", dy1, p['w_out'], 'nt', f32)
    g_out = matmul(f"out_proj_dw{l}", sv['mix'], dy1, 'tn', bf16, slots='rows')
    part_ff2, part_ff1, part_out = chip_partials(f"l{l}_", [g_ff2, g_ff1, g_out])
    proj = sv['proj']
    lb = lb_all[l:l + 1]
    going = [(('w_ff2', l), part_ff2)]
    d_a, (dlb,), (gr['hgrn_norm_w'],), got = scan_bwd(
        f"hgrn_bwd{l}", fn_hgrn, [(proj, 0), (proj, 4), (proj, 8), (proj, 12)], [lb], [p['hgrn_norm_w']],
        sv['st_a'], (dmix, 0), [bf16] * 4, NH, carry=('scatter', [a for _, a in going]))
    received.update({key: r for (key, _), r in zip(going, got)})
    going = [(('w_ff1', l), part_ff1), (('w_out', l), part_out)]
    d_b, _, (gr['gdn_norm_w'],), got = scan_bwd(
        f"gdn_bwd{l}", fn_gdn, [(sv['q'], 0), (sv['k'], 0), (sv['v'], 0), (sv['beta'], 0), (sv['g'], 0), (proj, 28)],
        [], [p['gdn_norm_w']], sv['st_b'], (dmix, 4), [f32] * 5 + [bf16], NH, extras=[sv['tinv_b']],
        carry=('scatter', [a for _, a in going]))
    received.update({key: r for (key, _), r in zip(going, got)})
    d_bp, (gr['gdn_conv_w'], dalog, ddtb) = halo_bwd(
        f"gdn_pre_bwd{l}", fn_gdn_pre,
        [(proj, GW, 4, True), (proj, GW, 5, True), (proj, GW, 6, True), (proj, 128, 48, False)],
        [p['gdn_conv_w'], p['alog'], p['dtb']], [(d_b[j], GW, 0) for j in range(5)], bf16, ROW_TILE, 8)
    gr['gdn_a_log'] = dalog[0, NH:2 * NH]
    gr['gdn_dt_bias'] = ddtb[0, NH:2 * NH]
    d_c, (gr['gmlp_ln_w'], gr['gmlp_ln_b'], gr['gmlp_w_s'], gr['gmlp_b_s']) = rowwise_bwd(
        f"gmlp_bwd{l}", fn_gmlp, [(proj, GW, 8), (proj, GW, 9)],
        [p['gmlp_ln_w'], p['gmlp_ln_b'], p['gmlp_w_s'], p['gmlp_b_s']], [(dmix, GW, 2)], [bf16, bf16], MIX_CHUNK)
    d_d, (gr['conv_dw_w'], gr['conv_dw_b'], gr['conv_ln_w'], gr['conv_ln_b']) = halo_bwd(
        f"conv_bwd{l}", fn_conv, [(proj, GW, 10, True), (proj, GW, 11, True)],
        [p['conv_dw_w'], p['conv_dw_b'], p['conv_ln_w'], p['conv_ln_b']], [(dmix, GW, 3)], bf16, ROW_TILE, 32)
    dproj = jnp.concatenate(list(d_a) + [d_bp[0], d_bp[1], d_bp[2], d_b[5]] + list(d_c) + list(d_d) + [d_bp[3]], axis=1)
    g_in = matmul(f"proj_dw{l}", sv['h'], dproj, 'tn', bf16, tn=896)
    (part_in,) = chip_partials(f"l{l}_in", [_slots_w_in(g_in)])
    dh, (received['w_in', l],) = matmul(f"proj_dx{l}", dproj, p['w_in'], 'nt', f32, tm=512, tn=512, tk=D_IN_PAD,
                                        carry=('scatter', [part_in]))
    (dx0,), (gr['norm_mix_pre'],) = rowwise_bwd(f"norm_mix_pre_bwd{l}", fn_norm, [(sv['x0'], D_MODEL, 0)], [p['g_mix_pre']],
                                                [(dh, D_MODEL, 0)], [f32], ROW_TILE, addto=(dx1, D_MODEL, 0))
    return dx0, gr, dlb


def kernel(x, lower_bounds, norm_mix_pre, norm_mix_post, norm_ff_pre, norm_ff_post, w_in, w_out, hgrn_norm_w, gdn_conv_w, gdn_a_log, gdn_dt_bias, gdn_norm_w, gmlp_ln_w, gmlp_ln_b, gmlp_w_s, gmlp_b_s, conv_dw_w, conv_dw_b, conv_ln_w, conv_ln_b, w_ff1, w_ff2, loss_target, m_lower_bounds, m_norm_mix_pre, m_norm_mix_post, m_norm_ff_pre, m_norm_ff_post, m_w_in, m_w_out, m_hgrn_norm_w, m_gdn_conv_w, m_gdn_a_log, m_gdn_dt_bias, m_gdn_norm_w, m_gmlp_ln_w, m_gmlp_ln_b, m_gmlp_w_s, m_gmlp_b_s, m_conv_dw_w, m_conv_dw_b, m_conv_ln_w, m_conv_ln_b, m_w_ff1, m_w_ff2, v_lower_bounds, v_norm_mix_pre, v_norm_mix_post, v_norm_ff_pre, v_norm_ff_post, v_w_in, v_w_out, v_hgrn_norm_w, v_gdn_conv_w, v_gdn_a_log, v_gdn_dt_bias, v_gdn_norm_w, v_gmlp_ln_w, v_gmlp_ln_b, v_gmlp_w_s, v_gmlp_b_s, v_conv_dw_w, v_conv_dw_b, v_conv_ln_w, v_conv_ln_b, v_w_ff1, v_w_ff2):
    loc = dict(locals())
    W = {n: loc[n] for n in WEIGHTS}
    M = {n: loc['m_' + n] for n in WEIGHTS}
    V = {n: loc['v_' + n] for n in WEIGHTS}
    t = x.shape[1]
    me = 4 * lax.axis_index("x") + 2 * lax.axis_index("y") + lax.axis_index("c")

    shards = {'w_out': w_out.astype(bf16)}
    for l in range(DEPTH):
        shards['w_in', l] = w_in[l:l + 1].astype(bf16)
        shards['w_ff1', l] = w_ff1[l:l + 1].astype(bf16)
        shards['w_ff2', l] = w_ff2[l:l + 1].astype(bf16)
    g_in0, g_gconv, g_dconv = gather_two_level("gather_weights", [shards['w_in', 0], gdn_conv_w, conv_dw_w])
    gconv_full = jnp.moveaxis(g_gconv, 0, 2).reshape(DEPTH, SHORT_CONV, 3 * GW)
    dconv_full = jnp.moveaxis(g_dconv, 0, 2).reshape(DEPTH, CONV_WIDTH, GW)

    (lb_all,) = rowwise("lower_bounds", fn_lb, [(lower_bounds, GW, 0)], [], [(GW, f32)], DEPTH)

    P = []
    for l in range(DEPTH):
        P.append(dict(
            g_mix_pre=_row(norm_mix_pre[l]), g_mix_post=_row(norm_mix_post[l]), g_ff_pre=_row(norm_ff_pre[l]),
            g_ff_post=_row(norm_ff_post[l]), hgrn_norm_w=_row(hgrn_norm_w[l]), gdn_conv_w=gconv_full[l],
            alog=_pad_lanes(gdn_a_log[l], NH), dtb=_pad_lanes(gdn_dt_bias[l], NH), gdn_norm_w=_row(gdn_norm_w[l]),
            gmlp_ln_w=_row(gmlp_ln_w[l]), gmlp_ln_b=_row(gmlp_ln_b[l]), gmlp_w_s=gmlp_w_s[l].reshape(NH * MIX_CHUNK, MIX_CHUNK),
            gmlp_b_s=gmlp_b_s[l], conv_dw_w=dconv_full[l], conv_dw_b=_row(conv_dw_b[l]), conv_ln_w=_row(conv_ln_w[l]),
            conv_ln_b=_row(conv_ln_b[l])))

    P[0]['w_in'] = _relayout_w_in(g_in0)
    xs = x[0]
    saved = []
    for l in range(DEPTH):
        sv = _layer_fwd(l, xs, P, lb_all, shards)
        saved.append(sv)
        if l < DEPTH - 1:
            (xs,) = rowwise(f"res_ff{l}", lambda a, y, gg: (a + _rms(y, gg),), [(sv['x1'], D_MODEL, 0), (sv['y2'], D_MODEL, 0)],
                            [P[l]['g_ff_post']], [(D_MODEL, f32)], ROW_TILE)
    sv = saved[-1]
    dx, loss_loc = final_loss("final_loss", sv['x1'], sv['y2'], P[-1]['g_ff_post'], loss_target[0])
    loss = lax.psum(loss_loc[0, 0], ("x", "y", "c"))

    G = {}
    dlb_rows = []
    received = {}
    for l in reversed(range(DEPTH)):
        dx, gr, dlb = _layer_bwd(l, dx, saved[l], P[l], lb_all, received)
        G[l] = gr
        dlb_rows.append(dlb)
    dlb_all = jnp.concatenate(dlb_rows[::-1], axis=0)
    (g_lower_bounds,), _ = rowwise_bwd("lower_bounds_bwd", fn_lb, [(lower_bounds, GW, 0)], [], [(dlb_all, GW, 0)],
                                       [f32], DEPTH)
    grad_x = dx[None]

    def stack(name, f=lambda a: a):
        return jnp.stack([f(G[l][name]) for l in range(DEPTH)], axis=0)

    full = {
        'lower_bounds': g_lower_bounds,
        'norm_mix_pre': stack('norm_mix_pre', lambda a: a[0]), 'norm_mix_post': stack('norm_mix_post', lambda a: a[0]),
        'norm_ff_pre': stack('norm_ff_pre', lambda a: a[0]), 'norm_ff_post': stack('norm_ff_post', lambda a: a[0]),
        'hgrn_norm_w': stack('hgrn_norm_w', lambda a: a[0]), 'gdn_a_log': stack('gdn_a_log'), 'gdn_dt_bias': stack('gdn_dt_bias'),
        'gdn_norm_w': stack('gdn_norm_w', lambda a: a[0]), 'gmlp_ln_w': stack('gmlp_ln_w', lambda a: a[0]),
        'gmlp_ln_b': stack('gmlp_ln_b', lambda a: a[0]),
        'gmlp_w_s': stack('gmlp_w_s', lambda a: a.reshape(NH, MIX_CHUNK, MIX_CHUNK)), 'gmlp_b_s': stack('gmlp_b_s'),
        'conv_dw_b': stack('conv_dw_b', lambda a: a[0]), 'conv_ln_w': stack('conv_ln_w', lambda a: a[0]),
        'conv_ln_b': stack('conv_ln_b', lambda a: a[0]),
        'gdn_conv_w': stack('gdn_conv_w'), 'conv_dw_w': stack('conv_dw_w'),
    }

    small_names = list(REPLICATED) + ['gdn_conv_w', 'conv_dw_w']
    flat = jnp.concatenate([full[n].reshape(-1) for n in small_names])
    n_small = flat.shape[0]
    n_pad = -(-n_small // 1024) * 1024
    packed = jnp.pad(flat, (0, n_pad - n_small)).reshape(n_pad // 128, 128)

    (small_slots,) = exchange("gather_small_grads", [packed], True, 'all')

    out = {}
    for name in ('w_in', 'w_out', 'w_ff1', 'w_ff2'):
        out[name] = adamw(f"adamw_{name}", W[name], M[name], V[name], [received[name, l] for l in range(DEPTH)])

    def pack(d, fill):
        parts = [d[n].reshape(-1) for n in REPLICATED]
        parts.append(jnp.full((n_pad - sum(a.shape[0] for a in parts),), fill, f32))
        return jnp.concatenate(parts).reshape(1, n_pad // 128, 128)

    sm = adamw("adamw_small", pack(W, 0.0), pack(M, 0.0), pack(V, 1.0), [small_slots], tr=n_pad // 128)
    off = 0
    for n in REPLICATED:
        sz = W[n].size
        out[n] = tuple(a.reshape(-1)[off:off + sz].reshape(W[n].shape) for a in sm)
        off += sz
    gsum = sm[0].reshape(-1)
    for n, full_shape in (('gdn_conv_w', (DEPTH, SHORT_CONV, 3 * GW)), ('conv_dw_w', (DEPTH, CONV_WIDTH, GW))):
        sz = math.prod(full_shape)
        gfull = gsum[off:off + sz].reshape(full_shape)
        off += sz
        sh = W[n].shape
        gmine = lax.dynamic_slice_in_dim(gfull, me * sh[2], sh[2], axis=2)
        r = adamw(f"adamw_{n}", W[n].reshape(1, sh[0] * sh[1], sh[2]), M[n].reshape(1, sh[0] * sh[1], sh[2]),
                  V[n].reshape(1, sh[0] * sh[1], sh[2]), [gmine.reshape(1, sh[0] * sh[1], sh[2])], tr=sh[0] * sh[1])
        out[n] = tuple(a.reshape(sh) for a in r)

    return (loss, grad_x, *[out[n][0] for n in WEIGHTS], *[out[n][1] for n in WEIGHTS],
            *[out[n][2] for n in WEIGHTS], *[out[n][3] for n in WEIGHTS])
```

```python
import functools
import math

import jax
import jax.numpy as jnp
from jax import lax
from jax.experimental import pallas as pl
from jax.experimental.pallas import tpu as pltpu

f32 = jnp.float32
bf16 = jnp.bfloat16
HI = lax.Precision.HIGHEST

N_DEV = 8
DEPTH = 2
D_MODEL = 2048
GW = 512
HD = 128
NH = 4
CHUNK = 64
MIX_CHUNK = 128
CONV_WIDTH = 31
SHORT_CONV = 4
D_FF = 4 * D_MODEL
D_IN = 12 * GW + 2 * NH
D_IN_PAD = 12 * GW + 128
ROW_TILE = 256
HGRN_SUB = 16
SCAN_CHUNKS = 2
EPS = 1e-6
TINY = 1e-30
ADAM_LR, ADAM_B1, ADAM_B2, ADAM_EPS, ADAM_WD, ADAM_STEP = 0.001, 0.9, 0.999, 1e-08, 0.01, 10
MESH = pl.DeviceIdType.MESH


def _dotb(a, b, ca, cb):
    return lax.dot_general(a.astype(bf16), b.astype(bf16), (((ca,), (cb,)), ((), ())),
                           preferred_element_type=f32)


@jax.custom_vjp
def mm(a, b):
    return _dotb(a, b, 1, 0)


def _mm_f(a, b):
    return mm(a, b), (a, b)


def _mm_b(res, ct):
    a, b = res
    return _dotb(ct, b, 1, 1), _dotb(a, ct, 0, 0)


mm.defvjp(_mm_f, _mm_b)


@jax.custom_vjp
def mm_nt(a, b):
    return _dotb(a, b, 1, 1)


def _mmnt_f(a, b):
    return mm_nt(a, b), (a, b)


def _mmnt_b(res, ct):
    a, b = res
    return _dotb(ct, b, 1, 0), _dotb(ct, a, 0, 0)


mm_nt.defvjp(_mmnt_f, _mmnt_b)


@jax.custom_vjp
def mm_tn(a, b):
    return _dotb(a, b, 0, 0)


def _mmtn_f(a, b):
    return mm_tn(a, b), (a, b)


def _mmtn_b(res, ct):
    a, b = res
    return _dotb(b, ct, 1, 1), _dotb(a, ct, 1, 0)


mm_tn.defvjp(_mmtn_f, _mmtn_b)


def mmh(a, b):
    return jnp.dot(a, b, precision=HI, preferred_element_type=f32)


def mm3(a, b):
    return jnp.dot(a, b, precision=lax.Precision.HIGH, preferred_element_type=f32)


def _rms(x, w):
    return x * lax.rsqrt(jnp.mean(x * x, axis=-1, keepdims=True) + EPS) * w


def _ln(x, w, b):
    mu = jnp.mean(x, axis=-1, keepdims=True)
    xc = x - mu
    var = jnp.mean(xc * xc, axis=-1, keepdims=True)
    return xc * lax.rsqrt(var + EPS) * w + b


def _gelu(x):
    return 0.5 * x * (1.0 + lax.erf(x * (2.0 ** -0.5)))


def _iota2(n, m, axis):
    return lax.broadcasted_iota(jnp.int32, (n, m), axis)


def _tri(n, strict=False):
    r, c = _iota2(n, n, 0), _iota2(n, n, 1)
    return (r > c) if strict else (r >= c)


def _eye(n):
    return (_iota2(n, n, 0) == _iota2(n, n, 1)).astype(f32)


def fn_norm(x, g):
    return (_rms(x, g),)


def fn_lb(lower_bounds):
    s = jax.nn.softmax(lower_bounds, axis=0)
    rows, cum = [], None
    for i in range(DEPTH):
        cum = s[i:i + 1] if cum is None else cum + s[i:i + 1]
        rows.append(cum - s[0:1])
    return (jnp.concatenate(rows, axis=0),)


def fn_hgrn(aq, af, ai, ag, lb, nw, st):
    c = aq.shape[0]
    sig = jax.nn.sigmoid(af)
    f = lb + (1.0 - lb) * sig
    logf = jnp.log(jnp.maximum(f, TINY))
    k = (1.0 - lb) * jax.nn.sigmoid(-af)
    q = jax.nn.silu(aq)
    v = ai
    b = mmh(_tri(c).astype(f32), logf)
    outs = []
    for lo in range(0, c, HGRN_SUB):
        qi, ki, vi, bi = (a[lo:lo + HGRN_SUB] for a in (q, k, v, b))
        rel = bi[:, None, :] - bi[None, :, :]
        dec = jnp.exp(jnp.minimum(rel, 0.0)) * qi[:, None, :] * ki[None, :, :]
        o_blk = mm(jnp.where(_tri(HGRN_SUB), jnp.sum(dec, axis=-1), 0.0), vi)
        if lo > 0:
            r = b[lo - 1:lo, :]
            o_blk = o_blk + mm(mm_nt(qi * jnp.exp(bi - r), k[:lo] * jnp.exp(r - b[:lo])), v[:lo])
        outs.append(o_blk)
    b_end = b[c - 1:c, :]
    out = jnp.concatenate(outs, axis=0) + mm_nt(q * jnp.exp(b), st)
    st_new = st * jnp.exp(b_end) + mm_tn(v, k * jnp.exp(b_end - b))
    o = _rms(out, nw) * jax.nn.silu(ag)
    return o, st_new


def fn_gdn_pre(tq, tk, tv, bq, bk, bv, p8, conv_w, alog, dtb):
    tile = bq.shape[0]
    h = tq.shape[0]
    outs = []
    for seg, (tl, cur) in enumerate(((tq, bq), (tk, bk), (tv, bv))):
        xe = jnp.concatenate([tl, cur], axis=0)
        acc = None
        for kk in range(SHORT_CONV):
            off = h - (SHORT_CONV - 1) + kk
            term = conv_w[kk:kk + 1, seg * GW:(seg + 1) * GW] * xe[off:off + tile, :]
            acc = term if acc is None else acc + term
        outs.append(jax.nn.silu(acc))
    sq, sk, sv = outs
    qh, kh = [], []
    for hh in range(NH):
        a = sq[:, hh * HD:(hh + 1) * HD]
        qh.append(a * lax.rsqrt(jnp.sum(a * a, axis=-1, keepdims=True) + EPS) * (HD ** -0.5))
        a = sk[:, hh * HD:(hh + 1) * HD]
        kh.append(a * lax.rsqrt(jnp.sum(a * a, axis=-1, keepdims=True) + EPS))
    q = jnp.concatenate(qh, axis=1)
    k = jnp.concatenate(kh, axis=1)
    beta = jax.nn.sigmoid(p8)
    g = -jnp.exp(alog) * jax.nn.softplus(p8 + dtb)
    r, cc = _iota2(128, GW, 0), _iota2(128, GW, 1) // HD
    e_beta = (r == cc).astype(f32)
    e_g = (r == cc + NH).astype(f32)
    return q, k, sv, mmh(beta, e_beta), mmh(g, e_g)


@jax.custom_vjp
def _inverse_given(m, tinv):
    return tinv


def _inverse_given_f(m, tinv):
    return tinv, tinv


def _inverse_given_b(tinv, ct):
    x = lax.dot_general(ct, tinv, (((1,), (1,)), ((), ())), precision=HI, preferred_element_type=f32)
    dm = -lax.dot_general(tinv, x, (((0,), (0,)), ((), ())), precision=HI, preferred_element_type=f32)
    return dm, jnp.zeros_like(tinv)


_inverse_given.defvjp(_inverse_given_f, _inverse_given_b)


def fn_gdn(q, k, v, beta, g, z, nw, s, tinv_saved=None):
    c = q.shape[0]
    gc = mmh(_tri(c).astype(f32), g)
    gcol = gc[:, 0:1]
    grow = jnp.sum(gcol * _eye(c), axis=0, keepdims=True)
    gamma = jnp.where(_tri(c), jnp.exp(jnp.minimum(gcol - grow, 0.0)), 0.0)
    kb = k * beta
    m = jnp.where(_tri(c, strict=True), mm_nt(kb, k) * gamma, 0.0)
    if tinv_saved is None:
        eye = _eye(c)
        tinv = eye - m
        p = m
        for _ in range(int(math.log2(c)) - 1):
            p = mm3(p, p)
            tinv = mm3(tinv, eye + p)
    else:
        tinv = _inverse_given(m, tinv_saved)
    egc = jnp.exp(gc)
    u = mmh(tinv, v * beta)
    w = mmh(tinv, kb * egc)
    qk = mm_nt(q, k) * gamma
    gc_end = gc[c - 1:c, :]
    q_dec = q * egc
    k_dec = k * jnp.exp(gc_end - gc)
    v_new = u - mm(w, s)
    out = mm(q_dec, s) + mm(qk, v_new)
    s_new = s * jnp.exp(gc_end) + mm_tn(k_dec, v_new)
    o = _rms(out, nw) * jax.nn.silu(z)
    return (o, s_new, tinv) if tinv_saved is None else (o, s_new)


def fn_gmlp(cu, cv, ln_w, ln_b, w_s, b_s):
    n = cu.shape[0]
    ug = _gelu(cu)
    vn = _ln(_gelu(cv), ln_w, ln_b)
    eye = _eye(n)
    cols = []
    for hh in range(NH):
        wc = jnp.where(_tri(n), w_s[hh * n:(hh + 1) * n, :], 0.0)
        bcol = jnp.sum(b_s[hh:hh + 1, :] * eye, axis=1, keepdims=True)
        cols.append(mm(wc, vn[:, hh * HD:(hh + 1) * HD]) + bcol)
    return (ug * jnp.concatenate(cols, axis=1),)


def fn_conv(ta, tg, a, gate, dw_w, dw_b, ln_w, ln_b):
    tile = a.shape[0]
    h = ta.shape[0]
    ya = jnp.concatenate([ta, a], axis=0)
    yg = jnp.concatenate([tg, gate], axis=0)
    y = ya * jax.nn.sigmoid(yg)
    acc = None
    for kk in range(CONV_WIDTH):
        off = h - (CONV_WIDTH - 1) + kk
        term = dw_w[kk:kk + 1, :] * y[off:off + tile, :]
        acc = term if acc is None else acc + term
    return (jax.nn.silu(_ln(acc + dw_b, ln_w, ln_b)),)


def _full_spec(arr):
    nd = arr.ndim
    return pl.BlockSpec(arr.shape, lambda *_: (0,) * nd)


def rowwise(name, fn, tiled, params, outs, tile):
    t = tiled[0][0].shape[0]
    tile = min(tile, t)
    nt, np_ = len(tiled), len(params)

    def body(*refs):
        vals = [r[...].astype(f32) for r in refs[:nt + np_]]
        res = fn(*vals)
        for o_ref, r in zip(refs[nt + np_:], res):
            o_ref[...] = r.astype(o_ref.dtype)

    in_specs = [pl.BlockSpec((tile, w), lambda i, c=c: (i, c)) for _, w, c in tiled]
    in_specs += [_full_spec(p) for p in params]
    out_specs = [pl.BlockSpec((tile, w), lambda i: (i, 0)) for w, _ in outs]
    out_shape = [jax.ShapeDtypeStruct((t, w), dt) for w, dt in outs]
    return pl.pallas_call(body, grid=(t // tile,), in_specs=in_specs, out_specs=out_specs,
                          out_shape=out_shape, name=name)(*[a for a, _, _ in tiled], *params)


def rowwise_bwd(name, fn, tiled, params, cots, gouts, tile, addto=None):
    t = tiled[0][0].shape[0]
    tile = min(tile, t)
    nt, np_, nc = len(tiled), len(params), len(cots)
    na = 0 if addto is None else 1
    gidx = [i for i, g in enumerate(gouts) if g is not None]

    def body(*refs):
        i = pl.program_id(0)
        vals = [r[...].astype(f32) for r in refs[:nt + np_]]
        cvals = tuple(r[...].astype(f32) for r in refs[nt + np_:nt + np_ + nc])
        _, vjp = jax.vjp(fn, *vals)
        grads = vjp(cvals)
        orefs = refs[nt + np_ + nc + na:]
        for n, j in enumerate(gidx):
            g = grads[j]
            if na and n == 0:
                g = g + refs[nt + np_ + nc][...].astype(f32)
            orefs[n][...] = g.astype(orefs[n].dtype)
        prefs = orefs[len(gidx):]

        @pl.when(i == 0)
        def _():
            for r in prefs:
                r[...] = jnp.zeros_like(r)

        for r, g in zip(prefs, grads[nt:]):
            r[...] += g

    in_specs = [pl.BlockSpec((tile, w), lambda i, c=c: (i, c)) for _, w, c in tiled]
    in_specs += [_full_spec(p) for p in params]
    in_specs += [pl.BlockSpec((tile, w), lambda i, c=c: (i, c)) for _, w, c in cots]
    args = [a for a, _, _ in tiled] + list(params) + [a for a, _, _ in cots]
    if na:
        in_specs.append(pl.BlockSpec((tile, addto[1]), lambda i, c=addto[2]: (i, c)))
        args.append(addto[0])
    out_specs = [pl.BlockSpec((tile, tiled[j][1]), lambda i: (i, 0)) for j in gidx]
    out_shape = [jax.ShapeDtypeStruct((t, tiled[j][1]), gouts[j]) for j in gidx]
    out_specs += [_full_spec(p) for p in params]
    out_shape += [jax.ShapeDtypeStruct(p.shape, f32) for p in params]
    res = pl.pallas_call(body, grid=(t // tile,), in_specs=in_specs, out_specs=out_specs,
                         out_shape=out_shape, name=name)(*args)
    return res[:len(gidx)], res[len(gidx):]


def halo_fwd(name, fn, tiled, params, outs, tile, halo):
    t = tiled[0][0].shape[0]
    tile = min(tile, t)
    hal = [j for j, x in enumerate(tiled) if x[3]]
    nt, nh, np_ = len(tiled), len(hal), len(params)
    per = tile // halo

    def body(*refs):
        i = pl.program_id(0)
        first = (i > 0).astype(f32)
        tails = [r[...].astype(f32) * first for r in refs[:nh]]
        vals = [r[...].astype(f32) for r in refs[nh:nh + nt + np_]]
        res = fn(*tails, *vals)
        for o_ref, r in zip(refs[nh + nt + np_:], res):
            o_ref[...] = r.astype(o_ref.dtype)

    in_specs = [pl.BlockSpec((halo, tiled[j][1]), lambda i, c=tiled[j][2]: (jnp.maximum(i * per - 1, 0), c))
                for j in hal]
    in_specs += [pl.BlockSpec((tile, w), lambda i, c=c: (i, c)) for _, w, c, _ in tiled]
    in_specs += [_full_spec(p) for p in params]
    out_specs = [pl.BlockSpec((tile, w), lambda i: (i, 0)) for w, _ in outs]
    out_shape = [jax.ShapeDtypeStruct((t, w), dt) for w, dt in outs]
    args = [tiled[j][0] for j in hal] + [x[0] for x in tiled] + list(params)
    return pl.pallas_call(body, grid=(t // tile,), in_specs=in_specs, out_specs=out_specs,
                          out_shape=out_shape, name=name)(*args)


def halo_bwd(name, fn, tiled, params, cots, gdtype, tile, halo):
    t = tiled[0][0].shape[0]
    tile = min(tile, t)
    hal = [j for j, x in enumerate(tiled) if x[3]]
    nt, nh, np_, nc = len(tiled), len(hal), len(params), len(cots)
    per = tile // halo
    n_tiles = t // tile

    def body(*refs):
        s = pl.program_id(0)
        i = n_tiles - 1 - s
        first = (i > 0).astype(f32)
        tails = [r[...].astype(f32) * first for r in refs[:nh]]
        vals = [r[...].astype(f32) for r in refs[nh:nh + nt + np_]]
        cvals = tuple(r[...].astype(f32) for r in refs[nh + nt + np_:nh + nt + np_ + nc])
        n_in = nh + nt + np_ + nc
        orefs = refs[n_in:n_in + nt]
        prefs = refs[n_in + nt:n_in + nt + np_]
        carries = refs[n_in + nt + np_:]

        @pl.when(s == 0)
        def _():
            for r in prefs:
                r[...] = jnp.zeros_like(r)
            for r in carries:
                r[...] = jnp.zeros_like(r)

        _, vjp = jax.vjp(fn, *tails, *vals)
        grads = vjp(cvals)
        for j in range(nt):
            g = grads[nh + j]
            if j in hal:
                cr = carries[hal.index(j)]
                g = jnp.concatenate([g[:tile - halo], g[tile - halo:] + cr[...]], axis=0)
            orefs[j][...] = g.astype(orefs[j].dtype)
        for n in range(nh):
            carries[n][...] = grads[n] * first
        for r, g in zip(prefs, grads[nh + nt:]):
            r[...] += g

    rev = lambda s: n_tiles - 1 - s
    in_specs = [pl.BlockSpec((halo, tiled[j][1]),
                             lambda s, c=tiled[j][2]: (jnp.maximum(rev(s) * per - 1, 0), c)) for j in hal]
    in_specs += [pl.BlockSpec((tile, w), lambda s, c=c: (rev(s), c)) for _, w, c, _ in tiled]
    in_specs += [_full_spec(p) for p in params]
    in_specs += [pl.BlockSpec((tile, w), lambda s, c=c: (rev(s), c)) for _, w, c in cots]
    out_specs = [pl.BlockSpec((tile, w), lambda s: (rev(s), 0)) for _, w, _, _ in tiled]
    out_shape = [jax.ShapeDtypeStruct((t, w), gdtype) for _, w, _, _ in tiled]
    out_specs += [_full_spec(p) for p in params]
    out_shape += [jax.ShapeDtypeStruct(p.shape, f32) for p in params]
    scratch = [pltpu.VMEM((halo, tiled[j][1]), f32) for j in hal]
    args = [tiled[j][0] for j in hal] + [x[0] for x in tiled] + list(params) + [a for a, _, _ in cots]
    res = pl.pallas_call(body, grid=(n_tiles,), in_specs=in_specs, out_specs=out_specs,
                         out_shape=out_shape, scratch_shapes=scratch, name=name)(*args)
    return res[:nt], res[nt:]


def _call_with_carry(name, body, grid, in_specs, out_specs, out_shape, scratch, args, carry):
    if carry is None:
        res = pl.pallas_call(body, grid=grid, in_specs=in_specs, out_specs=out_specs, out_shape=out_shape,
                             scratch_shapes=scratch, name=name)(*args)
        return list(res), []
    kind, arrays = carry
    nc, n_in, n_out, n_scr = len(arrays), len(in_specs), len(out_shape), len(scratch)

    def carried(*refs):
        cut = [n_in, nc, n_out, nc, n_scr]
        parts, pos = [], 0
        for k in cut:
            parts.append(refs[pos:pos + k])
            pos += k
        ins, cins, outs, couts, scr = parts
        start, finish = _carry_parts(kind, cins, couts, refs[pos:])
        ids = [pl.program_id(d) for d in range(len(grid))]
        first, last = ids[0] == 0, ids[0] == grid[0] - 1
        for d in range(1, len(grid)):
            first, last = first & (ids[d] == 0), last & (ids[d] == grid[d] - 1)
        pl.when(first)(start)
        body(*ins, *outs, *scr)
        pl.when(last)(finish)

    anyspec = pl.BlockSpec(memory_space=pl.ANY)
    res = pl.pallas_call(
        carried, grid=grid, in_specs=list(in_specs) + [anyspec] * nc, out_specs=list(out_specs) + [anyspec] * nc,
        out_shape=list(out_shape) + _carry_out_shape(kind, arrays),
        scratch_shapes=list(scratch) + _carry_sems(kind, nc), name=name)(*args, *arrays)
    return list(res[:n_out]), list(res[n_out:])


def scan_fwd(name, fn, tiled, pparams, sparams, out_dtype, hb, n_extra=0, carry=None):
    t = tiled[0][0].shape[0]
    n = t // CHUNK
    cpb = min(SCAN_CHUNKS, n)
    nt, npp, nsp = len(tiled), len(pparams), len(sparams)
    w = HD * hb

    def body(*refs):
        c = pl.program_id(1)
        n_in = nt + npp + nsp
        o_ref, sv_ref = refs[n_in], refs[n_in + 1]
        ex_refs, st = refs[n_in + 2:n_in + 2 + n_extra], refs[n_in + 2 + n_extra]

        @pl.when(c == 0)
        def _():
            st[...] = jnp.zeros_like(st)

        vals = [r[...].astype(f32) for r in refs[:n_in]]
        state = [st[hh] for hh in range(hb)]
        rows_out = []
        for j in range(cpb):
            rows = slice(j * CHUNK, (j + 1) * CHUNK)
            outs = []
            for hh in range(hb):
                sl = slice(hh * HD, (hh + 1) * HD)
                hv = [v[rows, sl] for v in vals[:nt]] + [v[:, sl] for v in vals[nt:nt + npp]] + vals[nt + npp:]
                sv_ref[hh, j] = state[hh]
                res = fn(*hv, state[hh])
                state[hh] = res[1]
                outs.append(res[0])
                for e_ref, e in zip(ex_refs, res[2:]):
                    e_ref[hh, j] = e
            rows_out.append(outs[0] if hb == 1 else jnp.concatenate(outs, axis=1))
        for hh in range(hb):
            st[hh] = state[hh]
        o_ref[...] = (rows_out[0] if cpb == 1 else jnp.concatenate(rows_out, axis=0)).astype(o_ref.dtype)

    in_specs = [pl.BlockSpec((CHUNK * cpb, w), lambda g, c, b=b: (c, b // hb + g)) for _, b in tiled]
    in_specs += [pl.BlockSpec((1, w), lambda g, c: (0, g)) for _ in pparams]
    in_specs += [_full_spec(p) for p in sparams]
    out_specs = [pl.BlockSpec((CHUNK * cpb, w), lambda g, c: (c, g)),
                 pl.BlockSpec((hb, cpb, HD, HD), lambda g, c: (g, c, 0, 0))]
    out_shape = [jax.ShapeDtypeStruct((t, GW), out_dtype), jax.ShapeDtypeStruct((NH, n, HD, HD), f32)]
    out_specs += [pl.BlockSpec((hb, cpb, CHUNK, CHUNK), lambda g, c: (g, c, 0, 0))] * n_extra
    out_shape += [jax.ShapeDtypeStruct((NH, n, CHUNK, CHUNK), f32)] * n_extra
    res, carried = _call_with_carry(name, body, (NH // hb, n // cpb), in_specs, out_specs, out_shape,
                                    [pltpu.VMEM((hb, HD, HD), f32)], [a for a, _ in tiled] + list(pparams) + list(sparams),
                                    carry)
    return res + [carried]


def scan_bwd(name, fn, tiled, pparams, sparams, states, cot, gdtypes, hb, extras=(), carry=None):
    t = tiled[0][0].shape[0]
    n = t // CHUNK
    cpb = min(SCAN_CHUNKS, n)
    nb = n // cpb
    nt, npp, nsp, nex = len(tiled), len(pparams), len(sparams), len(extras)
    w = HD * hb

    def body(*refs):
        g, s = pl.program_id(0), pl.program_id(1)
        n_in = nt + npp + nsp
        vals = [r[...].astype(f32) for r in refs[:n_in]]
        st_ref = refs[n_in]
        do = refs[n_in + 1][...].astype(f32)
        ex_refs = refs[n_in + 2:n_in + 2 + nex]
        n_op = n_in + 2 + nex
        orefs = refs[n_op:n_op + nt]
        pprefs = refs[n_op + nt:n_op + nt + npp]
        sprefs = refs[n_op + nt + npp:n_op + nt + npp + nsp]
        ds = refs[n_op + nt + npp + nsp]

        @pl.when(s == 0)
        def _():
            ds[...] = jnp.zeros_like(ds)
            for r in pprefs:
                r[...] = jnp.zeros_like(r)

        @pl.when((s == 0) & (g == 0))
        def _():
            for r in sprefs:
                r[...] = jnp.zeros_like(r)

        dstate = [ds[hh] for hh in range(hb)]
        by_chunk = [None] * cpb
        for j in reversed(range(cpb)):
            rows = slice(j * CHUNK, (j + 1) * CHUNK)
            per_head = []
            for hh in range(hb):
                sl = slice(hh * HD, (hh + 1) * HD)
                hv = [v[rows, sl] for v in vals[:nt]] + [v[:, sl] for v in vals[nt:nt + npp]] + vals[nt + npp:]
                ex = [r[hh, j] for r in ex_refs]
                _, vjp = jax.vjp(lambda *a: fn(*a, *ex), *hv, st_ref[hh, j])
                grads = vjp((do[rows, sl], dstate[hh]))
                dstate[hh] = grads[n_in]
                per_head.append(grads)
            by_chunk[j] = per_head
        for hh in range(hb):
            ds[hh] = dstate[hh]

        def lanes(j, k):
            return by_chunk[j][0][k] if hb == 1 else jnp.concatenate([gr[k] for gr in by_chunk[j]], axis=1)

        for k in range(nt):
            blk = lanes(0, k) if cpb == 1 else jnp.concatenate([lanes(j, k) for j in range(cpb)], axis=0)
            orefs[k][...] = blk.astype(orefs[k].dtype)
        for k, r in enumerate(pprefs):
            tot = lanes(0, nt + k)
            for j in range(1, cpb):
                tot = tot + lanes(j, nt + k)
            r[...] += tot
        for k, r in enumerate(sprefs):
            tot = None
            for j in range(cpb):
                for gr in by_chunk[j]:
                    tot = gr[nt + npp + k] if tot is None else tot + gr[nt + npp + k]
            r[...] += tot

    rev = lambda s: nb - 1 - s
    in_specs = [pl.BlockSpec((CHUNK * cpb, w), lambda g, s, b=b: (rev(s), b // hb + g)) for _, b in tiled]
    in_specs += [pl.BlockSpec((1, w), lambda g, s: (0, g)) for _ in pparams]
    in_specs += [_full_spec(p) for p in sparams]
    in_specs += [pl.BlockSpec((hb, cpb, HD, HD), lambda g, s: (g, rev(s), 0, 0)),
                 pl.BlockSpec((CHUNK * cpb, w), lambda g, s, b=cot[1]: (rev(s), b // hb + g))]
    in_specs += [pl.BlockSpec((hb, cpb, CHUNK, CHUNK), lambda g, s: (g, rev(s), 0, 0)) for _ in extras]
    out_specs = [pl.BlockSpec((CHUNK * cpb, w), lambda g, s: (rev(s), g)) for _ in tiled]
    out_shape = [jax.ShapeDtypeStruct((t, GW), dt) for dt in gdtypes]
    out_specs += [pl.BlockSpec((1, w), lambda g, s: (0, g)) for _ in pparams]
    out_shape += [jax.ShapeDtypeStruct(p.shape, f32) for p in pparams]
    out_specs += [_full_spec(p) for p in sparams]
    out_shape += [jax.ShapeDtypeStruct(p.shape, f32) for p in sparams]
    res, carried = _call_with_carry(
        name, body, (NH // hb, nb), in_specs, out_specs, out_shape, [pltpu.VMEM((hb, HD, HD), f32)],
        [a for a, _ in tiled] + list(pparams) + list(sparams) + [states, cot[0]] + list(extras), carry)
    return res[:nt], res[nt:nt + npp], res[nt + npp:], carried


def matmul(name, a, b, mode, out_dtype=f32, tm=1024, tn=1024, tk=2048, epilogue=None, extra=None, slots=None,
           b_gathered=None, carry=None):
    if b_gathered is not None:
        cut, layer = b_gathered
        _, _, sr, sc = b.shape
        b_rows, b_cols = (N_DEV * sr, sc) if cut == 'rows' else (sr, N_DEV * sc)
    else:
        b_rows, b_cols = b.shape
    if mode == 'nn':
        (m, k), n = a.shape, b_cols
    elif mode == 'nt':
        (m, k), n = a.shape, b_rows
    else:
        (k, m), n = a.shape, b_cols
    tm, tn, tk = min(tm, m), min(tn, n), min(tk, k)
    if b_gathered is not None:
        if (mode == 'nn') == (cut == 'cols'):
            tn = min(tn, sc if cut == 'cols' else sr)
        else:
            tk = min(tk, sr if cut == 'rows' else sc)
    if slots == 'rows':
        tm = min(tm, m // N_DEV)
    if slots == 'cols':
        tn = min(tn, n // N_DEV)
    nk = k // tk
    ca, cb = {'nn': (1, 0), 'nt': (1, 1), 'tn': (0, 0)}[mode]

    def finish(refs, r):
        if epilogue == 'relu2':
            refs[2][...] = r
            refs[3][...] = jnp.square(jnp.maximum(r, 0.0)).astype(bf16)
        elif epilogue == 'relu2_bwd':
            refs[3][...] = (r * 2.0 * jnp.maximum(refs[2][...], 0.0)).astype(refs[3].dtype)
        else:
            refs[2][...] = r.astype(refs[2].dtype)

    def body(*refs):
        part = _dotb(refs[0][...], refs[1][...], ca, cb)
        if nk == 1:
            finish(refs, part)
            return
        acc = refs[-1]
        kk = pl.program_id(2)

        @pl.when(kk == 0)
        def _():
            acc[...] = part

        @pl.when(kk > 0)
        def _():
            acc[...] += part

        @pl.when(kk == nk - 1)
        def _():
            finish(refs, acc[...])

    if mode == 'nn':
        a_spec = pl.BlockSpec((tm, tk), lambda i, j, kk: (i, kk))
        b_spec = pl.BlockSpec((tk, tn), lambda i, j, kk: (kk, j))
    elif mode == 'nt':
        a_spec = pl.BlockSpec((tm, tk), lambda i, j, kk: (i, kk))
        b_spec = pl.BlockSpec((tn, tk), lambda i, j, kk: (j, kk))
    else:
        a_spec = pl.BlockSpec((tk, tm), lambda i, j, kk: (kk, i))
        b_spec = pl.BlockSpec((tk, tn), lambda i, j, kk: (kk, j))
    if b_gathered is not None:
        bshape = (None, None, tk, tn) if mode == 'nn' else (None, None, tn, tk)
        if mode == 'nn' and cut == 'cols':
            per = sc // tn
            b_spec = pl.BlockSpec(bshape, lambda i, j, kk: (j // per, layer, kk, j % per))
        elif mode == 'nn':
            per = sr // tk
            b_spec = pl.BlockSpec(bshape, lambda i, j, kk: (kk // per, layer, kk % per, j))
        elif cut == 'cols':
            per = sc // tk
            b_spec = pl.BlockSpec(bshape, lambda i, j, kk: (kk // per, layer, j, kk % per))
        else:
            per = sr // tn
            b_spec = pl.BlockSpec(bshape, lambda i, j, kk: (j // per, layer, j % per, kk))
    o_spec = pl.BlockSpec((tm, tn), lambda i, j, kk: (i, j))
    in_specs, args = [a_spec, b_spec], [a, b]
    if epilogue == 'relu2':
        out_specs = [o_spec, o_spec]
        out_shape = [jax.ShapeDtypeStruct((m, n), f32), jax.ShapeDtypeStruct((m, n), bf16)]
    elif slots == 'rows':
        per = (m // N_DEV) // tm
        out_specs = pl.BlockSpec((None, None, tm, tn), lambda i, j, kk: ((i // per) % 2, (i // per) // 2, i % per, j))
        out_shape = jax.ShapeDtypeStruct((2, N_DEV // 2, m // N_DEV, n), out_dtype)
    elif slots == 'cols':
        per = (n // N_DEV) // tn
        out_specs = pl.BlockSpec((None, None, tm, tn), lambda i, j, kk: ((j // per) % 2, (j // per) // 2, i, j % per))
        out_shape = jax.ShapeDtypeStruct((2, N_DEV // 2, m, n // N_DEV), out_dtype)
    else:
        out_specs, out_shape = o_spec, jax.ShapeDtypeStruct((m, n), out_dtype)
        if epilogue == 'relu2_bwd':
            in_specs.append(o_spec)
            args.append(extra)
    scratch = [pltpu.VMEM((tm, tn), f32)] if nk > 1 else []
    if carry is None:
        return pl.pallas_call(body, grid=(m // tm, n // tn, nk), in_specs=in_specs, out_specs=out_specs,
                              out_shape=out_shape, scratch_shapes=scratch, name=name)(*args)
    single = not isinstance(out_shape, list)
    res, carried = _call_with_carry(name, body, (m // tm, n // tn, nk), in_specs, [out_specs] if single else out_specs,
                                    [out_shape] if single else out_shape, scratch, args, carry)
    return (res[0] if single else res), carried


def final_loss(name, x, y, g, target):
    t, d = x.shape
    tile = min(ROW_TILE, t)

    def body(x_ref, y_ref, g_ref, t_ref, dx_ref, l_ref):
        i = pl.program_id(0)

        @pl.when(i == 0)
        def _():
            l_ref[...] = jnp.zeros_like(l_ref)

        err = x_ref[...] + _rms(y_ref[...], g_ref[...]) - t_ref[...]
        dx_ref[...] = err * (1.0 / d)
        l_ref[...] += 0.5 * jnp.sum(jnp.mean(err * err, axis=-1, keepdims=True), axis=0, keepdims=True)

    row = pl.BlockSpec((tile, d), lambda i: (i, 0))
    return pl.pallas_call(
        body, grid=(t // tile,), in_specs=[row, row, _full_spec(g), row],
        out_specs=[row, pl.BlockSpec((1, 1), lambda i: (0, 0))],
        out_shape=[jax.ShapeDtypeStruct((t, d), f32), jax.ShapeDtypeStruct((1, 1), f32)], name=name)(x, y, g, target)


def adamw(name, w, m, v, gslots, tr=128):
    nl, r, c = w.shape
    tr = min(tr, r)
    nr = r // tr
    ns = gslots[0].shape[0]
    c1 = 1.0 / (1.0 - ADAM_B1 ** ADAM_STEP)
    c2 = 1.0 / (1.0 - ADAM_B2 ** ADAM_STEP)

    def body(*refs):
        w_ref, m_ref, v_ref = refs[:3]
        g_refs = refs[3:3 + nl]
        go_ref, d_ref, mo_ref, vo_ref = refs[3 + nl:]
        l = pl.program_id(0)
        g = None
        for li in range(nl):
            s = g_refs[li][0].astype(f32)
            for k in range(1, ns):
                s = s + g_refs[li][k].astype(f32)
            g = s if g is None else jnp.where(l == li, s, g)
        mn = ADAM_B1 * m_ref[...] + (1.0 - ADAM_B1) * g
        vn = ADAM_B2 * v_ref[...] + (1.0 - ADAM_B2) * jnp.square(g)
        go_ref[...] = g
        mo_ref[...] = mn
        vo_ref[...] = vn
        d_ref[...] = -ADAM_LR * ((mn * c1) / (jnp.sqrt(vn * c2) + ADAM_EPS) + ADAM_WD * w_ref[...])

    blk = pl.BlockSpec((None, tr, c), lambda l, i: (l, i, 0))

    def gspec(li):
        return pl.BlockSpec((ns, tr, c), lambda l, i: (0, jnp.where(l == li, i, jnp.where(l < li, 0, nr - 1)), 0))

    return pl.pallas_call(
        body, grid=(nl, nr), in_specs=[blk, blk, blk] + [gspec(li) for li in range(nl)],
        out_specs=[blk] * 4, out_shape=[jax.ShapeDtypeStruct(w.shape, f32)] * 4, name=name)(w, m, v, *gslots)


def exchange(name, arrays, gather, group):
    n = len(arrays)
    ns = {'all': 8, 'chips': 4, 'core': 2}[group]

    def body(*refs):
        ins, outs = refs[:n], refs[n:2 * n]
        send_sems, recv_sems, loc_sems = refs[2 * n:]
        x, y, c = lax.axis_index("x"), lax.axis_index("y"), lax.axis_index("c")

        def member(k):
            if group == 'all':
                px, py, pc = x ^ ((k >> 2) & 1), y ^ ((k >> 1) & 1), c ^ (k & 1)
                return (px, py, pc), 4 * px + 2 * py + pc
            if group == 'chips':
                px, py = x ^ ((k >> 1) & 1), y ^ (k & 1)
                return (px, py, c), 2 * px + py
            return (x, y, c ^ k), c ^ k

        _, me = member(0)
        sends, recvs, locs = [], [], []
        for a in range(n):
            lc = pltpu.make_async_copy(ins[a] if gather else ins[a].at[me], outs[a].at[me], loc_sems.at[a])
            lc.start()
            locs.append(lc)
            for k in range(1, ns):
                dev, peer = member(k)
                src = ins[a] if gather else ins[a].at[peer]
                cp = pltpu.make_async_remote_copy(src_ref=src, dst_ref=outs[a].at[me], send_sem=send_sems.at[a, k],
                                                  recv_sem=recv_sems.at[a, k], device_id=dev, device_id_type=MESH)
                cp.start()
                sends.append(cp)
                recvs.append(pltpu.make_async_remote_copy(src_ref=src, dst_ref=outs[a].at[peer], send_sem=send_sems.at[a, k],
                                                          recv_sem=recv_sems.at[a, k], device_id=dev, device_id_type=MESH))
        for cp in recvs:
            cp.wait_recv()
        for cp in sends:
            cp.wait_send()
        for lc in locs:
            lc.wait()

    anyspec = pl.BlockSpec(memory_space=pl.ANY)
    out_shape = [jax.ShapeDtypeStruct(((ns,) + a.shape) if gather else a.shape, a.dtype) for a in arrays]
    return pl.pallas_call(
        body, in_specs=[anyspec] * n, out_specs=[anyspec] * n, out_shape=out_shape,
        scratch_shapes=[pltpu.SemaphoreType.DMA((n, ns)), pltpu.SemaphoreType.DMA((n, ns)),
                        pltpu.SemaphoreType.DMA((n,))], name=name)(*arrays)


def gather_two_level(name, arrays):
    n = len(arrays)

    def body(*refs):
        start, finish = _gather_parts(refs[:n], refs[n:2 * n], *refs[2 * n:])
        start()
        finish()

    anyspec = pl.BlockSpec(memory_space=pl.ANY)
    return pl.pallas_call(
        body, in_specs=[anyspec] * n, out_specs=[anyspec] * n, out_shape=_carry_out_shape('gather', arrays),
        scratch_shapes=_carry_sems('gather', n), name=name)(*arrays)


def _gather_parts(ins, outs, send_sems, recv_sems, loc_sems):
    n = len(ins)
    x, y, c = lax.axis_index("x"), lax.axis_index("y"), lax.axis_index("c")
    sib = (x, y, 1 - c)

    def chip(k):
        px, py = x ^ ((k >> 1) & 1), y ^ (k & 1)
        return (px, py), 2 * px + py

    _, mine = chip(0)

    def copy(a, sem, src, slot, to):
        return pltpu.make_async_remote_copy(src_ref=src, dst_ref=outs[a].at[slot], send_sem=send_sems.at[a, sem],
                                            recv_sem=recv_sems.at[a, sem], device_id=to, device_id_type=MESH)

    def local(a):
        return pltpu.make_async_copy(ins[a], outs[a].at[2 * mine + c], loc_sems.at[a])

    def own_sends(a):
        cps = [copy(a, 0, ins[a], 2 * mine + c, sib)]
        for k in range(1, 4):
            (px, py), _ = chip(k)
            cps.append(copy(a, k, ins[a], 2 * mine + c, (px, py, c)))
        return cps

    def start():
        for a in range(n):
            local(a).start()
            for cp in own_sends(a):
                cp.start()

    def finish():
        passed = []
        for a in range(n):
            for k in range(1, 4):
                _, other = chip(k)
                slot = 2 * other + c
                copy(a, k, outs[a].at[slot], slot, sib).wait_recv()
                fw = copy(a, 3 + k, outs[a].at[slot], slot, sib)
                fw.start()
                passed.append(fw)
        for a in range(n):
            copy(a, 0, ins[a], 2 * mine + 1 - c, sib).wait_recv()
            for k in range(1, 4):
                _, other = chip(k)
                slot = 2 * other + 1 - c
                copy(a, 3 + k, outs[a].at[slot], slot, sib).wait_recv()
        for a in range(n):
            for cp in own_sends(a):
                cp.wait_send()
        for cp in passed:
            cp.wait_send()
        for a in range(n):
            local(a).wait()

    return start, finish


def _scatter_parts(ins, outs, send_sems, recv_sems, loc_sems):
    n = len(ins)
    x, y, c = lax.axis_index("x"), lax.axis_index("y"), lax.axis_index("c")
    mine = 2 * x + y

    def local(a):
        return pltpu.make_async_copy(ins[a].at[mine], outs[a].at[mine], loc_sems.at[a])

    def remote(a, k, slot):
        px, py = x ^ ((k >> 1) & 1), y ^ (k & 1)
        return pltpu.make_async_remote_copy(src_ref=ins[a].at[2 * px + py], dst_ref=outs[a].at[slot],
                                            send_sem=send_sems.at[a, k], recv_sem=recv_sems.at[a, k],
                                            device_id=(px, py, c), device_id_type=MESH)

    def start():
        for a in range(n):
            local(a).start()
            for k in range(1, 4):
                remote(a, k, mine).start()

    def finish():
        for a in range(n):
            for k in range(1, 4):
                remote(a, k, 2 * (x ^ ((k >> 1) & 1)) + (y ^ (k & 1))).wait_recv()
        for a in range(n):
            for k in range(1, 4):
                remote(a, k, mine).wait_send()
            local(a).wait()

    return start, finish


def _carry_out_shape(kind, arrays):
    if kind == 'gather':
        return [jax.ShapeDtypeStruct((N_DEV,) + a.shape, a.dtype) for a in arrays]
    return [jax.ShapeDtypeStruct(a.shape, a.dtype) for a in arrays]


def _carry_sems(kind, n):
    k = 7 if kind == 'gather' else 4
    return [pltpu.SemaphoreType.DMA((n, k)), pltpu.SemaphoreType.DMA((n, k)), pltpu.SemaphoreType.DMA((n,))]


def _carry_parts(kind, ins, outs, sems):
    return (_gather_parts if kind == 'gather' else _scatter_parts)(ins, outs, *sems)


def chip_partials(tag, slots):
    from_sibling = send_to_sibling(f"to_sibling_{tag}", slots)
    return [chip_sum(f"chip_sum_{tag}{i}", b, o) for i, (b, o) in enumerate(zip(slots, from_sibling))]


def send_to_sibling(name, arrays):
    n = len(arrays)

    def body(*refs):
        ins, outs = refs[:n], refs[n:2 * n]
        send_sems, recv_sems = refs[2 * n:]
        x, y, c = lax.axis_index("x"), lax.axis_index("y"), lax.axis_index("c")
        cps = [pltpu.make_async_remote_copy(src_ref=ins[a].at[1 - c], dst_ref=outs[a], send_sem=send_sems.at[a],
                                            recv_sem=recv_sems.at[a], device_id=(x, y, 1 - c), device_id_type=MESH)
               for a in range(n)]
        for cp in cps:
            cp.start()
        for cp in cps:
            cp.wait()

    anyspec = pl.BlockSpec(memory_space=pl.ANY)
    out_shape = [jax.ShapeDtypeStruct(a.shape[1:], a.dtype) for a in arrays]
    return pl.pallas_call(
        body, in_specs=[anyspec] * n, out_specs=[anyspec] * n, out_shape=out_shape,
        scratch_shapes=[pltpu.SemaphoreType.DMA((n,)), pltpu.SemaphoreType.DMA((n,))], name=name)(*arrays)


def chip_sum(name, both, other):
    _, nc, r, c = both.shape
    tr = ROW_TILE if r % ROW_TILE == 0 else r

    def body(b_ref, o_ref, s_ref):
        core = lax.axis_index("c")
        own = jnp.where(core == 0, b_ref[0], b_ref[1]).astype(f32)
        s_ref[...] = (own + o_ref[...].astype(f32)).astype(s_ref.dtype)

    return pl.pallas_call(
        body, grid=(nc, r // tr),
        in_specs=[pl.BlockSpec((2, None, tr, c), lambda s, i: (0, s, i, 0)),
                  pl.BlockSpec((None, tr, c), lambda s, i: (s, i, 0))],
        out_specs=pl.BlockSpec((None, tr, c), lambda s, i: (s, i, 0)),
        out_shape=jax.ShapeDtypeStruct((nc, r, c), both.dtype), name=name)(both, other)


REPLICATED = ('lower_bounds', 'norm_mix_pre', 'norm_mix_post', 'norm_ff_pre', 'norm_ff_post', 'hgrn_norm_w',
              'gdn_a_log', 'gdn_dt_bias', 'gdn_norm_w', 'gmlp_ln_w', 'gmlp_ln_b', 'gmlp_w_s', 'gmlp_b_s',
              'conv_dw_b', 'conv_ln_w', 'conv_ln_b')
WEIGHTS = ('lower_bounds', 'norm_mix_pre', 'norm_mix_post', 'norm_ff_pre', 'norm_ff_post', 'w_in', 'w_out',
           'hgrn_norm_w', 'gdn_conv_w', 'gdn_a_log', 'gdn_dt_bias', 'gdn_norm_w', 'gmlp_ln_w', 'gmlp_ln_b',
           'gmlp_w_s', 'gmlp_b_s', 'conv_dw_w', 'conv_dw_b', 'conv_ln_w', 'conv_ln_b', 'w_ff1', 'w_ff2')


def _row(v):
    return v.reshape(1, -1)


def _pad_lanes(v, offset):
    return jnp.pad(v, (offset, 128 - offset - v.shape[0])).reshape(1, 128)


def _relayout_w_in(g):
    full = jnp.moveaxis(g[:, 0], 0, 1).reshape(D_MODEL, D_IN)
    return jnp.concatenate([full[:, :8 * GW], full[:, 8 * GW + 2 * NH:], full[:, 8 * GW:8 * GW + 2 * NH],
                            jnp.zeros((D_MODEL, 128 - 2 * NH), bf16)], axis=1)


def _layer_fwd(l, x0, params, lb_all, shards):
    p = params[l]
    sv = {'x0': x0}
    (h,) = rowwise(f"norm_mix_pre{l}", fn_norm, [(x0, D_MODEL, 0)], [p['g_mix_pre']], [(D_MODEL, bf16)], ROW_TILE)
    proj = matmul(f"proj{l}", h, p['w_in'], 'nn', f32, tn=896)
    sv.update(h=h, proj=proj)
    lb = lb_all[l:l + 1]
    o_a, st_a, (p['w_ff1'],) = scan_fwd(f"hgrn{l}", fn_hgrn, [(proj, 0), (proj, 4), (proj, 8), (proj, 12)], [lb],
                                        [p['hgrn_norm_w']], bf16, NH, carry=('gather', [shards['w_ff1', l]]))
    q, k, v, beta, g = halo_fwd(
        f"gdn_pre{l}", fn_gdn_pre,
        [(proj, GW, 4, True), (proj, GW, 5, True), (proj, GW, 6, True), (proj, 128, 48, False)],
        [p['gdn_conv_w'], p['alog'], p['dtb']], [(GW, f32)] * 5, ROW_TILE, 8)
    wanted = [shards['w_ff2', l]] + ([shards['w_out']] if l == 0 else []) + ([shards['w_in', l + 1]] if l + 1 < DEPTH else [])
    o_b, st_b, tinv_b, got = scan_fwd(f"gdn{l}", fn_gdn, [(q, 0), (k, 0), (v, 0), (beta, 0), (g, 0), (proj, 28)], [],
                                      [p['gdn_norm_w']], bf16, NH, n_extra=1, carry=('gather', wanted))
    p['w_ff2'] = got.pop(0)
    if l == 0:
        w_out_full = jnp.moveaxis(got.pop(0), 0, 1).reshape(DEPTH, D_MODEL, D_MODEL)
        for ll in range(DEPTH):
            params[ll]['w_out'] = w_out_full[ll]
    if l + 1 < DEPTH:
        params[l + 1]['w_in'] = _relayout_w_in(got.pop(0))
    (o_c,) = rowwise(f"gmlp{l}", fn_gmlp, [(proj, GW, 8), (proj, GW, 9)],
                     [p['gmlp_ln_w'], p['gmlp_ln_b'], p['gmlp_w_s'], p['gmlp_b_s']], [(GW, bf16)], MIX_CHUNK)
    (o_d,) = halo_fwd(f"conv{l}", fn_conv, [(proj, GW, 10, True), (proj, GW, 11, True)],
                      [p['conv_dw_w'], p['conv_dw_b'], p['conv_ln_w'], p['conv_ln_b']], [(GW, bf16)], ROW_TILE, 32)
    mix = jnp.concatenate([o_a, o_b, o_c, o_d], axis=1)
    y1 = matmul(f"out_proj{l}", mix, p['w_out'], 'nn', f32)
    (x1,) = rowwise(f"res_mix{l}", lambda x, y, gg: (x + _rms(y, gg),), [(x0, D_MODEL, 0), (y1, D_MODEL, 0)],
                    [p['g_mix_post']], [(D_MODEL, f32)], ROW_TILE)
    (h2,) = rowwise(f"norm_ff_pre{l}", fn_norm, [(x1, D_MODEL, 0)], [p['g_ff_pre']], [(D_MODEL, bf16)], ROW_TILE)
    u, a = matmul(f"ff1_{l}", h2, p['w_ff1'], 'nn', epilogue='relu2', b_gathered=('cols', 0))
    y2 = matmul(f"ff2_{l}", a, p['w_ff2'], 'nn', f32, b_gathered=('rows', 0))
    sv.update(st_a=st_a, q=q, k=k, v=v, beta=beta, g=g, st_b=st_b, tinv_b=tinv_b, mix=mix, y1=y1, x1=x1, h2=h2, u=u, a=a, y2=y2)
    return sv


def _slots_w_in(gt):
    gl = jnp.concatenate([gt[:8 * GW], gt[12 * GW:12 * GW + 2 * NH], gt[8 * GW:12 * GW]], axis=0)
    return jnp.transpose(gl.reshape(N_DEV // 2, 2, D_IN // N_DEV, D_MODEL), (1, 0, 2, 3))


def sum_slots(name, slots):
    ns, r, c = slots.shape
    tc = 512

    def body(s_ref, o_ref):
        tot = s_ref[0].astype(f32)
        for k in range(1, ns):
            tot = tot + s_ref[k].astype(f32)
        o_ref[...] = tot

    return pl.pallas_call(
        body, grid=(c // tc,), in_specs=[pl.BlockSpec((ns, r, tc), lambda j: (0, 0, j))],
        out_specs=pl.BlockSpec((r, tc), lambda j: (0, j)), out_shape=jax.ShapeDtypeStruct((r, c), f32), name=name)(slots)


def _layer_bwd(l, dx, sv, p, lb_all, received):
    gr = {}
    (dy2,), (gr['norm_ff_post'],) = rowwise_bwd(f"res_ff_bwd{l}", fn_norm, [(sv['y2'], D_MODEL, 0)], [p['g_ff_post']],
                                                [(dx, D_MODEL, 0)], [bf16], ROW_TILE)
    du = matmul(f"ff2_dx{l}", dy2, p['w_ff2'], 'nt', bf16, epilogue='relu2_bwd', extra=sv['u'], b_gathered=('rows', 0))
    g_ff2 = matmul(f"ff2_dw{l}", sv['a'], dy2, 'tn', bf16, slots='rows')
    dh2 = matmul(f"ff1_dx{l}", du, p['w_ff1'], 'nt', f32, b_gathered=('cols', 0))
    g_ff1 = matmul(f"ff1_dw{l}", sv['h2'], du, 'tn', bf16, slots='cols')
    (dx1,), (gr['norm_ff_pre'],) = rowwise_bwd(f"norm_ff_pre_bwd{l}", fn_norm, [(sv['x1'], D_MODEL, 0)], [p['g_ff_pre']],
                                               [(dh2, D_MODEL, 0)], [f32], ROW_TILE, addto=(dx, D_MODEL, 0))
    (dy1,), (gr['norm_mix_post'],) = rowwise_bwd(f"res_mix_bwd{l}", fn_norm, [(sv['y1'], D_MODEL, 0)], [p['g_mix_post']],
                                                 [(dx1, D_MODEL, 0)], [bf16], ROW_TILE)
    dmix = matmul(f"out_proj_dx{l}", dy1, p['w_out'], 'nt', f32)
    g_out = matmul(f"out_proj_dw{l}", sv['mix'], dy1, 'tn', bf16, slots='rows')
    part_ff2, part_ff1, part_out = chip_partials(f"l{l}_", [g_ff2, g_ff1, g_out])
    proj = sv['proj']
    lb = lb_all[l:l + 1]
    going = [(('w_ff2', l), part_ff2)]
    d_a, (dlb,), (gr['hgrn_norm_w'],), got = scan_bwd(
        f"hgrn_bwd{l}", fn_hgrn, [(proj, 0), (proj, 4), (proj, 8), (proj, 12)], [lb], [p['hgrn_norm_w']],
        sv['st_a'], (dmix, 0), [bf16] * 4, NH, carry=('scatter', [a for _, a in going]))
    received.update({key: r for (key, _), r in zip(going, got)})
    going = [(('w_ff1', l), part_ff1), (('w_out', l), part_out)]
    d_b, _, (gr['gdn_norm_w'],), got = scan_bwd(
        f"gdn_bwd{l}", fn_gdn, [(sv['q'], 0), (sv['k'], 0), (sv['v'], 0), (sv['beta'], 0), (sv['g'], 0), (proj, 28)],
        [], [p['gdn_norm_w']], sv['st_b'], (dmix, 4), [f32] * 5 + [bf16], NH, extras=[sv['tinv_b']],
        carry=('scatter', [a for _, a in going]))
    received.update({key: r for (key, _), r in zip(going, got)})
    d_bp, (gr['gdn_conv_w'], dalog, ddtb) = halo_bwd(
        f"gdn_pre_bwd{l}", fn_gdn_pre,
        [(proj, GW, 4, True), (proj, GW, 5, True), (proj, GW, 6, True), (proj, 128, 48, False)],
        [p['gdn_conv_w'], p['alog'], p['dtb']], [(d_b[j], GW, 0) for j in range(5)], bf16, ROW_TILE, 8)
    gr['gdn_a_log'] = dalog[0, NH:2 * NH]
    gr['gdn_dt_bias'] = ddtb[0, NH:2 * NH]
    d_c, (gr['gmlp_ln_w'], gr['gmlp_ln_b'], gr['gmlp_w_s'], gr['gmlp_b_s']) = rowwise_bwd(
        f"gmlp_bwd{l}", fn_gmlp, [(proj, GW, 8), (proj, GW, 9)],
        [p['gmlp_ln_w'], p['gmlp_ln_b'], p['gmlp_w_s'], p['gmlp_b_s']], [(dmix, GW, 2)], [bf16, bf16], MIX_CHUNK)
    d_d, (gr['conv_dw_w'], gr['conv_dw_b'], gr['conv_ln_w'], gr['conv_ln_b']) = halo_bwd(
        f"conv_bwd{l}", fn_conv, [(proj, GW, 10, True), (proj, GW, 11, True)],
        [p['conv_dw_w'], p['conv_dw_b'], p['conv_ln_w'], p['conv_ln_b']], [(dmix, GW, 3)], bf16, ROW_TILE, 32)
    dproj = jnp.concatenate(list(d_a) + [d_bp[0], d_bp[1], d_bp[2], d_b[5]] + list(d_c) + list(d_d) + [d_bp[3]], axis=1)
    g_in_t = matmul(f"proj_dw{l}", dproj, sv['h'], 'tn', bf16, tm=896)
    (part_in,) = chip_partials(f"l{l}_in", [_slots_w_in(g_in_t)])
    dh, (received['w_in', l],) = matmul(f"proj_dx{l}", dproj, p['w_in'], 'nt', f32, tm=512, tn=512, tk=D_IN_PAD,
                                        carry=('scatter', [part_in]))
    (dx0,), (gr['norm_mix_pre'],) = rowwise_bwd(f"norm_mix_pre_bwd{l}", fn_norm, [(sv['x0'], D_MODEL, 0)], [p['g_mix_pre']],
                                                [(dh, D_MODEL, 0)], [f32], ROW_TILE, addto=(dx1, D_MODEL, 0))
    return dx0, gr, dlb


def kernel(x, lower_bounds, norm_mix_pre, norm_mix_post, norm_ff_pre, norm_ff_post, w_in, w_out, hgrn_norm_w, gdn_conv_w, gdn_a_log, gdn_dt_bias, gdn_norm_w, gmlp_ln_w, gmlp_ln_b, gmlp_w_s, gmlp_b_s, conv_dw_w, conv_dw_b, conv_ln_w, conv_ln_b, w_ff1, w_ff2, loss_target, m_lower_bounds, m_norm_mix_pre, m_norm_mix_post, m_norm_ff_pre, m_norm_ff_post, m_w_in, m_w_out, m_hgrn_norm_w, m_gdn_conv_w, m_gdn_a_log, m_gdn_dt_bias, m_gdn_norm_w, m_gmlp_ln_w, m_gmlp_ln_b, m_gmlp_w_s, m_gmlp_b_s, m_conv_dw_w, m_conv_dw_b, m_conv_ln_w, m_conv_ln_b, m_w_ff1, m_w_ff2, v_lower_bounds, v_norm_mix_pre, v_norm_mix_post, v_norm_ff_pre, v_norm_ff_post, v_w_in, v_w_out, v_hgrn_norm_w, v_gdn_conv_w, v_gdn_a_log, v_gdn_dt_bias, v_gdn_norm_w, v_gmlp_ln_w, v_gmlp_ln_b, v_gmlp_w_s, v_gmlp_b_s, v_conv_dw_w, v_conv_dw_b, v_conv_ln_w, v_conv_ln_b, v_w_ff1, v_w_ff2):
    loc = dict(locals())
    W = {n: loc[n] for n in WEIGHTS}
    M = {n: loc['m_' + n] for n in WEIGHTS}
    V = {n: loc['v_' + n] for n in WEIGHTS}
    t = x.shape[1]
    me = 4 * lax.axis_index("x") + 2 * lax.axis_index("y") + lax.axis_index("c")

    shards = {'w_out': w_out.astype(bf16)}
    for l in range(DEPTH):
        shards['w_in', l] = w_in[l:l + 1].astype(bf16)
        shards['w_ff1', l] = w_ff1[l:l + 1].astype(bf16)
        shards['w_ff2', l] = w_ff2[l:l + 1].astype(bf16)
    g_in0, g_gconv, g_dconv = gather_two_level("gather_weights", [shards['w_in', 0], gdn_conv_w, conv_dw_w])
    gconv_full = jnp.moveaxis(g_gconv, 0, 2).reshape(DEPTH, SHORT_CONV, 3 * GW)
    dconv_full = jnp.moveaxis(g_dconv, 0, 2).reshape(DEPTH, CONV_WIDTH, GW)

    (lb_all,) = rowwise("lower_bounds", fn_lb, [(lower_bounds, GW, 0)], [], [(GW, f32)], DEPTH)

    P = []
    for l in range(DEPTH):
        P.append(dict(
            g_mix_pre=_row(norm_mix_pre[l]), g_mix_post=_row(norm_mix_post[l]), g_ff_pre=_row(norm_ff_pre[l]),
            g_ff_post=_row(norm_ff_post[l]), hgrn_norm_w=_row(hgrn_norm_w[l]), gdn_conv_w=gconv_full[l],
            alog=_pad_lanes(gdn_a_log[l], NH), dtb=_pad_lanes(gdn_dt_bias[l], NH), gdn_norm_w=_row(gdn_norm_w[l]),
            gmlp_ln_w=_row(gmlp_ln_w[l]), gmlp_ln_b=_row(gmlp_ln_b[l]), gmlp_w_s=gmlp_w_s[l].reshape(NH * MIX_CHUNK, MIX_CHUNK),
            gmlp_b_s=gmlp_b_s[l], conv_dw_w=dconv_full[l], conv_dw_b=_row(conv_dw_b[l]), conv_ln_w=_row(conv_ln_w[l]),
            conv_ln_b=_row(conv_ln_b[l])))

    P[0]['w_in'] = _relayout_w_in(g_in0)
    xs = x[0]
    saved = []
    for l in range(DEPTH):
        sv = _layer_fwd(l, xs, P, lb_all, shards)
        saved.append(sv)
        if l < DEPTH - 1:
            (xs,) = rowwise(f"res_ff{l}", lambda a, y, gg: (a + _rms(y, gg),), [(sv['x1'], D_MODEL, 0), (sv['y2'], D_MODEL, 0)],
                            [P[l]['g_ff_post']], [(D_MODEL, f32)], ROW_TILE)
    sv = saved[-1]
    dx, loss_loc = final_loss("final_loss", sv['x1'], sv['y2'], P[-1]['g_ff_post'], loss_target[0])
    loss = lax.psum(loss_loc[0, 0], ("x", "y", "c"))

    G = {}
    dlb_rows = []
    received = {}
    for l in reversed(range(DEPTH)):
        dx, gr, dlb = _layer_bwd(l, dx, saved[l], P[l], lb_all, received)
        G[l] = gr
        dlb_rows.append(dlb)
    dlb_all = jnp.concatenate(dlb_rows[::-1], axis=0)
    (g_lower_bounds,), _ = rowwise_bwd("lower_bounds_bwd", fn_lb, [(lower_bounds, GW, 0)], [], [(dlb_all, GW, 0)],
                                       [f32], DEPTH)
    grad_x = dx[None]

    def stack(name, f=lambda a: a):
        return jnp.stack([f(G[l][name]) for l in range(DEPTH)], axis=0)

    full = {
        'lower_bounds': g_lower_bounds,
        'norm_mix_pre': stack('norm_mix_pre', lambda a: a[0]), 'norm_mix_post': stack('norm_mix_post', lambda a: a[0]),
        'norm_ff_pre': stack('norm_ff_pre', lambda a: a[0]), 'norm_ff_post': stack('norm_ff_post', lambda a: a[0]),
        'hgrn_norm_w': stack('hgrn_norm_w', lambda a: a[0]), 'gdn_a_log': stack('gdn_a_log'), 'gdn_dt_bias': stack('gdn_dt_bias'),
        'gdn_norm_w': stack('gdn_norm_w', lambda a: a[0]), 'gmlp_ln_w': stack('gmlp_ln_w', lambda a: a[0]),
        'gmlp_ln_b': stack('gmlp_ln_b', lambda a: a[0]),
        'gmlp_w_s': stack('gmlp_w_s', lambda a: a.reshape(NH, MIX_CHUNK, MIX_CHUNK)), 'gmlp_b_s': stack('gmlp_b_s'),
        'conv_dw_b': stack('conv_dw_b', lambda a: a[0]), 'conv_ln_w': stack('conv_ln_w', lambda a: a[0]),
        'conv_ln_b': stack('conv_ln_b', lambda a: a[0]),
        'gdn_conv_w': stack('gdn_conv_w'), 'conv_dw_w': stack('conv_dw_w'),
    }

    small_names = list(REPLICATED) + ['gdn_conv_w', 'conv_dw_w']
    flat = jnp.concatenate([full[n].reshape(-1) for n in small_names])
    n_small = flat.shape[0]
    n_pad = -(-n_small // 1024) * 1024
    packed = jnp.pad(flat, (0, n_pad - n_small)).reshape(n_pad // 128, 128)

    (small_slots,) = exchange("gather_small_grads", [packed], True, 'all')

    out = {}
    for l in range(DEPTH):
        received['w_in', l] = sum_slots(f"sum_w_in{l}", received['w_in', l]).T[None]
    for name in ('w_in', 'w_out', 'w_ff1', 'w_ff2'):
        out[name] = adamw(f"adamw_{name}", W[name], M[name], V[name], [received[name, l] for l in range(DEPTH)])

    def pack(d, fill):
        parts = [d[n].reshape(-1) for n in REPLICATED]
        parts.append(jnp.full((n_pad - sum(a.shape[0] for a in parts),), fill, f32))
        return jnp.concatenate(parts).reshape(1, n_pad // 128, 128)

    sm = adamw("adamw_small", pack(W, 0.0), pack(M, 0.0), pack(V, 1.0), [small_slots], tr=n_pad // 128)
    off = 0
    for n in REPLICATED:
        sz = W[n].size
        out[n] = tuple(a.reshape(-1)[off:off + sz].reshape(W[n].shape) for a in sm)
        off += sz
    gsum = sm[0].reshape(-1)
    for n, full_shape in (('gdn_conv_w', (DEPTH, SHORT_CONV, 3 * GW)), ('conv_dw_w', (DEPTH, CONV_WIDTH, GW))):
        sz = math.prod(full_shape)
        gfull = gsum[off:off + sz].reshape(full_shape)
        off += sz
        sh = W[n].shape
        gmine = lax.dynamic_slice_in_dim(gfull, me * sh[2], sh[2], axis=2)
        r = adamw(f"adamw_{n}", W[n].reshape(1, sh[0] * sh[1], sh[2]), M[n].reshape(1, sh[0] * sh[1], sh[2]),
                  V[n].reshape(1, sh[0] * sh[1], sh[2]), [gmine.reshape(1, sh[0] * sh[1], sh[2])], tr=sh[0] * sh[1])
        out[n] = tuple(a.reshape(sh) for a in r)

    return (loss, grad_x, *[out[n][0] for n in WEIGHTS], *[out[n][1] for n in WEIGHTS],
            *[out[n][2] for n in WEIGHTS], *[out[n][3] for n in WEIGHTS])
```

```python
import functools
import math

import jax
import jax.numpy as jnp
from jax import lax
from jax.experimental import pallas as pl
from jax.experimental.pallas import tpu as pltpu

f32 = jnp.float32
bf16 = jnp.bfloat16
HI = lax.Precision.HIGHEST

N_DEV = 8
DEPTH = 2
D_MODEL = 2048
GW = 512
HD = 128
NH = 4
CHUNK = 64
MIX_CHUNK = 128
CONV_WIDTH = 31
SHORT_CONV = 4
D_FF = 4 * D_MODEL
D_IN = 12 * GW + 2 * NH
D_IN_PAD = 12 * GW + 128
ROW_TILE = 256
HGRN_SUB = 16
SCAN_CHUNKS = 2
EPS = 1e-6
TINY = 1e-30
ADAM_LR, ADAM_B1, ADAM_B2, ADAM_EPS, ADAM_WD, ADAM_STEP = 0.001, 0.9, 0.999, 1e-08, 0.01, 10
MESH = pl.DeviceIdType.MESH


def _dotb(a, b, ca, cb):
    return lax.dot_general(a.astype(bf16), b.astype(bf16), (((ca,), (cb,)), ((), ())),
                           preferred_element_type=f32)


@jax.custom_vjp
def mm(a, b):
    return _dotb(a, b, 1, 0)


def _mm_f(a, b):
    return mm(a, b), (a, b)


def _mm_b(res, ct):
    a, b = res
    return _dotb(ct, b, 1, 1), _dotb(a, ct, 0, 0)


mm.defvjp(_mm_f, _mm_b)


@jax.custom_vjp
def mm_nt(a, b):
    return _dotb(a, b, 1, 1)


def _mmnt_f(a, b):
    return mm_nt(a, b), (a, b)


def _mmnt_b(res, ct):
    a, b = res
    return _dotb(ct, b, 1, 0), _dotb(ct, a, 0, 0)


mm_nt.defvjp(_mmnt_f, _mmnt_b)


@jax.custom_vjp
def mm_tn(a, b):
    return _dotb(a, b, 0, 0)


def _mmtn_f(a, b):
    return mm_tn(a, b), (a, b)


def _mmtn_b(res, ct):
    a, b = res
    return _dotb(b, ct, 1, 1), _dotb(a, ct, 1, 0)


mm_tn.defvjp(_mmtn_f, _mmtn_b)


def mmh(a, b):
    return jnp.dot(a, b, precision=HI, preferred_element_type=f32)


def mm3(a, b):
    return jnp.dot(a, b, precision=lax.Precision.HIGH, preferred_element_type=f32)


def _rms(x, w):
    return x * lax.rsqrt(jnp.mean(x * x, axis=-1, keepdims=True) + EPS) * w


def _ln(x, w, b):
    mu = jnp.mean(x, axis=-1, keepdims=True)
    xc = x - mu
    var = jnp.mean(xc * xc, axis=-1, keepdims=True)
    return xc * lax.rsqrt(var + EPS) * w + b


def _gelu(x):
    return 0.5 * x * (1.0 + lax.erf(x * (2.0 ** -0.5)))


def _iota2(n, m, axis):
    return lax.broadcasted_iota(jnp.int32, (n, m), axis)


def _tri(n, strict=False):
    r, c = _iota2(n, n, 0), _iota2(n, n, 1)
    return (r > c) if strict else (r >= c)


def _eye(n):
    return (_iota2(n, n, 0) == _iota2(n, n, 1)).astype(f32)


def fn_norm(x, g):
    return (_rms(x, g),)


def fn_lb(lower_bounds):
    s = jax.nn.softmax(lower_bounds, axis=0)
    rows, cum = [], None
    for i in range(DEPTH):
        cum = s[i:i + 1] if cum is None else cum + s[i:i + 1]
        rows.append(cum - s[0:1])
    return (jnp.concatenate(rows, axis=0),)


def fn_hgrn(aq, af, ai, ag, lb, nw, st):
    c = aq.shape[0]
    sig = jax.nn.sigmoid(af)
    f = lb + (1.0 - lb) * sig
    logf = jnp.log(jnp.maximum(f, TINY))
    k = (1.0 - lb) * jax.nn.sigmoid(-af)
    q = jax.nn.silu(aq)
    v = ai
    b = mmh(_tri(c).astype(f32), logf)
    outs = []
    for lo in range(0, c, HGRN_SUB):
        qi, ki, vi, bi = (a[lo:lo + HGRN_SUB] for a in (q, k, v, b))
        rel = bi[:, None, :] - bi[None, :, :]
        dec = jnp.exp(jnp.minimum(rel, 0.0)) * qi[:, None, :] * ki[None, :, :]
        o_blk = mm(jnp.where(_tri(HGRN_SUB), jnp.sum(dec, axis=-1), 0.0), vi)
        if lo > 0:
            r = b[lo - 1:lo, :]
            o_blk = o_blk + mm(mm_nt(qi * jnp.exp(bi - r), k[:lo] * jnp.exp(r - b[:lo])), v[:lo])
        outs.append(o_blk)
    b_end = b[c - 1:c, :]
    out = jnp.concatenate(outs, axis=0) + mm_nt(q * jnp.exp(b), st)
    st_new = st * jnp.exp(b_end) + mm_tn(v, k * jnp.exp(b_end - b))
    o = _rms(out, nw) * jax.nn.silu(ag)
    return o, st_new


def fn_gdn_pre(tq, tk, tv, bq, bk, bv, p8, conv_w, alog, dtb):
    tile = bq.shape[0]
    h = tq.shape[0]
    outs = []
    for seg, (tl, cur) in enumerate(((tq, bq), (tk, bk), (tv, bv))):
        xe = jnp.concatenate([tl, cur], axis=0)
        acc = None
        for kk in range(SHORT_CONV):
            off = h - (SHORT_CONV - 1) + kk
            term = conv_w[kk:kk + 1, seg * GW:(seg + 1) * GW] * xe[off:off + tile, :]
            acc = term if acc is None else acc + term
        outs.append(jax.nn.silu(acc))
    sq, sk, sv = outs
    qh, kh = [], []
    for hh in range(NH):
        a = sq[:, hh * HD:(hh + 1) * HD]
        qh.append(a * lax.rsqrt(jnp.sum(a * a, axis=-1, keepdims=True) + EPS) * (HD ** -0.5))
        a = sk[:, hh * HD:(hh + 1) * HD]
        kh.append(a * lax.rsqrt(jnp.sum(a * a, axis=-1, keepdims=True) + EPS))
    q = jnp.concatenate(qh, axis=1)
    k = jnp.concatenate(kh, axis=1)
    beta = jax.nn.sigmoid(p8)
    g = -jnp.exp(alog) * jax.nn.softplus(p8 + dtb)
    r, cc = _iota2(128, GW, 0), _iota2(128, GW, 1) // HD
    e_beta = (r == cc).astype(f32)
    e_g = (r == cc + NH).astype(f32)
    return q, k, sv, mmh(beta, e_beta), mmh(g, e_g)


@jax.custom_vjp
def _inverse_given(m, tinv):
    return tinv


def _inverse_given_f(m, tinv):
    return tinv, tinv


def _inverse_given_b(tinv, ct):
    x = lax.dot_general(ct, tinv, (((1,), (1,)), ((), ())), precision=lax.Precision.HIGH, preferred_element_type=f32)
    dm = -lax.dot_general(tinv, x, (((0,), (0,)), ((), ())), precision=lax.Precision.HIGH, preferred_element_type=f32)
    return dm, jnp.zeros_like(tinv)


_inverse_given.defvjp(_inverse_given_f, _inverse_given_b)


def fn_gdn(q, k, v, beta, g, z, nw, s, tinv_saved=None):
    c = q.shape[0]
    gc = mm3(_tri(c).astype(f32), g)
    gcol = gc[:, 0:1]
    grow = jnp.sum(gcol * _eye(c), axis=0, keepdims=True)
    gamma = jnp.where(_tri(c), jnp.exp(jnp.minimum(gcol - grow, 0.0)), 0.0)
    kb = k * beta
    m = jnp.where(_tri(c, strict=True), mm_nt(kb, k) * gamma, 0.0)
    if tinv_saved is None:
        eye = _eye(c)
        tinv = eye - m
        p = m
        for _ in range(int(math.log2(c)) - 1):
            p = mm3(p, p)
            tinv = mm3(tinv, eye + p)
    else:
        tinv = _inverse_given(m, tinv_saved)
    egc = jnp.exp(gc)
    u = mm3(tinv, v * beta)
    w = mm3(tinv, kb * egc)
    qk = mm_nt(q, k) * gamma
    gc_end = gc[c - 1:c, :]
    q_dec = q * egc
    k_dec = k * jnp.exp(gc_end - gc)
    v_new = u - mm(w, s)
    out = mm(q_dec, s) + mm(qk, v_new)
    s_new = s * jnp.exp(gc_end) + mm_tn(k_dec, v_new)
    o = _rms(out, nw) * jax.nn.silu(z)
    return (o, s_new, tinv) if tinv_saved is None else (o, s_new)


def fn_gmlp(cu, cv, ln_w, ln_b, w_s, b_s):
    n = cu.shape[0]
    ug = _gelu(cu)
    vn = _ln(_gelu(cv), ln_w, ln_b)
    eye = _eye(n)
    cols = []
    for hh in range(NH):
        wc = jnp.where(_tri(n), w_s[hh * n:(hh + 1) * n, :], 0.0)
        bcol = jnp.sum(b_s[hh:hh + 1, :] * eye, axis=1, keepdims=True)
        cols.append(mm(wc, vn[:, hh * HD:(hh + 1) * HD]) + bcol)
    return (ug * jnp.concatenate(cols, axis=1),)


def fn_conv(ta, tg, a, gate, dw_w, dw_b, ln_w, ln_b):
    tile = a.shape[0]
    h = ta.shape[0]
    ya = jnp.concatenate([ta, a], axis=0)
    yg = jnp.concatenate([tg, gate], axis=0)
    y = ya * jax.nn.sigmoid(yg)
    acc = None
    for kk in range(CONV_WIDTH):
        off = h - (CONV_WIDTH - 1) + kk
        term = dw_w[kk:kk + 1, :] * y[off:off + tile, :]
        acc = term if acc is None else acc + term
    return (jax.nn.silu(_ln(acc + dw_b, ln_w, ln_b)),)


def _full_spec(arr):
    nd = arr.ndim
    return pl.BlockSpec(arr.shape, lambda *_: (0,) * nd)


def rowwise(name, fn, tiled, params, outs, tile):
    t = tiled[0][0].shape[0]
    tile = min(tile, t)
    nt, np_ = len(tiled), len(params)

    def body(*refs):
        vals = [r[...].astype(f32) for r in refs[:nt + np_]]
        res = fn(*vals)
        for o_ref, r in zip(refs[nt + np_:], res):
            o_ref[...] = r.astype(o_ref.dtype)

    in_specs = [pl.BlockSpec((tile, w), lambda i, c=c: (i, c)) for _, w, c in tiled]
    in_specs += [_full_spec(p) for p in params]
    out_specs = [pl.BlockSpec((tile, w), lambda i: (i, 0)) for w, _ in outs]
    out_shape = [jax.ShapeDtypeStruct((t, w), dt) for w, dt in outs]
    return pl.pallas_call(body, grid=(t // tile,), in_specs=in_specs, out_specs=out_specs,
                          out_shape=out_shape, name=name)(*[a for a, _, _ in tiled], *params)


def rowwise_bwd(name, fn, tiled, params, cots, gouts, tile, addto=None):
    t = tiled[0][0].shape[0]
    tile = min(tile, t)
    nt, np_, nc = len(tiled), len(params), len(cots)
    na = 0 if addto is None else 1
    gidx = [i for i, g in enumerate(gouts) if g is not None]

    def body(*refs):
        i = pl.program_id(0)
        vals = [r[...].astype(f32) for r in refs[:nt + np_]]
        cvals = tuple(r[...].astype(f32) for r in refs[nt + np_:nt + np_ + nc])
        _, vjp = jax.vjp(fn, *vals)
        grads = vjp(cvals)
        orefs = refs[nt + np_ + nc + na:]
        for n, j in enumerate(gidx):
            g = grads[j]
            if na and n == 0:
                g = g + refs[nt + np_ + nc][...].astype(f32)
            orefs[n][...] = g.astype(orefs[n].dtype)
        prefs = orefs[len(gidx):]

        @pl.when(i == 0)
        def _():
            for r in prefs:
                r[...] = jnp.zeros_like(r)

        for r, g in zip(prefs, grads[nt:]):
            r[...] += g

    in_specs = [pl.BlockSpec((tile, w), lambda i, c=c: (i, c)) for _, w, c in tiled]
    in_specs += [_full_spec(p) for p in params]
    in_specs += [pl.BlockSpec((tile, w), lambda i, c=c: (i, c)) for _, w, c in cots]
    args = [a for a, _, _ in tiled] + list(params) + [a for a, _, _ in cots]
    if na:
        in_specs.append(pl.BlockSpec((tile, addto[1]), lambda i, c=addto[2]: (i, c)))
        args.append(addto[0])
    out_specs = [pl.BlockSpec((tile, tiled[j][1]), lambda i: (i, 0)) for j in gidx]
    out_shape = [jax.ShapeDtypeStruct((t, tiled[j][1]), gouts[j]) for j in gidx]
    out_specs += [_full_spec(p) for p in params]
    out_shape += [jax.ShapeDtypeStruct(p.shape, f32) for p in params]
    res = pl.pallas_call(body, grid=(t // tile,), in_specs=in_specs, out_specs=out_specs,
                         out_shape=out_shape, name=name)(*args)
    return res[:len(gidx)], res[len(gidx):]


def halo_fwd(name, fn, tiled, params, outs, tile, halo):
    t = tiled[0][0].shape[0]
    tile = min(tile, t)
    hal = [j for j, x in enumerate(tiled) if x[3]]
    nt, nh, np_ = len(tiled), len(hal), len(params)
    per = tile // halo

    def body(*refs):
        i = pl.program_id(0)
        first = (i > 0).astype(f32)
        tails = [r[...].astype(f32) * first for r in refs[:nh]]
        vals = [r[...].astype(f32) for r in refs[nh:nh + nt + np_]]
        res = fn(*tails, *vals)
        for o_ref, r in zip(refs[nh + nt + np_:], res):
            o_ref[...] = r.astype(o_ref.dtype)

    in_specs = [pl.BlockSpec((halo, tiled[j][1]), lambda i, c=tiled[j][2]: (jnp.maximum(i * per - 1, 0), c))
                for j in hal]
    in_specs += [pl.BlockSpec((tile, w), lambda i, c=c: (i, c)) for _, w, c, _ in tiled]
    in_specs += [_full_spec(p) for p in params]
    out_specs = [pl.BlockSpec((tile, w), lambda i: (i, 0)) for w, _ in outs]
    out_shape = [jax.ShapeDtypeStruct((t, w), dt) for w, dt in outs]
    args = [tiled[j][0] for j in hal] + [x[0] for x in tiled] + list(params)
    return pl.pallas_call(body, grid=(t // tile,), in_specs=in_specs, out_specs=out_specs,
                          out_shape=out_shape, name=name)(*args)


def halo_bwd(name, fn, tiled, params, cots, gdtype, tile, halo):
    t = tiled[0][0].shape[0]
    tile = min(tile, t)
    hal = [j for j, x in enumerate(tiled) if x[3]]
    nt, nh, np_, nc = len(tiled), len(hal), len(params), len(cots)
    per = tile // halo
    n_tiles = t // tile

    def body(*refs):
        s = pl.program_id(0)
        i = n_tiles - 1 - s
        first = (i > 0).astype(f32)
        tails = [r[...].astype(f32) * first for r in refs[:nh]]
        vals = [r[...].astype(f32) for r in refs[nh:nh + nt + np_]]
        cvals = tuple(r[...].astype(f32) for r in refs[nh + nt + np_:nh + nt + np_ + nc])
        n_in = nh + nt + np_ + nc
        orefs = refs[n_in:n_in + nt]
        prefs = refs[n_in + nt:n_in + nt + np_]
        carries = refs[n_in + nt + np_:]

        @pl.when(s == 0)
        def _():
            for r in prefs:
                r[...] = jnp.zeros_like(r)
            for r in carries:
                r[...] = jnp.zeros_like(r)

        _, vjp = jax.vjp(fn, *tails, *vals)
        grads = vjp(cvals)
        for j in range(nt):
            g = grads[nh + j]
            if j in hal:
                cr = carries[hal.index(j)]
                g = jnp.concatenate([g[:tile - halo], g[tile - halo:] + cr[...]], axis=0)
            orefs[j][...] = g.astype(orefs[j].dtype)
        for n in range(nh):
            carries[n][...] = grads[n] * first
        for r, g in zip(prefs, grads[nh + nt:]):
            r[...] += g

    rev = lambda s: n_tiles - 1 - s
    in_specs = [pl.BlockSpec((halo, tiled[j][1]),
                             lambda s, c=tiled[j][2]: (jnp.maximum(rev(s) * per - 1, 0), c)) for j in hal]
    in_specs += [pl.BlockSpec((tile, w), lambda s, c=c: (rev(s), c)) for _, w, c, _ in tiled]
    in_specs += [_full_spec(p) for p in params]
    in_specs += [pl.BlockSpec((tile, w), lambda s, c=c: (rev(s), c)) for _, w, c in cots]
    out_specs = [pl.BlockSpec((tile, w), lambda s: (rev(s), 0)) for _, w, _, _ in tiled]
    out_shape = [jax.ShapeDtypeStruct((t, w), gdtype) for _, w, _, _ in tiled]
    out_specs += [_full_spec(p) for p in params]
    out_shape += [jax.ShapeDtypeStruct(p.shape, f32) for p in params]
    scratch = [pltpu.VMEM((halo, tiled[j][1]), f32) for j in hal]
    args = [tiled[j][0] for j in hal] + [x[0] for x in tiled] + list(params) + [a for a, _, _ in cots]
    res = pl.pallas_call(body, grid=(n_tiles,), in_specs=in_specs, out_specs=out_specs,
                         out_shape=out_shape, scratch_shapes=scratch, name=name)(*args)
    return res[:nt], res[nt:]


def _call_with_carry(name, body, grid, in_specs, out_specs, out_shape, scratch, args, carry):
    if carry is None:
        res = pl.pallas_call(body, grid=grid, in_specs=in_specs, out_specs=out_specs, out_shape=out_shape,
                             scratch_shapes=scratch, name=name)(*args)
        return list(res), []
    kind, arrays = carry
    nc, n_in, n_out, n_scr = len(arrays), len(in_specs), len(out_shape), len(scratch)

    def carried(*refs):
        cut = [n_in, nc, n_out, nc, n_scr]
        parts, pos = [], 0
        for k in cut:
            parts.append(refs[pos:pos + k])
            pos += k
        ins, cins, outs, couts, scr = parts
        start, finish = _carry_parts(kind, cins, couts, refs[pos:])
        ids = [pl.program_id(d) for d in range(len(grid))]
        first, last = ids[0] == 0, ids[0] == grid[0] - 1
        for d in range(1, len(grid)):
            first, last = first & (ids[d] == 0), last & (ids[d] == grid[d] - 1)
        pl.when(first)(start)
        body(*ins, *outs, *scr)
        pl.when(last)(finish)

    anyspec = pl.BlockSpec(memory_space=pl.ANY)
    res = pl.pallas_call(
        carried, grid=grid, in_specs=list(in_specs) + [anyspec] * nc, out_specs=list(out_specs) + [anyspec] * nc,
        out_shape=list(out_shape) + _carry_out_shape(kind, arrays),
        scratch_shapes=list(scratch) + _carry_sems(kind, nc), name=name)(*args, *arrays)
    return list(res[:n_out]), list(res[n_out:])


def scan_fwd(name, fn, tiled, pparams, sparams, out_dtype, hb, n_extra=0, carry=None):
    t = tiled[0][0].shape[0]
    n = t // CHUNK
    cpb = min(SCAN_CHUNKS, n)
    nt, npp, nsp = len(tiled), len(pparams), len(sparams)
    w = HD * hb

    def body(*refs):
        c = pl.program_id(1)
        n_in = nt + npp + nsp
        o_ref, sv_ref = refs[n_in], refs[n_in + 1]
        ex_refs, st = refs[n_in + 2:n_in + 2 + n_extra], refs[n_in + 2 + n_extra]

        @pl.when(c == 0)
        def _():
            st[...] = jnp.zeros_like(st)

        vals = [r[...].astype(f32) for r in refs[:n_in]]
        state = [st[hh] for hh in range(hb)]
        rows_out = []
        for j in range(cpb):
            rows = slice(j * CHUNK, (j + 1) * CHUNK)
            outs = []
            for hh in range(hb):
                sl = slice(hh * HD, (hh + 1) * HD)
                hv = [v[rows, sl] for v in vals[:nt]] + [v[:, sl] for v in vals[nt:nt + npp]] + vals[nt + npp:]
                sv_ref[hh, j] = state[hh]
                res = fn(*hv, state[hh])
                state[hh] = res[1]
                outs.append(res[0])
                for e_ref, e in zip(ex_refs, res[2:]):
                    e_ref[hh, j] = e
            rows_out.append(outs[0] if hb == 1 else jnp.concatenate(outs, axis=1))
        for hh in range(hb):
            st[hh] = state[hh]
        o_ref[...] = (rows_out[0] if cpb == 1 else jnp.concatenate(rows_out, axis=0)).astype(o_ref.dtype)

    in_specs = [pl.BlockSpec((CHUNK * cpb, w), lambda g, c, b=b: (c, b // hb + g)) for _, b in tiled]
    in_specs += [pl.BlockSpec((1, w), lambda g, c: (0, g)) for _ in pparams]
    in_specs += [_full_spec(p) for p in sparams]
    out_specs = [pl.BlockSpec((CHUNK * cpb, w), lambda g, c: (c, g)),
                 pl.BlockSpec((hb, cpb, HD, HD), lambda g, c: (g, c, 0, 0))]
    out_shape = [jax.ShapeDtypeStruct((t, GW), out_dtype), jax.ShapeDtypeStruct((NH, n, HD, HD), f32)]
    out_specs += [pl.BlockSpec((hb, cpb, CHUNK, CHUNK), lambda g, c: (g, c, 0, 0))] * n_extra
    out_shape += [jax.ShapeDtypeStruct((NH, n, CHUNK, CHUNK), f32)] * n_extra
    res, carried = _call_with_carry(name, body, (NH // hb, n // cpb), in_specs, out_specs, out_shape,
                                    [pltpu.VMEM((hb, HD, HD), f32)], [a for a, _ in tiled] + list(pparams) + list(sparams),
                                    carry)
    return res + [carried]


def scan_bwd(name, fn, tiled, pparams, sparams, states, cot, gdtypes, hb, extras=(), carry=None):
    t = tiled[0][0].shape[0]
    n = t // CHUNK
    cpb = min(SCAN_CHUNKS, n)
    nb = n // cpb
    nt, npp, nsp, nex = len(tiled), len(pparams), len(sparams), len(extras)
    w = HD * hb

    def body(*refs):
        g, s = pl.program_id(0), pl.program_id(1)
        n_in = nt + npp + nsp
        vals = [r[...].astype(f32) for r in refs[:n_in]]
        st_ref = refs[n_in]
        do = refs[n_in + 1][...].astype(f32)
        ex_refs = refs[n_in + 2:n_in + 2 + nex]
        n_op = n_in + 2 + nex
        orefs = refs[n_op:n_op + nt]
        pprefs = refs[n_op + nt:n_op + nt + npp]
        sprefs = refs[n_op + nt + npp:n_op + nt + npp + nsp]
        ds = refs[n_op + nt + npp + nsp]

        @pl.when(s == 0)
        def _():
            ds[...] = jnp.zeros_like(ds)
            for r in pprefs:
                r[...] = jnp.zeros_like(r)

        @pl.when((s == 0) & (g == 0))
        def _():
            for r in sprefs:
                r[...] = jnp.zeros_like(r)

        dstate = [ds[hh] for hh in range(hb)]
        by_chunk = [None] * cpb
        for j in reversed(range(cpb)):
            rows = slice(j * CHUNK, (j + 1) * CHUNK)
            per_head = []
            for hh in range(hb):
                sl = slice(hh * HD, (hh + 1) * HD)
                hv = [v[rows, sl] for v in vals[:nt]] + [v[:, sl] for v in vals[nt:nt + npp]] + vals[nt + npp:]
                ex = [r[hh, j] for r in ex_refs]
                _, vjp = jax.vjp(lambda *a: fn(*a, *ex), *hv, st_ref[hh, j])
                grads = vjp((do[rows, sl], dstate[hh]))
                dstate[hh] = grads[n_in]
                per_head.append(grads)
            by_chunk[j] = per_head
        for hh in range(hb):
            ds[hh] = dstate[hh]

        def lanes(j, k):
            return by_chunk[j][0][k] if hb == 1 else jnp.concatenate([gr[k] for gr in by_chunk[j]], axis=1)

        for k in range(nt):
            blk = lanes(0, k) if cpb == 1 else jnp.concatenate([lanes(j, k) for j in range(cpb)], axis=0)
            orefs[k][...] = blk.astype(orefs[k].dtype)
        for k, r in enumerate(pprefs):
            tot = lanes(0, nt + k)
            for j in range(1, cpb):
                tot = tot + lanes(j, nt + k)
            r[...] += tot
        for k, r in enumerate(sprefs):
            tot = None
            for j in range(cpb):
                for gr in by_chunk[j]:
                    tot = gr[nt + npp + k] if tot is None else tot + gr[nt + npp + k]
            r[...] += tot

    rev = lambda s: nb - 1 - s
    in_specs = [pl.BlockSpec((CHUNK * cpb, w), lambda g, s, b=b: (rev(s), b // hb + g)) for _, b in tiled]
    in_specs += [pl.BlockSpec((1, w), lambda g, s: (0, g)) for _ in pparams]
    in_specs += [_full_spec(p) for p in sparams]
    in_specs += [pl.BlockSpec((hb, cpb, HD, HD), lambda g, s: (g, rev(s), 0, 0)),
                 pl.BlockSpec((CHUNK * cpb, w), lambda g, s, b=cot[1]: (rev(s), b // hb + g))]
    in_specs += [pl.BlockSpec((hb, cpb, CHUNK, CHUNK), lambda g, s: (g, rev(s), 0, 0)) for _ in extras]
    out_specs = [pl.BlockSpec((CHUNK * cpb, w), lambda g, s: (rev(s), g)) for _ in tiled]
    out_shape = [jax.ShapeDtypeStruct((t, GW), dt) for dt in gdtypes]
    out_specs += [pl.BlockSpec((1, w), lambda g, s: (0, g)) for _ in pparams]
    out_shape += [jax.ShapeDtypeStruct(p.shape, f32) for p in pparams]
    out_specs += [_full_spec(p) for p in sparams]
    out_shape += [jax.ShapeDtypeStruct(p.shape, f32) for p in sparams]
    res, carried = _call_with_carry(
        name, body, (NH // hb, nb), in_specs, out_specs, out_shape, [pltpu.VMEM((hb, HD, HD), f32)],
        [a for a, _ in tiled] + list(pparams) + list(sparams) + [states, cot[0]] + list(extras), carry)
    return res[:nt], res[nt:nt + npp], res[nt + npp:], carried


def matmul(name, a, b, mode, out_dtype=f32, tm=1024, tn=1024, tk=2048, epilogue=None, extra=None, slots=None,
           b_gathered=None, carry=None):
    if b_gathered is not None:
        cut, layer = b_gathered
        _, _, sr, sc = b.shape
        b_rows, b_cols = (N_DEV * sr, sc) if cut == 'rows' else (sr, N_DEV * sc)
    else:
        b_rows, b_cols = b.shape
    if mode == 'nn':
        (m, k), n = a.shape, b_cols
    elif mode == 'nt':
        (m, k), n = a.shape, b_rows
    else:
        (k, m), n = a.shape, b_cols
    tm, tn, tk = min(tm, m), min(tn, n), min(tk, k)
    if b_gathered is not None:
        if (mode == 'nn') == (cut == 'cols'):
            tn = min(tn, sc if cut == 'cols' else sr)
        else:
            tk = min(tk, sr if cut == 'rows' else sc)
    if slots == 'rows':
        tm = min(tm, m // N_DEV)
    if slots == 'cols':
        tn = min(tn, n // N_DEV)
    nk = k // tk
    ca, cb = {'nn': (1, 0), 'nt': (1, 1), 'tn': (0, 0)}[mode]

    def finish(refs, r):
        if epilogue == 'relu2':
            refs[2][...] = r
            refs[3][...] = jnp.square(jnp.maximum(r, 0.0)).astype(bf16)
        elif epilogue == 'relu2_bwd':
            refs[3][...] = (r * 2.0 * jnp.maximum(refs[2][...], 0.0)).astype(refs[3].dtype)
        else:
            refs[2][...] = r.astype(refs[2].dtype)

    def body(*refs):
        part = _dotb(refs[0][...], refs[1][...], ca, cb)
        if nk == 1:
            finish(refs, part)
            return
        acc = refs[-1]
        kk = pl.program_id(2)

        @pl.when(kk == 0)
        def _():
            acc[...] = part

        @pl.when(kk > 0)
        def _():
            acc[...] += part

        @pl.when(kk == nk - 1)
        def _():
            finish(refs, acc[...])

    if mode == 'nn':
        a_spec = pl.BlockSpec((tm, tk), lambda i, j, kk: (i, kk))
        b_spec = pl.BlockSpec((tk, tn), lambda i, j, kk: (kk, j))
    elif mode == 'nt':
        a_spec = pl.BlockSpec((tm, tk), lambda i, j, kk: (i, kk))
        b_spec = pl.BlockSpec((tn, tk), lambda i, j, kk: (j, kk))
    else:
        a_spec = pl.BlockSpec((tk, tm), lambda i, j, kk: (kk, i))
        b_spec = pl.BlockSpec((tk, tn), lambda i, j, kk: (kk, j))
    if b_gathered is not None:
        bshape = (None, None, tk, tn) if mode == 'nn' else (None, None, tn, tk)
        if mode == 'nn' and cut == 'cols':
            per = sc // tn
            b_spec = pl.BlockSpec(bshape, lambda i, j, kk: (j // per, layer, kk, j % per))
        elif mode == 'nn':
            per = sr // tk
            b_spec = pl.BlockSpec(bshape, lambda i, j, kk: (kk // per, layer, kk % per, j))
        elif cut == 'cols':
            per = sc // tk
            b_spec = pl.BlockSpec(bshape, lambda i, j, kk: (kk // per, layer, j, kk % per))
        else:
            per = sr // tn
            b_spec = pl.BlockSpec(bshape, lambda i, j, kk: (j // per, layer, j % per, kk))
    o_spec = pl.BlockSpec((tm, tn), lambda i, j, kk: (i, j))
    in_specs, args = [a_spec, b_spec], [a, b]
    if epilogue == 'relu2':
        out_specs = [o_spec, o_spec]
        out_shape = [jax.ShapeDtypeStruct((m, n), f32), jax.ShapeDtypeStruct((m, n), bf16)]
    elif slots == 'rows':
        per = (m // N_DEV) // tm
        out_specs = pl.BlockSpec((None, None, tm, tn), lambda i, j, kk: ((i // per) % 2, (i // per) // 2, i % per, j))
        out_shape = jax.ShapeDtypeStruct((2, N_DEV // 2, m // N_DEV, n), out_dtype)
    elif slots == 'cols':
        per = (n // N_DEV) // tn
        out_specs = pl.BlockSpec((None, None, tm, tn), lambda i, j, kk: ((j // per) % 2, (j // per) // 2, i, j % per))
        out_shape = jax.ShapeDtypeStruct((2, N_DEV // 2, m, n // N_DEV), out_dtype)
    else:
        out_specs, out_shape = o_spec, jax.ShapeDtypeStruct((m, n), out_dtype)
        if epilogue == 'relu2_bwd':
            in_specs.append(o_spec)
            args.append(extra)
    scratch = [pltpu.VMEM((tm, tn), f32)] if nk > 1 else []
    if carry is None:
        return pl.pallas_call(body, grid=(m // tm, n // tn, nk), in_specs=in_specs, out_specs=out_specs,
                              out_shape=out_shape, scratch_shapes=scratch, name=name)(*args)
    single = not isinstance(out_shape, list)
    res, carried = _call_with_carry(name, body, (m // tm, n // tn, nk), in_specs, [out_specs] if single else out_specs,
                                    [out_shape] if single else out_shape, scratch, args, carry)
    return (res[0] if single else res), carried


def final_loss(name, x, y, g, target):
    t, d = x.shape
    tile = min(ROW_TILE, t)

    def body(x_ref, y_ref, g_ref, t_ref, dx_ref, l_ref):
        i = pl.program_id(0)

        @pl.when(i == 0)
        def _():
            l_ref[...] = jnp.zeros_like(l_ref)

        err = x_ref[...] + _rms(y_ref[...], g_ref[...]) - t_ref[...]
        dx_ref[...] = err * (1.0 / d)
        l_ref[...] += 0.5 * jnp.sum(jnp.mean(err * err, axis=-1, keepdims=True), axis=0, keepdims=True)

    row = pl.BlockSpec((tile, d), lambda i: (i, 0))
    return pl.pallas_call(
        body, grid=(t // tile,), in_specs=[row, row, _full_spec(g), row],
        out_specs=[row, pl.BlockSpec((1, 1), lambda i: (0, 0))],
        out_shape=[jax.ShapeDtypeStruct((t, d), f32), jax.ShapeDtypeStruct((1, 1), f32)], name=name)(x, y, g, target)


def adamw(name, w, m, v, gslots, tr=128):
    nl, r, c = w.shape
    tr = min(tr, r)
    nr = r // tr
    ns = gslots[0].shape[0]
    c1 = 1.0 / (1.0 - ADAM_B1 ** ADAM_STEP)
    c2 = 1.0 / (1.0 - ADAM_B2 ** ADAM_STEP)

    def body(*refs):
        w_ref, m_ref, v_ref = refs[:3]
        g_refs = refs[3:3 + nl]
        go_ref, d_ref, mo_ref, vo_ref = refs[3 + nl:]
        l = pl.program_id(0)
        g = None
        for li in range(nl):
            s = g_refs[li][0].astype(f32)
            for k in range(1, ns):
                s = s + g_refs[li][k].astype(f32)
            g = s if g is None else jnp.where(l == li, s, g)
        mn = ADAM_B1 * m_ref[...] + (1.0 - ADAM_B1) * g
        vn = ADAM_B2 * v_ref[...] + (1.0 - ADAM_B2) * jnp.square(g)
        go_ref[...] = g
        mo_ref[...] = mn
        vo_ref[...] = vn
        d_ref[...] = -ADAM_LR * ((mn * c1) / (jnp.sqrt(vn * c2) + ADAM_EPS) + ADAM_WD * w_ref[...])

    blk = pl.BlockSpec((None, tr, c), lambda l, i: (l, i, 0))

    def gspec(li):
        return pl.BlockSpec((ns, tr, c), lambda l, i: (0, jnp.where(l == li, i, jnp.where(l < li, 0, nr - 1)), 0))

    return pl.pallas_call(
        body, grid=(nl, nr), in_specs=[blk, blk, blk] + [gspec(li) for li in range(nl)],
        out_specs=[blk] * 4, out_shape=[jax.ShapeDtypeStruct(w.shape, f32)] * 4, name=name)(w, m, v, *gslots)


def exchange(name, arrays, gather, group):
    n = len(arrays)
    ns = {'all': 8, 'chips': 4, 'core': 2}[group]

    def body(*refs):
        ins, outs = refs[:n], refs[n:2 * n]
        send_sems, recv_sems, loc_sems = refs[2 * n:]
        x, y, c = lax.axis_index("x"), lax.axis_index("y"), lax.axis_index("c")

        def member(k):
            if group == 'all':
                px, py, pc = x ^ ((k >> 2) & 1), y ^ ((k >> 1) & 1), c ^ (k & 1)
                return (px, py, pc), 4 * px + 2 * py + pc
            if group == 'chips':
                px, py = x ^ ((k >> 1) & 1), y ^ (k & 1)
                return (px, py, c), 2 * px + py
            return (x, y, c ^ k), c ^ k

        _, me = member(0)
        sends, recvs, locs = [], [], []
        for a in range(n):
            lc = pltpu.make_async_copy(ins[a] if gather else ins[a].at[me], outs[a].at[me], loc_sems.at[a])
            lc.start()
            locs.append(lc)
            for k in range(1, ns):
                dev, peer = member(k)
                src = ins[a] if gather else ins[a].at[peer]
                cp = pltpu.make_async_remote_copy(src_ref=src, dst_ref=outs[a].at[me], send_sem=send_sems.at[a, k],
                                                  recv_sem=recv_sems.at[a, k], device_id=dev, device_id_type=MESH)
                cp.start()
                sends.append(cp)
                recvs.append(pltpu.make_async_remote_copy(src_ref=src, dst_ref=outs[a].at[peer], send_sem=send_sems.at[a, k],
                                                          recv_sem=recv_sems.at[a, k], device_id=dev, device_id_type=MESH))
        for cp in recvs:
            cp.wait_recv()
        for cp in sends:
            cp.wait_send()
        for lc in locs:
            lc.wait()

    anyspec = pl.BlockSpec(memory_space=pl.ANY)
    out_shape = [jax.ShapeDtypeStruct(((ns,) + a.shape) if gather else a.shape, a.dtype) for a in arrays]
    return pl.pallas_call(
        body, in_specs=[anyspec] * n, out_specs=[anyspec] * n, out_shape=out_shape,
        scratch_shapes=[pltpu.SemaphoreType.DMA((n, ns)), pltpu.SemaphoreType.DMA((n, ns)),
                        pltpu.SemaphoreType.DMA((n,))], name=name)(*arrays)


def gather_two_level(name, arrays):
    n = len(arrays)

    def body(*refs):
        start, finish = _gather_parts(refs[:n], refs[n:2 * n], *refs[2 * n:])
        start()
        finish()

    anyspec = pl.BlockSpec(memory_space=pl.ANY)
    return pl.pallas_call(
        body, in_specs=[anyspec] * n, out_specs=[anyspec] * n, out_shape=_carry_out_shape('gather', arrays),
        scratch_shapes=_carry_sems('gather', n), name=name)(*arrays)


def _gather_parts(ins, outs, send_sems, recv_sems, loc_sems):
    n = len(ins)
    x, y, c = lax.axis_index("x"), lax.axis_index("y"), lax.axis_index("c")
    sib = (x, y, 1 - c)

    def chip(k):
        px, py = x ^ ((k >> 1) & 1), y ^ (k & 1)
        return (px, py), 2 * px + py

    _, mine = chip(0)

    def copy(a, sem, src, slot, to):
        return pltpu.make_async_remote_copy(src_ref=src, dst_ref=outs[a].at[slot], send_sem=send_sems.at[a, sem],
                                            recv_sem=recv_sems.at[a, sem], device_id=to, device_id_type=MESH)

    def local(a):
        return pltpu.make_async_copy(ins[a], outs[a].at[2 * mine + c], loc_sems.at[a])

    def own_sends(a):
        cps = [copy(a, 0, ins[a], 2 * mine + c, sib)]
        for k in range(1, 4):
            (px, py), _ = chip(k)
            cps.append(copy(a, k, ins[a], 2 * mine + c, (px, py, c)))
        return cps

    def start():
        for a in range(n):
            local(a).start()
            for cp in own_sends(a):
                cp.start()

    def finish():
        passed = []
        for a in range(n):
            for k in range(1, 4):
                _, other = chip(k)
                slot = 2 * other + c
                copy(a, k, outs[a].at[slot], slot, sib).wait_recv()
                fw = copy(a, 3 + k, outs[a].at[slot], slot, sib)
                fw.start()
                passed.append(fw)
        for a in range(n):
            copy(a, 0, ins[a], 2 * mine + 1 - c, sib).wait_recv()
            for k in range(1, 4):
                _, other = chip(k)
                slot = 2 * other + 1 - c
                copy(a, 3 + k, outs[a].at[slot], slot, sib).wait_recv()
        for a in range(n):
            for cp in own_sends(a):
                cp.wait_send()
        for cp in passed:
            cp.wait_send()
        for a in range(n):
            local(a).wait()

    return start, finish


def _scatter_parts(ins, outs, send_sems, recv_sems, loc_sems):
    n = len(ins)
    x, y, c = lax.axis_index("x"), lax.axis_index("y"), lax.axis_index("c")
    mine = 2 * x + y

    def local(a):
        return pltpu.make_async_copy(ins[a].at[mine], outs[a].at[mine], loc_sems.at[a])

    def remote(a, k, slot):
        px, py = x ^ ((k >> 1) & 1), y ^ (k & 1)
        return pltpu.make_async_remote_copy(src_ref=ins[a].at[2 * px + py], dst_ref=outs[a].at[slot],
                                            send_sem=send_sems.at[a, k], recv_sem=recv_sems.at[a, k],
                                            device_id=(px, py, c), device_id_type=MESH)

    def start():
        for a in range(n):
            local(a).start()
            for k in range(1, 4):
                remote(a, k, mine).start()

    def finish():
        for a in range(n):
            for k in range(1, 4):
                remote(a, k, 2 * (x ^ ((k >> 1) & 1)) + (y ^ (k & 1))).wait_recv()
        for a in range(n):
            for k in range(1, 4):
                remote(a, k, mine).wait_send()
            local(a).wait()

    return start, finish


def _carry_out_shape(kind, arrays):
    if kind == 'gather':
        return [jax.ShapeDtypeStruct((N_DEV,) + a.shape, a.dtype) for a in arrays]
    return [jax.ShapeDtypeStruct(a.shape, a.dtype) for a in arrays]


def _carry_sems(kind, n):
    k = 7 if kind == 'gather' else 4
    return [pltpu.SemaphoreType.DMA((n, k)), pltpu.SemaphoreType.DMA((n, k)), pltpu.SemaphoreType.DMA((n,))]


def _carry_parts(kind, ins, outs, sems):
    return (_gather_parts if kind == 'gather' else _scatter_parts)(ins, outs, *sems)


def chip_partials(tag, slots):
    from_sibling = send_to_sibling(f"to_sibling_{tag}", slots)
    return [chip_sum(f"chip_sum_{tag}{i}", b, o) for i, (b, o) in enumerate(zip(slots, from_sibling))]


def send_to_sibling(name, arrays):
    n = len(arrays)

    def body(*refs):
        ins, outs = refs[:n], refs[n:2 * n]
        send_sems, recv_sems = refs[2 * n:]
        x, y, c = lax.axis_index("x"), lax.axis_index("y"), lax.axis_index("c")
        cps = [pltpu.make_async_remote_copy(src_ref=ins[a].at[1 - c], dst_ref=outs[a], send_sem=send_sems.at[a],
                                            recv_sem=recv_sems.at[a], device_id=(x, y, 1 - c), device_id_type=MESH)
               for a in range(n)]
        for cp in cps:
            cp.start()
        for cp in cps:
            cp.wait()

    anyspec = pl.BlockSpec(memory_space=pl.ANY)
    out_shape = [jax.ShapeDtypeStruct(a.shape[1:], a.dtype) for a in arrays]
    return pl.pallas_call(
        body, in_specs=[anyspec] * n, out_specs=[anyspec] * n, out_shape=out_shape,
        scratch_shapes=[pltpu.SemaphoreType.DMA((n,)), pltpu.SemaphoreType.DMA((n,))], name=name)(*arrays)


def chip_sum(name, both, other):
    _, nc, r, c = both.shape
    tr = ROW_TILE if r % ROW_TILE == 0 else r

    def body(b_ref, o_ref, s_ref):
        core = lax.axis_index("c")
        own = jnp.where(core == 0, b_ref[0], b_ref[1]).astype(f32)
        s_ref[...] = (own + o_ref[...].astype(f32)).astype(s_ref.dtype)

    return pl.pallas_call(
        body, grid=(nc, r // tr),
        in_specs=[pl.BlockSpec((2, None, tr, c), lambda s, i: (0, s, i, 0)),
                  pl.BlockSpec((None, tr, c), lambda s, i: (s, i, 0))],
        out_specs=pl.BlockSpec((None, tr, c), lambda s, i: (s, i, 0)),
        out_shape=jax.ShapeDtypeStruct((nc, r, c), both.dtype), name=name)(both, other)


REPLICATED = ('lower_bounds', 'norm_mix_pre', 'norm_mix_post', 'norm_ff_pre', 'norm_ff_post', 'hgrn_norm_w',
              'gdn_a_log', 'gdn_dt_bias', 'gdn_norm_w', 'gmlp_ln_w', 'gmlp_ln_b', 'gmlp_w_s', 'gmlp_b_s',
              'conv_dw_b', 'conv_ln_w', 'conv_ln_b')
WEIGHTS = ('lower_bounds', 'norm_mix_pre', 'norm_mix_post', 'norm_ff_pre', 'norm_ff_post', 'w_in', 'w_out',
           'hgrn_norm_w', 'gdn_conv_w', 'gdn_a_log', 'gdn_dt_bias', 'gdn_norm_w', 'gmlp_ln_w', 'gmlp_ln_b',
           'gmlp_w_s', 'gmlp_b_s', 'conv_dw_w', 'conv_dw_b', 'conv_ln_w', 'conv_ln_b', 'w_ff1', 'w_ff2')


def _row(v):
    return v.reshape(1, -1)


def _pad_lanes(v, offset):
    return jnp.pad(v, (offset, 128 - offset - v.shape[0])).reshape(1, 128)


def _relayout_w_in(g):
    full = jnp.moveaxis(g[:, 0], 0, 1).reshape(D_MODEL, D_IN)
    return jnp.concatenate([full[:, :8 * GW], full[:, 8 * GW + 2 * NH:], full[:, 8 * GW:8 * GW + 2 * NH],
                            jnp.zeros((D_MODEL, 128 - 2 * NH), bf16)], axis=1)


def _layer_fwd(l, x0, params, lb_all, shards):
    p = params[l]
    sv = {'x0': x0}
    (h,) = rowwise(f"norm_mix_pre{l}", fn_norm, [(x0, D_MODEL, 0)], [p['g_mix_pre']], [(D_MODEL, bf16)], ROW_TILE)
    proj = matmul(f"proj{l}", h, p['w_in'], 'nn', f32, tn=896)
    sv.update(h=h, proj=proj)
    lb = lb_all[l:l + 1]
    o_a, st_a, (p['w_ff1'],) = scan_fwd(f"hgrn{l}", fn_hgrn, [(proj, 0), (proj, 4), (proj, 8), (proj, 12)], [lb],
                                        [p['hgrn_norm_w']], bf16, NH, carry=('gather', [shards['w_ff1', l]]))
    q, k, v, beta, g = halo_fwd(
        f"gdn_pre{l}", fn_gdn_pre,
        [(proj, GW, 4, True), (proj, GW, 5, True), (proj, GW, 6, True), (proj, 128, 48, False)],
        [p['gdn_conv_w'], p['alog'], p['dtb']], [(GW, f32)] * 5, ROW_TILE, 8)
    wanted = [shards['w_ff2', l]] + ([shards['w_out']] if l == 0 else []) + ([shards['w_in', l + 1]] if l + 1 < DEPTH else [])
    o_b, st_b, tinv_b, got = scan_fwd(f"gdn{l}", fn_gdn, [(q, 0), (k, 0), (v, 0), (beta, 0), (g, 0), (proj, 28)], [],
                                      [p['gdn_norm_w']], bf16, NH, n_extra=1, carry=('gather', wanted))
    p['w_ff2'] = got.pop(0)
    if l == 0:
        w_out_full = jnp.moveaxis(got.pop(0), 0, 1).reshape(DEPTH, D_MODEL, D_MODEL)
        for ll in range(DEPTH):
            params[ll]['w_out'] = w_out_full[ll]
    if l + 1 < DEPTH:
        params[l + 1]['w_in'] = _relayout_w_in(got.pop(0))
    (o_c,) = rowwise(f"gmlp{l}", fn_gmlp, [(proj, GW, 8), (proj, GW, 9)],
                     [p['gmlp_ln_w'], p['gmlp_ln_b'], p['gmlp_w_s'], p['gmlp_b_s']], [(GW, bf16)], MIX_CHUNK)
    (o_d,) = halo_fwd(f"conv{l}", fn_conv, [(proj, GW, 10, True), (proj, GW, 11, True)],
                      [p['conv_dw_w'], p['conv_dw_b'], p['conv_ln_w'], p['conv_ln_b']], [(GW, bf16)], ROW_TILE, 32)
    mix = jnp.concatenate([o_a, o_b, o_c, o_d], axis=1)
    y1 = matmul(f"out_proj{l}", mix, p['w_out'], 'nn', f32)
    (x1,) = rowwise(f"res_mix{l}", lambda x, y, gg: (x + _rms(y, gg),), [(x0, D_MODEL, 0), (y1, D_MODEL, 0)],
                    [p['g_mix_post']], [(D_MODEL, f32)], ROW_TILE)
    (h2,) = rowwise(f"norm_ff_pre{l}", fn_norm, [(x1, D_MODEL, 0)], [p['g_ff_pre']], [(D_MODEL, bf16)], ROW_TILE)
    u, a = matmul(f"ff1_{l}", h2, p['w_ff1'], 'nn', epilogue='relu2', b_gathered=('cols', 0))
    y2 = matmul(f"ff2_{l}", a, p['w_ff2'], 'nn', f32, b_gathered=('rows', 0))
    sv.update(st_a=st_a, q=q, k=k, v=v, beta=beta, g=g, st_b=st_b, tinv_b=tinv_b, mix=mix, y1=y1, x1=x1, h2=h2, u=u, a=a, y2=y2)
    return sv


def _slots_w_in(gt):
    gl = jnp.concatenate([gt[:8 * GW], gt[12 * GW:12 * GW + 2 * NH], gt[8 * GW:12 * GW]], axis=0)
    return jnp.transpose(gl.reshape(N_DEV // 2, 2, D_IN // N_DEV, D_MODEL), (1, 0, 2, 3))


def sum_slots(name, slots):
    ns, r, c = slots.shape
    tc = 512

    def body(s_ref, o_ref):
        tot = s_ref[0].astype(f32)
        for k in range(1, ns):
            tot = tot + s_ref[k].astype(f32)
        o_ref[...] = tot

    return pl.pallas_call(
        body, grid=(c // tc,), in_specs=[pl.BlockSpec((ns, r, tc), lambda j: (0, 0, j))],
        out_specs=pl.BlockSpec((r, tc), lambda j: (0, j)), out_shape=jax.ShapeDtypeStruct((r, c), f32), name=name)(slots)


def _layer_bwd(l, dx, sv, p, lb_all, received):
    gr = {}
    (dy2,), (gr['norm_ff_post'],) = rowwise_bwd(f"res_ff_bwd{l}", fn_norm, [(sv['y2'], D_MODEL, 0)], [p['g_ff_post']],
                                                [(dx, D_MODEL, 0)], [bf16], ROW_TILE)
    du = matmul(f"ff2_dx{l}", dy2, p['w_ff2'], 'nt', bf16, epilogue='relu2_bwd', extra=sv['u'], b_gathered=('rows', 0))
    g_ff2 = matmul(f"ff2_dw{l}", sv['a'], dy2, 'tn', bf16, slots='rows')
    dh2 = matmul(f"ff1_dx{l}", du, p['w_ff1'], 'nt', f32, b_gathered=('cols', 0))
    g_ff1 = matmul(f"ff1_dw{l}", sv['h2'], du, 'tn', bf16, slots='cols')
    (dx1,), (gr['norm_ff_pre'],) = rowwise_bwd(f"norm_ff_pre_bwd{l}", fn_norm, [(sv['x1'], D_MODEL, 0)], [p['g_ff_pre']],
                                               [(dh2, D_MODEL, 0)], [f32], ROW_TILE, addto=(dx, D_MODEL, 0))
    (dy1,), (gr['norm_mix_post'],) = rowwise_bwd(f"res_mix_bwd{l}", fn_norm, [(sv['y1'], D_MODEL, 0)], [p['g_mix_post']],
                                                 [(dx1, D_MODEL, 0)], [bf16], ROW_TILE)
    dmix = matmul(f"out_proj_dx{l}", dy1, p['w_out'], 'nt', f32)
    g_out = matmul(f"out_proj_dw{l}", sv['mix'], dy1, 'tn', bf16, slots='rows')
    part_ff2, part_ff1, part_out = chip_partials(f"l{l}_", [g_ff2, g_ff1, g_out])
    proj = sv['proj']
    lb = lb_all[l:l + 1]
    going = [(('w_ff2', l), part_ff2)]
    d_a, (dlb,), (gr['hgrn_norm_w'],), got = scan_bwd(
        f"hgrn_bwd{l}", fn_hgrn, [(proj, 0), (proj, 4), (proj, 8), (proj, 12)], [lb], [p['hgrn_norm_w']],
        sv['st_a'], (dmix, 0), [bf16] * 4, NH, carry=('scatter', [a for _, a in going]))
    received.update({key: r for (key, _), r in zip(going, got)})
    going = [(('w_ff1', l), part_ff1), (('w_out', l), part_out)]
    d_b, _, (gr['gdn_norm_w'],), got = scan_bwd(
        f"gdn_bwd{l}", fn_gdn, [(sv['q'], 0), (sv['k'], 0), (sv['v'], 0), (sv['beta'], 0), (sv['g'], 0), (proj, 28)],
        [], [p['gdn_norm_w']], sv['st_b'], (dmix, 4), [f32] * 5 + [bf16], NH, extras=[sv['tinv_b']],
        carry=('scatter', [a for _, a in going]))
    received.update({key: r for (key, _), r in zip(going, got)})
    d_bp, (gr['gdn_conv_w'], dalog, ddtb) = halo_bwd(
        f"gdn_pre_bwd{l}", fn_gdn_pre,
        [(proj, GW, 4, True), (proj, GW, 5, True), (proj, GW, 6, True), (proj, 128, 48, False)],
        [p['gdn_conv_w'], p['alog'], p['dtb']], [(d_b[j], GW, 0) for j in range(5)], bf16, ROW_TILE, 8)
    gr['gdn_a_log'] = dalog[0, NH:2 * NH]
    gr['gdn_dt_bias'] = ddtb[0, NH:2 * NH]
    d_c, (gr['gmlp_ln_w'], gr['gmlp_ln_b'], gr['gmlp_w_s'], gr['gmlp_b_s']) = rowwise_bwd(
        f"gmlp_bwd{l}", fn_gmlp, [(proj, GW, 8), (proj, GW, 9)],
        [p['gmlp_ln_w'], p['gmlp_ln_b'], p['gmlp_w_s'], p['gmlp_b_s']], [(dmix, GW, 2)], [bf16, bf16], MIX_CHUNK)
    d_d, (gr['conv_dw_w'], gr['conv_dw_b'], gr['conv_ln_w'], gr['conv_ln_b']) = halo_bwd(
        f"conv_bwd{l}", fn_conv, [(proj, GW, 10, True), (proj, GW, 11, True)],
        [p['conv_dw_w'], p['conv_dw_b'], p['conv_ln_w'], p['conv_ln_b']], [(dmix, GW, 3)], bf16, ROW_TILE, 32)
    dproj = jnp.concatenate(list(d_a) + [d_bp[0], d_bp[1], d_bp[2], d_b[5]] + list(d_c) + list(d_d) + [d_bp[3]], axis=1)
    g_in_t = matmul(f"proj_dw{l}", dproj, sv['h'], 'tn', bf16, tm=896)
    (part_in,) = chip_partials(f"l{l}_in", [_slots_w_in(g_in_t)])
    dh, (received['w_in', l],) = matmul(f"proj_dx{l}", dproj, p['w_in'], 'nt', f32, tm=512, tn=512, tk=D_IN_PAD,
                                        carry=('scatter', [part_in]))
    (dx0,), (gr['norm_mix_pre'],) = rowwise_bwd(f"norm_mix_pre_bwd{l}", fn_norm, [(sv['x0'], D_MODEL, 0)], [p['g_mix_pre']],
                                                [(dh, D_MODEL, 0)], [f32], ROW_TILE, addto=(dx1, D_MODEL, 0))
    return dx0, gr, dlb


def kernel(x, lower_bounds, norm_mix_pre, norm_mix_post, norm_ff_pre, norm_ff_post, w_in, w_out, hgrn_norm_w, gdn_conv_w, gdn_a_log, gdn_dt_bias, gdn_norm_w, gmlp_ln_w, gmlp_ln_b, gmlp_w_s, gmlp_b_s, conv_dw_w, conv_dw_b, conv_ln_w, conv_ln_b, w_ff1, w_ff2, loss_target, m_lower_bounds, m_norm_mix_pre, m_norm_mix_post, m_norm_ff_pre, m_norm_ff_post, m_w_in, m_w_out, m_hgrn_norm_w, m_gdn_conv_w, m_gdn_a_log, m_gdn_dt_bias, m_gdn_norm_w, m_gmlp_ln_w, m_gmlp_ln_b, m_gmlp_w_s, m_gmlp_b_s, m_conv_dw_w, m_conv_dw_b, m_conv_ln_w, m_conv_ln_b, m_w_ff1, m_w_ff2, v_lower_bounds, v_norm_mix_pre, v_norm_mix_post, v_norm_ff_pre, v_norm_ff_post, v_w_in, v_w_out, v_hgrn_norm_w, v_gdn_conv_w, v_gdn_a_log, v_gdn_dt_bias, v_gdn_norm_w, v_gmlp_ln_w, v_gmlp_ln_b, v_gmlp_w_s, v_gmlp_b_s, v_conv_dw_w, v_conv_dw_b, v_conv_ln_w, v_conv_ln_b, v_w_ff1, v_w_ff2):
    loc = dict(locals())
    W = {n: loc[n] for n in WEIGHTS}
    M = {n: loc['m_' + n] for n in WEIGHTS}
    V = {n: loc['v_' + n] for n in WEIGHTS}
    t = x.shape[1]
    me = 4 * lax.axis_index("x") + 2 * lax.axis_index("y") + lax.axis_index("c")

    shards = {'w_out': w_out.astype(bf16)}
    for l in range(DEPTH):
        shards['w_in', l] = w_in[l:l + 1].astype(bf16)
        shards['w_ff1', l] = w_ff1[l:l + 1].astype(bf16)
        shards['w_ff2', l] = w_ff2[l:l + 1].astype(bf16)
    g_in0, g_gconv, g_dconv = gather_two_level("gather_weights", [shards['w_in', 0], gdn_conv_w, conv_dw_w])
    gconv_full = jnp.moveaxis(g_gconv, 0, 2).reshape(DEPTH, SHORT_CONV, 3 * GW)
    dconv_full = jnp.moveaxis(g_dconv, 0, 2).reshape(DEPTH, CONV_WIDTH, GW)

    (lb_all,) = rowwise("lower_bounds", fn_lb, [(lower_bounds, GW, 0)], [], [(GW, f32)], DEPTH)

    P = []
    for l in range(DEPTH):
        P.append(dict(
            g_mix_pre=_row(norm_mix_pre[l]), g_mix_post=_row(norm_mix_post[l]), g_ff_pre=_row(norm_ff_pre[l]),
            g_ff_post=_row(norm_ff_post[l]), hgrn_norm_w=_row(hgrn_norm_w[l]), gdn_conv_w=gconv_full[l],
            alog=_pad_lanes(gdn_a_log[l], NH), dtb=_pad_lanes(gdn_dt_bias[l], NH), gdn_norm_w=_row(gdn_norm_w[l]),
            gmlp_ln_w=_row(gmlp_ln_w[l]), gmlp_ln_b=_row(gmlp_ln_b[l]), gmlp_w_s=gmlp_w_s[l].reshape(NH * MIX_CHUNK, MIX_CHUNK),
            gmlp_b_s=gmlp_b_s[l], conv_dw_w=dconv_full[l], conv_dw_b=_row(conv_dw_b[l]), conv_ln_w=_row(conv_ln_w[l]),
            conv_ln_b=_row(conv_ln_b[l])))

    P[0]['w_in'] = _relayout_w_in(g_in0)
    xs = x[0]
    saved = []
    for l in range(DEPTH):
        sv = _layer_fwd(l, xs, P, lb_all, shards)
        saved.append(sv)
        if l < DEPTH - 1:
            (xs,) = rowwise(f"res_ff{l}", lambda a, y, gg: (a + _rms(y, gg),), [(sv['x1'], D_MODEL, 0), (sv['y2'], D_MODEL, 0)],
                            [P[l]['g_ff_post']], [(D_MODEL, f32)], ROW_TILE)
    sv = saved[-1]
    dx, loss_loc = final_loss("final_loss", sv['x1'], sv['y2'], P[-1]['g_ff_post'], loss_target[0])
    loss = lax.psum(loss_loc[0, 0], ("x", "y", "c"))

    G = {}
    dlb_rows = []
    received = {}
    for l in reversed(range(DEPTH)):
        dx, gr, dlb = _layer_bwd(l, dx, saved[l], P[l], lb_all, received)
        G[l] = gr
        dlb_rows.append(dlb)
    dlb_all = jnp.concatenate(dlb_rows[::-1], axis=0)
    (g_lower_bounds,), _ = rowwise_bwd("lower_bounds_bwd", fn_lb, [(lower_bounds, GW, 0)], [], [(dlb_all, GW, 0)],
                                       [f32], DEPTH)
    grad_x = dx[None]

    def stack(name, f=lambda a: a):
        return jnp.stack([f(G[l][name]) for l in range(DEPTH)], axis=0)

    full = {
        'lower_bounds': g_lower_bounds,
        'norm_mix_pre': stack('norm_mix_pre', lambda a: a[0]), 'norm_mix_post': stack('norm_mix_post', lambda a: a[0]),
        'norm_ff_pre': stack('norm_ff_pre', lambda a: a[0]), 'norm_ff_post': stack('norm_ff_post', lambda a: a[0]),
        'hgrn_norm_w': stack('hgrn_norm_w', lambda a: a[0]), 'gdn_a_log': stack('gdn_a_log'), 'gdn_dt_bias': stack('gdn_dt_bias'),
        'gdn_norm_w': stack('gdn_norm_w', lambda a: a[0]), 'gmlp_ln_w': stack('gmlp_ln_w', lambda a: a[0]),
        'gmlp_ln_b': stack('gmlp_ln_b', lambda a: a[0]),
        'gmlp_w_s': stack('gmlp_w_s', lambda a: a.reshape(NH, MIX_CHUNK, MIX_CHUNK)), 'gmlp_b_s': stack('gmlp_b_s'),
        'conv_dw_b': stack('conv_dw_b', lambda a: a[0]), 'conv_ln_w': stack('conv_ln_w', lambda a: a[0]),
        'conv_ln_b': stack('conv_ln_b', lambda a: a[0]),
        'gdn_conv_w': stack('gdn_conv_w'), 'conv_dw_w': stack('conv_dw_w'),
    }

    small_names = list(REPLICATED) + ['gdn_conv_w', 'conv_dw_w']
    flat = jnp.concatenate([full[n].reshape(-1) for n in small_names])
    n_small = flat.shape[0]
    n_pad = -(-n_small // 1024) * 1024
    packed = jnp.pad(flat, (0, n_pad - n_small)).reshape(n_pad // 128, 128)

    (small_slots,) = exchange("gather_small_grads", [packed], True, 'all')

    out = {}
    for l in range(DEPTH):
        received['w_in', l] = sum_slots(f"sum_w_in{l}", received['w_in', l]).T[None]
    for name in ('w_in', 'w_out', 'w_ff1', 'w_ff2'):
        out[name] = adamw(f"adamw_{name}", W[name], M[name], V[name], [received[name, l] for l in range(DEPTH)])

    def pack(d, fill):
        parts = [d[n].reshape(-1) for n in REPLICATED]
        parts.append(jnp.full((n_pad - sum(a.shape[0] for a in parts),), fill, f32))
        return jnp.concatenate(parts).reshape(1, n_pad // 128, 128)

    sm = adamw("adamw_small", pack(W, 0.0), pack(M, 0.0), pack(V, 1.0), [small_slots], tr=n_pad // 128)
    off = 0
    for n in REPLICATED:
        sz = W[n].size
        out[n] = tuple(a.reshape(-1)[off:off + sz].reshape(W[n].shape) for a in sm)
        off += sz
    gsum = sm[0].reshape(-1)
    for n, full_shape in (('gdn_conv_w', (DEPTH, SHORT_CONV, 3 * GW)), ('conv_dw_w', (DEPTH, CONV_WIDTH, GW))):
        sz = math.prod(full_shape)
        gfull = gsum[off:off + sz].reshape(full_shape)
        off += sz
        sh = W[n].shape
        gmine = lax.dynamic_slice_in_dim(gfull, me * sh[2], sh[2], axis=2)
        r = adamw(f"adamw_{n}", W[n].reshape(1, sh[0] * sh[1], sh[2]), M[n].reshape(1, sh[0] * sh[1], sh[2]),
                  V[n].reshape(1, sh[0] * sh[1], sh[2]), [gmine.reshape(1, sh[0] * sh[1], sh[2])], tr=sh[0] * sh[1])
        out[n] = tuple(a.reshape(sh) for a in r)

    return (loss, grad_x, *[out[n][0] for n in WEIGHTS], *[out[n][1] for n in WEIGHTS],
            *[out[n][2] for n in WEIGHTS], *[out[n][3] for n in WEIGHTS])
```

```python
import functools
import math

import jax
import jax.numpy as jnp
from jax import lax
from jax.experimental import pallas as pl
from jax.experimental.pallas import tpu as pltpu

f32 = jnp.float32
bf16 = jnp.bfloat16
HI = lax.Precision.HIGHEST

N_DEV = 8
DEPTH = 2
D_MODEL = 2048
GW = 512
HD = 128
NH = 4
CHUNK = 64
MIX_CHUNK = 128
CONV_WIDTH = 31
SHORT_CONV = 4
D_FF = 4 * D_MODEL
D_IN = 12 * GW + 2 * NH
D_IN_PAD = 12 * GW + 128
ROW_TILE = 256
HGRN_SUB = 16
SCAN_CHUNKS = 2
EPS = 1e-6
TINY = 1e-30
ADAM_LR, ADAM_B1, ADAM_B2, ADAM_EPS, ADAM_WD, ADAM_STEP = 0.001, 0.9, 0.999, 1e-08, 0.01, 10
MESH = pl.DeviceIdType.MESH


def _dotb(a, b, ca, cb):
    return lax.dot_general(a.astype(bf16), b.astype(bf16), (((ca,), (cb,)), ((), ())),
                           preferred_element_type=f32)


@jax.custom_vjp
def mm(a, b):
    return _dotb(a, b, 1, 0)


def _mm_f(a, b):
    return mm(a, b), (a, b)


def _mm_b(res, ct):
    a, b = res
    return _dotb(ct, b, 1, 1), _dotb(a, ct, 0, 0)


mm.defvjp(_mm_f, _mm_b)


@jax.custom_vjp
def mm_nt(a, b):
    return _dotb(a, b, 1, 1)


def _mmnt_f(a, b):
    return mm_nt(a, b), (a, b)


def _mmnt_b(res, ct):
    a, b = res
    return _dotb(ct, b, 1, 0), _dotb(ct, a, 0, 0)


mm_nt.defvjp(_mmnt_f, _mmnt_b)


@jax.custom_vjp
def mm_tn(a, b):
    return _dotb(a, b, 0, 0)


def _mmtn_f(a, b):
    return mm_tn(a, b), (a, b)


def _mmtn_b(res, ct):
    a, b = res
    return _dotb(b, ct, 1, 1), _dotb(a, ct, 1, 0)


mm_tn.defvjp(_mmtn_f, _mmtn_b)


def mmh(a, b):
    return jnp.dot(a, b, precision=HI, preferred_element_type=f32)


def mm3(a, b):
    return jnp.dot(a, b, precision=lax.Precision.HIGH, preferred_element_type=f32)


def _rms(x, w):
    return x * lax.rsqrt(jnp.mean(x * x, axis=-1, keepdims=True) + EPS) * w


def _ln(x, w, b):
    mu = jnp.mean(x, axis=-1, keepdims=True)
    xc = x - mu
    var = jnp.mean(xc * xc, axis=-1, keepdims=True)
    return xc * lax.rsqrt(var + EPS) * w + b


def _gelu(x):
    return 0.5 * x * (1.0 + lax.erf(x * (2.0 ** -0.5)))


def _iota2(n, m, axis):
    return lax.broadcasted_iota(jnp.int32, (n, m), axis)


def _tri(n, strict=False):
    r, c = _iota2(n, n, 0), _iota2(n, n, 1)
    return (r > c) if strict else (r >= c)


def _eye(n):
    return (_iota2(n, n, 0) == _iota2(n, n, 1)).astype(f32)


def fn_norm(x, g):
    return (_rms(x, g),)


def fn_lb(lower_bounds):
    s = jax.nn.softmax(lower_bounds, axis=0)
    rows, cum = [], None
    for i in range(DEPTH):
        cum = s[i:i + 1] if cum is None else cum + s[i:i + 1]
        rows.append(cum - s[0:1])
    return (jnp.concatenate(rows, axis=0),)


def fn_hgrn(aq, af, ai, ag, lb, nw, st):
    c = aq.shape[0]
    sig = jax.nn.sigmoid(af)
    f = lb + (1.0 - lb) * sig
    logf = jnp.log(jnp.maximum(f, TINY))
    k = (1.0 - lb) * jax.nn.sigmoid(-af)
    q = jax.nn.silu(aq)
    v = ai
    b = mm3(_tri(c).astype(f32), logf)
    outs = []
    for lo in range(0, c, HGRN_SUB):
        qi, ki, vi, bi = (a[lo:lo + HGRN_SUB] for a in (q, k, v, b))
        rel = bi[:, None, :] - bi[None, :, :]
        dec = jnp.exp(jnp.minimum(rel, 0.0)) * qi[:, None, :] * ki[None, :, :]
        o_blk = mm(jnp.where(_tri(HGRN_SUB), jnp.sum(dec, axis=-1), 0.0), vi)
        if lo > 0:
            r = b[lo - 1:lo, :]
            o_blk = o_blk + mm(mm_nt(qi * jnp.exp(bi - r), k[:lo] * jnp.exp(r - b[:lo])), v[:lo])
        outs.append(o_blk)
    b_end = b[c - 1:c, :]
    out = jnp.concatenate(outs, axis=0) + mm_nt(q * jnp.exp(b), st)
    st_new = st * jnp.exp(b_end) + mm_tn(v, k * jnp.exp(b_end - b))
    o = _rms(out, nw) * jax.nn.silu(ag)
    return o, st_new


def fn_gdn_pre(tq, tk, tv, bq, bk, bv, p8, conv_w, alog, dtb):
    tile = bq.shape[0]
    h = tq.shape[0]
    outs = []
    for seg, (tl, cur) in enumerate(((tq, bq), (tk, bk), (tv, bv))):
        xe = jnp.concatenate([tl, cur], axis=0)
        acc = None
        for kk in range(SHORT_CONV):
            off = h - (SHORT_CONV - 1) + kk
            term = conv_w[kk:kk + 1, seg * GW:(seg + 1) * GW] * xe[off:off + tile, :]
            acc = term if acc is None else acc + term
        outs.append(jax.nn.silu(acc))
    sq, sk, sv = outs
    qh, kh = [], []
    for hh in range(NH):
        a = sq[:, hh * HD:(hh + 1) * HD]
        qh.append(a * lax.rsqrt(jnp.sum(a * a, axis=-1, keepdims=True) + EPS) * (HD ** -0.5))
        a = sk[:, hh * HD:(hh + 1) * HD]
        kh.append(a * lax.rsqrt(jnp.sum(a * a, axis=-1, keepdims=True) + EPS))
    q = jnp.concatenate(qh, axis=1)
    k = jnp.concatenate(kh, axis=1)
    beta = jax.nn.sigmoid(p8)
    g = -jnp.exp(alog) * jax.nn.softplus(p8 + dtb)
    r, cc = _iota2(128, GW, 0), _iota2(128, GW, 1) // HD
    e_beta = (r == cc).astype(f32)
    e_g = (r == cc + NH).astype(f32)
    return q, k, sv, mmh(beta, e_beta), mmh(g, e_g)


@jax.custom_vjp
def _inverse_given(m, tinv):
    return tinv


def _inverse_given_f(m, tinv):
    return tinv, tinv


def _inverse_given_b(tinv, ct):
    x = lax.dot_general(ct, tinv, (((1,), (1,)), ((), ())), precision=lax.Precision.HIGH, preferred_element_type=f32)
    dm = -lax.dot_general(tinv, x, (((0,), (0,)), ((), ())), precision=lax.Precision.HIGH, preferred_element_type=f32)
    return dm, jnp.zeros_like(tinv)


_inverse_given.defvjp(_inverse_given_f, _inverse_given_b)


def fn_gdn(q, k, v, beta, g, z, nw, s, tinv_saved=None):
    c = q.shape[0]
    gc = mm3(_tri(c).astype(f32), g)
    gcol = gc[:, 0:1]
    grow = jnp.sum(gcol * _eye(c), axis=0, keepdims=True)
    gamma = jnp.where(_tri(c), jnp.exp(jnp.minimum(gcol - grow, 0.0)), 0.0)
    kb = k * beta
    m = jnp.where(_tri(c, strict=True), mm_nt(kb, k) * gamma, 0.0)
    if tinv_saved is None:
        eye = _eye(c)
        tinv = eye - m
        p = m
        for _ in range(int(math.log2(c)) - 1):
            p = mm3(p, p)
            tinv = mm3(tinv, eye + p)
    else:
        tinv = _inverse_given(m, tinv_saved)
    egc = jnp.exp(gc)
    u = mm3(tinv, v * beta)
    w = mm3(tinv, kb * egc)
    qk = mm_nt(q, k) * gamma
    gc_end = gc[c - 1:c, :]
    q_dec = q * egc
    k_dec = k * jnp.exp(gc_end - gc)
    v_new = u - mm(w, s)
    out = mm(q_dec, s) + mm(qk, v_new)
    s_new = s * jnp.exp(gc_end) + mm_tn(k_dec, v_new)
    o = _rms(out, nw) * jax.nn.silu(z)
    return (o, s_new, tinv) if tinv_saved is None else (o, s_new)


def fn_gmlp(cu, cv, ln_w, ln_b, w_s, b_s):
    n = cu.shape[0]
    ug = _gelu(cu)
    vn = _ln(_gelu(cv), ln_w, ln_b)
    eye = _eye(n)
    cols = []
    for hh in range(NH):
        wc = jnp.where(_tri(n), w_s[hh * n:(hh + 1) * n, :], 0.0)
        bcol = jnp.sum(b_s[hh:hh + 1, :] * eye, axis=1, keepdims=True)
        cols.append(mm(wc, vn[:, hh * HD:(hh + 1) * HD]) + bcol)
    return (ug * jnp.concatenate(cols, axis=1),)


def fn_conv(ta, tg, a, gate, dw_w, dw_b, ln_w, ln_b):
    tile = a.shape[0]
    h = ta.shape[0]
    ya = jnp.concatenate([ta, a], axis=0)
    yg = jnp.concatenate([tg, gate], axis=0)
    y = ya * jax.nn.sigmoid(yg)
    acc = None
    for kk in range(CONV_WIDTH):
        off = h - (CONV_WIDTH - 1) + kk
        term = dw_w[kk:kk + 1, :] * y[off:off + tile, :]
        acc = term if acc is None else acc + term
    return (jax.nn.silu(_ln(acc + dw_b, ln_w, ln_b)),)


def _full_spec(arr):
    nd = arr.ndim
    return pl.BlockSpec(arr.shape, lambda *_: (0,) * nd)


def rowwise(name, fn, tiled, params, outs, tile):
    t = tiled[0][0].shape[0]
    tile = min(tile, t)
    nt, np_ = len(tiled), len(params)

    def body(*refs):
        vals = [r[...].astype(f32) for r in refs[:nt + np_]]
        res = fn(*vals)
        for o_ref, r in zip(refs[nt + np_:], res):
            o_ref[...] = r.astype(o_ref.dtype)

    in_specs = [pl.BlockSpec((tile, w), lambda i, c=c: (i, c)) for _, w, c in tiled]
    in_specs += [_full_spec(p) for p in params]
    out_specs = [pl.BlockSpec((tile, w), lambda i: (i, 0)) for w, _ in outs]
    out_shape = [jax.ShapeDtypeStruct((t, w), dt) for w, dt in outs]
    return pl.pallas_call(body, grid=(t // tile,), in_specs=in_specs, out_specs=out_specs,
                          out_shape=out_shape, name=name)(*[a for a, _, _ in tiled], *params)


def rowwise_bwd(name, fn, tiled, params, cots, gouts, tile, addto=None):
    t = tiled[0][0].shape[0]
    tile = min(tile, t)
    nt, np_, nc = len(tiled), len(params), len(cots)
    na = 0 if addto is None else 1
    gidx = [i for i, g in enumerate(gouts) if g is not None]

    def body(*refs):
        i = pl.program_id(0)
        vals = [r[...].astype(f32) for r in refs[:nt + np_]]
        cvals = tuple(r[...].astype(f32) for r in refs[nt + np_:nt + np_ + nc])
        _, vjp = jax.vjp(fn, *vals)
        grads = vjp(cvals)
        orefs = refs[nt + np_ + nc + na:]
        for n, j in enumerate(gidx):
            g = grads[j]
            if na and n == 0:
                g = g + refs[nt + np_ + nc][...].astype(f32)
            orefs[n][...] = g.astype(orefs[n].dtype)
        prefs = orefs[len(gidx):]

        @pl.when(i == 0)
        def _():
            for r in prefs:
                r[...] = jnp.zeros_like(r)

        for r, g in zip(prefs, grads[nt:]):
            r[...] += g

    in_specs = [pl.BlockSpec((tile, w), lambda i, c=c: (i, c)) for _, w, c in tiled]
    in_specs += [_full_spec(p) for p in params]
    in_specs += [pl.BlockSpec((tile, w), lambda i, c=c: (i, c)) for _, w, c in cots]
    args = [a for a, _, _ in tiled] + list(params) + [a for a, _, _ in cots]
    if na:
        in_specs.append(pl.BlockSpec((tile, addto[1]), lambda i, c=addto[2]: (i, c)))
        args.append(addto[0])
    out_specs = [pl.BlockSpec((tile, tiled[j][1]), lambda i: (i, 0)) for j in gidx]
    out_shape = [jax.ShapeDtypeStruct((t, tiled[j][1]), gouts[j]) for j in gidx]
    out_specs += [_full_spec(p) for p in params]
    out_shape += [jax.ShapeDtypeStruct(p.shape, f32) for p in params]
    res = pl.pallas_call(body, grid=(t // tile,), in_specs=in_specs, out_specs=out_specs,
                         out_shape=out_shape, name=name)(*args)
    return res[:len(gidx)], res[len(gidx):]


def halo_fwd(name, fn, tiled, params, outs, tile, halo):
    t = tiled[0][0].shape[0]
    tile = min(tile, t)
    hal = [j for j, x in enumerate(tiled) if x[3]]
    nt, nh, np_ = len(tiled), len(hal), len(params)
    per = tile // halo

    def body(*refs):
        i = pl.program_id(0)
        first = (i > 0).astype(f32)
        tails = [r[...].astype(f32) * first for r in refs[:nh]]
        vals = [r[...].astype(f32) for r in refs[nh:nh + nt + np_]]
        res = fn(*tails, *vals)
        for o_ref, r in zip(refs[nh + nt + np_:], res):
            o_ref[...] = r.astype(o_ref.dtype)

    in_specs = [pl.BlockSpec((halo, tiled[j][1]), lambda i, c=tiled[j][2]: (jnp.maximum(i * per - 1, 0), c))
                for j in hal]
    in_specs += [pl.BlockSpec((tile, w), lambda i, c=c: (i, c)) for _, w, c, _ in tiled]
    in_specs += [_full_spec(p) for p in params]
    out_specs = [pl.BlockSpec((tile, w), lambda i: (i, 0)) for w, _ in outs]
    out_shape = [jax.ShapeDtypeStruct((t, w), dt) for w, dt in outs]
    args = [tiled[j][0] for j in hal] + [x[0] for x in tiled] + list(params)
    return pl.pallas_call(body, grid=(t // tile,), in_specs=in_specs, out_specs=out_specs,
                          out_shape=out_shape, name=name)(*args)


def halo_bwd(name, fn, tiled, params, cots, gdtype, tile, halo):
    t = tiled[0][0].shape[0]
    tile = min(tile, t)
    hal = [j for j, x in enumerate(tiled) if x[3]]
    nt, nh, np_, nc = len(tiled), len(hal), len(params), len(cots)
    per = tile // halo
    n_tiles = t // tile

    def body(*refs):
        s = pl.program_id(0)
        i = n_tiles - 1 - s
        first = (i > 0).astype(f32)
        tails = [r[...].astype(f32) * first for r in refs[:nh]]
        vals = [r[...].astype(f32) for r in refs[nh:nh + nt + np_]]
        cvals = tuple(r[...].astype(f32) for r in refs[nh + nt + np_:nh + nt + np_ + nc])
        n_in = nh + nt + np_ + nc
        orefs = refs[n_in:n_in + nt]
        prefs = refs[n_in + nt:n_in + nt + np_]
        carries = refs[n_in + nt + np_:]

        @pl.when(s == 0)
        def _():
            for r in prefs:
                r[...] = jnp.zeros_like(r)
            for r in carries:
                r[...] = jnp.zeros_like(r)

        _, vjp = jax.vjp(fn, *tails, *vals)
        grads = vjp(cvals)
        for j in range(nt):
            g = grads[nh + j]
            if j in hal:
                cr = carries[hal.index(j)]
                g = jnp.concatenate([g[:tile - halo], g[tile - halo:] + cr[...]], axis=0)
            orefs[j][...] = g.astype(orefs[j].dtype)
        for n in range(nh):
            carries[n][...] = grads[n] * first
        for r, g in zip(prefs, grads[nh + nt:]):
            r[...] += g

    rev = lambda s: n_tiles - 1 - s
    in_specs = [pl.BlockSpec((halo, tiled[j][1]),
                             lambda s, c=tiled[j][2]: (jnp.maximum(rev(s) * per - 1, 0), c)) for j in hal]
    in_specs += [pl.BlockSpec((tile, w), lambda s, c=c: (rev(s), c)) for _, w, c, _ in tiled]
    in_specs += [_full_spec(p) for p in params]
    in_specs += [pl.BlockSpec((tile, w), lambda s, c=c: (rev(s), c)) for _, w, c in cots]
    out_specs = [pl.BlockSpec((tile, w), lambda s: (rev(s), 0)) for _, w, _, _ in tiled]
    out_shape = [jax.ShapeDtypeStruct((t, w), gdtype) for _, w, _, _ in tiled]
    out_specs += [_full_spec(p) for p in params]
    out_shape += [jax.ShapeDtypeStruct(p.shape, f32) for p in params]
    scratch = [pltpu.VMEM((halo, tiled[j][1]), f32) for j in hal]
    args = [tiled[j][0] for j in hal] + [x[0] for x in tiled] + list(params) + [a for a, _, _ in cots]
    res = pl.pallas_call(body, grid=(n_tiles,), in_specs=in_specs, out_specs=out_specs,
                         out_shape=out_shape, scratch_shapes=scratch, name=name)(*args)
    return res[:nt], res[nt:]


def _call_with_carry(name, body, grid, in_specs, out_specs, out_shape, scratch, args, carry):
    if carry is None:
        res = pl.pallas_call(body, grid=grid, in_specs=in_specs, out_specs=out_specs, out_shape=out_shape,
                             scratch_shapes=scratch, name=name)(*args)
        return list(res), []
    kind, arrays = carry
    nc, n_in, n_out, n_scr = len(arrays), len(in_specs), len(out_shape), len(scratch)

    def carried(*refs):
        cut = [n_in, nc, n_out, nc, n_scr]
        parts, pos = [], 0
        for k in cut:
            parts.append(refs[pos:pos + k])
            pos += k
        ins, cins, outs, couts, scr = parts
        start, finish = _carry_parts(kind, cins, couts, refs[pos:])
        ids = [pl.program_id(d) for d in range(len(grid))]
        first, last = ids[0] == 0, ids[0] == grid[0] - 1
        for d in range(1, len(grid)):
            first, last = first & (ids[d] == 0), last & (ids[d] == grid[d] - 1)
        pl.when(first)(start)
        body(*ins, *outs, *scr)
        pl.when(last)(finish)

    anyspec = pl.BlockSpec(memory_space=pl.ANY)
    res = pl.pallas_call(
        carried, grid=grid, in_specs=list(in_specs) + [anyspec] * nc, out_specs=list(out_specs) + [anyspec] * nc,
        out_shape=list(out_shape) + _carry_out_shape(kind, arrays),
        scratch_shapes=list(scratch) + _carry_sems(kind, nc), name=name)(*args, *arrays)
    return list(res[:n_out]), list(res[n_out:])


def scan_fwd(name, fn, tiled, pparams, sparams, out_dtype, hb, n_extra=0, carry=None):
    t = tiled[0][0].shape[0]
    n = t // CHUNK
    cpb = min(SCAN_CHUNKS, n)
    nt, npp, nsp = len(tiled), len(pparams), len(sparams)
    w = HD * hb

    def body(*refs):
        c = pl.program_id(1)
        n_in = nt + npp + nsp
        o_ref, sv_ref = refs[n_in], refs[n_in + 1]
        ex_refs, st = refs[n_in + 2:n_in + 2 + n_extra], refs[n_in + 2 + n_extra]

        @pl.when(c == 0)
        def _():
            st[...] = jnp.zeros_like(st)

        vals = [r[...].astype(f32) for r in refs[:n_in]]
        state = [st[hh] for hh in range(hb)]
        rows_out = []
        for j in range(cpb):
            rows = slice(j * CHUNK, (j + 1) * CHUNK)
            outs = []
            for hh in range(hb):
                sl = slice(hh * HD, (hh + 1) * HD)
                hv = [v[rows, sl] for v in vals[:nt]] + [v[:, sl] for v in vals[nt:nt + npp]] + vals[nt + npp:]
                sv_ref[hh, j] = state[hh]
                res = fn(*hv, state[hh])
                state[hh] = res[1]
                outs.append(res[0])
                for e_ref, e in zip(ex_refs, res[2:]):
                    e_ref[hh, j] = e
            rows_out.append(outs[0] if hb == 1 else jnp.concatenate(outs, axis=1))
        for hh in range(hb):
            st[hh] = state[hh]
        o_ref[...] = (rows_out[0] if cpb == 1 else jnp.concatenate(rows_out, axis=0)).astype(o_ref.dtype)

    in_specs = [pl.BlockSpec((CHUNK * cpb, w), lambda g, c, b=b: (c, b // hb + g)) for _, b in tiled]
    in_specs += [pl.BlockSpec((1, w), lambda g, c: (0, g)) for _ in pparams]
    in_specs += [_full_spec(p) for p in sparams]
    out_specs = [pl.BlockSpec((CHUNK * cpb, w), lambda g, c: (c, g)),
                 pl.BlockSpec((hb, cpb, HD, HD), lambda g, c: (g, c, 0, 0))]
    out_shape = [jax.ShapeDtypeStruct((t, GW), out_dtype), jax.ShapeDtypeStruct((NH, n, HD, HD), f32)]
    out_specs += [pl.BlockSpec((hb, cpb, CHUNK, CHUNK), lambda g, c: (g, c, 0, 0))] * n_extra
    out_shape += [jax.ShapeDtypeStruct((NH, n, CHUNK, CHUNK), f32)] * n_extra
    res, carried = _call_with_carry(name, body, (NH // hb, n // cpb), in_specs, out_specs, out_shape,
                                    [pltpu.VMEM((hb, HD, HD), f32)], [a for a, _ in tiled] + list(pparams) + list(sparams),
                                    carry)
    return res + [carried]


def scan_bwd(name, fn, tiled, pparams, sparams, states, cot, gdtypes, hb, extras=(), carry=None):
    t = tiled[0][0].shape[0]
    n = t // CHUNK
    cpb = min(SCAN_CHUNKS, n)
    nb = n // cpb
    nt, npp, nsp, nex = len(tiled), len(pparams), len(sparams), len(extras)
    w = HD * hb

    def body(*refs):
        g, s = pl.program_id(0), pl.program_id(1)
        n_in = nt + npp + nsp
        vals = [r[...].astype(f32) for r in refs[:n_in]]
        st_ref = refs[n_in]
        do = refs[n_in + 1][...].astype(f32)
        ex_refs = refs[n_in + 2:n_in + 2 + nex]
        n_op = n_in + 2 + nex
        orefs = refs[n_op:n_op + nt]
        pprefs = refs[n_op + nt:n_op + nt + npp]
        sprefs = refs[n_op + nt + npp:n_op + nt + npp + nsp]
        ds = refs[n_op + nt + npp + nsp]

        @pl.when(s == 0)
        def _():
            ds[...] = jnp.zeros_like(ds)
            for r in pprefs:
                r[...] = jnp.zeros_like(r)

        @pl.when((s == 0) & (g == 0))
        def _():
            for r in sprefs:
                r[...] = jnp.zeros_like(r)

        dstate = [ds[hh] for hh in range(hb)]
        by_chunk = [None] * cpb
        for j in reversed(range(cpb)):
            rows = slice(j * CHUNK, (j + 1) * CHUNK)
            per_head = []
            for hh in range(hb):
                sl = slice(hh * HD, (hh + 1) * HD)
                hv = [v[rows, sl] for v in vals[:nt]] + [v[:, sl] for v in vals[nt:nt + npp]] + vals[nt + npp:]
                ex = [r[hh, j] for r in ex_refs]
                _, vjp = jax.vjp(lambda *a: fn(*a, *ex), *hv, st_ref[hh, j])
                grads = vjp((do[rows, sl], dstate[hh]))
                dstate[hh] = grads[n_in]
                per_head.append(grads)
            by_chunk[j] = per_head
        for hh in range(hb):
            ds[hh] = dstate[hh]

        def lanes(j, k):
            return by_chunk[j][0][k] if hb == 1 else jnp.concatenate([gr[k] for gr in by_chunk[j]], axis=1)

        for k in range(nt):
            blk = lanes(0, k) if cpb == 1 else jnp.concatenate([lanes(j, k) for j in range(cpb)], axis=0)
            orefs[k][...] = blk.astype(orefs[k].dtype)
        for k, r in enumerate(pprefs):
            tot = lanes(0, nt + k)
            for j in range(1, cpb):
                tot = tot + lanes(j, nt + k)
            r[...] += tot
        for k, r in enumerate(sprefs):
            tot = None
            for j in range(cpb):
                for gr in by_chunk[j]:
                    tot = gr[nt + npp + k] if tot is None else tot + gr[nt + npp + k]
            r[...] += tot

    rev = lambda s: nb - 1 - s
    in_specs = [pl.BlockSpec((CHUNK * cpb, w), lambda g, s, b=b: (rev(s), b // hb + g)) for _, b in tiled]
    in_specs += [pl.BlockSpec((1, w), lambda g, s: (0, g)) for _ in pparams]
    in_specs += [_full_spec(p) for p in sparams]
    in_specs += [pl.BlockSpec((hb, cpb, HD, HD), lambda g, s: (g, rev(s), 0, 0)),
                 pl.BlockSpec((CHUNK * cpb, w), lambda g, s, b=cot[1]: (rev(s), b // hb + g))]
    in_specs += [pl.BlockSpec((hb, cpb, CHUNK, CHUNK), lambda g, s: (g, rev(s), 0, 0)) for _ in extras]
    out_specs = [pl.BlockSpec((CHUNK * cpb, w), lambda g, s: (rev(s), g)) for _ in tiled]
    out_shape = [jax.ShapeDtypeStruct((t, GW), dt) for dt in gdtypes]
    out_specs += [pl.BlockSpec((1, w), lambda g, s: (0, g)) for _ in pparams]
    out_shape += [jax.ShapeDtypeStruct(p.shape, f32) for p in pparams]
    out_specs += [_full_spec(p) for p in sparams]
    out_shape += [jax.ShapeDtypeStruct(p.shape, f32) for p in sparams]
    res, carried = _call_with_carry(
        name, body, (NH // hb, nb), in_specs, out_specs, out_shape, [pltpu.VMEM((hb, HD, HD), f32)],
        [a for a, _ in tiled] + list(pparams) + list(sparams) + [states, cot[0]] + list(extras), carry)
    return res[:nt], res[nt:nt + npp], res[nt + npp:], carried


def matmul(name, a, b, mode, out_dtype=f32, tm=1024, tn=1024, tk=2048, epilogue=None, extra=None, slots=None,
           b_gathered=None, carry=None):
    if b_gathered is not None:
        cut, layer = b_gathered
        _, _, sr, sc = b.shape
        b_rows, b_cols = (N_DEV * sr, sc) if cut == 'rows' else (sr, N_DEV * sc)
    else:
        b_rows, b_cols = b.shape
    if mode == 'nn':
        (m, k), n = a.shape, b_cols
    elif mode == 'nt':
        (m, k), n = a.shape, b_rows
    else:
        (k, m), n = a.shape, b_cols
    tm, tn, tk = min(tm, m), min(tn, n), min(tk, k)
    if b_gathered is not None:
        if (mode == 'nn') == (cut == 'cols'):
            tn = min(tn, sc if cut == 'cols' else sr)
        else:
            tk = min(tk, sr if cut == 'rows' else sc)
    if slots == 'rows':
        tm = min(tm, m // N_DEV)
    if slots == 'cols':
        tn = min(tn, n // N_DEV)
    nk = k // tk
    ca, cb = {'nn': (1, 0), 'nt': (1, 1), 'tn': (0, 0)}[mode]

    def finish(refs, r):
        if epilogue == 'relu2':
            refs[2][...] = r
            refs[3][...] = jnp.square(jnp.maximum(r, 0.0)).astype(bf16)
        elif epilogue == 'relu2_bwd':
            refs[3][...] = (r * 2.0 * jnp.maximum(refs[2][...], 0.0)).astype(refs[3].dtype)
        else:
            refs[2][...] = r.astype(refs[2].dtype)

    def body(*refs):
        part = _dotb(refs[0][...], refs[1][...], ca, cb)
        if nk == 1:
            finish(refs, part)
            return
        acc = refs[-1]
        kk = pl.program_id(2)

        @pl.when(kk == 0)
        def _():
            acc[...] = part

        @pl.when(kk > 0)
        def _():
            acc[...] += part

        @pl.when(kk == nk - 1)
        def _():
            finish(refs, acc[...])

    if mode == 'nn':
        a_spec = pl.BlockSpec((tm, tk), lambda i, j, kk: (i, kk))
        b_spec = pl.BlockSpec((tk, tn), lambda i, j, kk: (kk, j))
    elif mode == 'nt':
        a_spec = pl.BlockSpec((tm, tk), lambda i, j, kk: (i, kk))
        b_spec = pl.BlockSpec((tn, tk), lambda i, j, kk: (j, kk))
    else:
        a_spec = pl.BlockSpec((tk, tm), lambda i, j, kk: (kk, i))
        b_spec = pl.BlockSpec((tk, tn), lambda i, j, kk: (kk, j))
    if b_gathered is not None:
        bshape = (None, None, tk, tn) if mode == 'nn' else (None, None, tn, tk)
        if mode == 'nn' and cut == 'cols':
            per = sc // tn
            b_spec = pl.BlockSpec(bshape, lambda i, j, kk: (j // per, layer, kk, j % per))
        elif mode == 'nn':
            per = sr // tk
            b_spec = pl.BlockSpec(bshape, lambda i, j, kk: (kk // per, layer, kk % per, j))
        elif cut == 'cols':
            per = sc // tk
            b_spec = pl.BlockSpec(bshape, lambda i, j, kk: (kk // per, layer, j, kk % per))
        else:
            per = sr // tn
            b_spec = pl.BlockSpec(bshape, lambda i, j, kk: (j // per, layer, j % per, kk))
    o_spec = pl.BlockSpec((tm, tn), lambda i, j, kk: (i, j))
    in_specs, args = [a_spec, b_spec], [a, b]
    if epilogue == 'relu2':
        out_specs = [o_spec, o_spec]
        out_shape = [jax.ShapeDtypeStruct((m, n), f32), jax.ShapeDtypeStruct((m, n), bf16)]
    elif slots == 'rows':
        per = (m // N_DEV) // tm
        out_specs = pl.BlockSpec((None, None, tm, tn), lambda i, j, kk: ((i // per) % 2, (i // per) // 2, i % per, j))
        out_shape = jax.ShapeDtypeStruct((2, N_DEV // 2, m // N_DEV, n), out_dtype)
    elif slots == 'cols':
        per = (n // N_DEV) // tn
        out_specs = pl.BlockSpec((None, None, tm, tn), lambda i, j, kk: ((j // per) % 2, (j // per) // 2, i, j % per))
        out_shape = jax.ShapeDtypeStruct((2, N_DEV // 2, m, n // N_DEV), out_dtype)
    else:
        out_specs, out_shape = o_spec, jax.ShapeDtypeStruct((m, n), out_dtype)
        if epilogue == 'relu2_bwd':
            in_specs.append(o_spec)
            args.append(extra)
    scratch = [pltpu.VMEM((tm, tn), f32)] if nk > 1 else []
    if carry is None:
        return pl.pallas_call(body, grid=(m // tm, n // tn, nk), in_specs=in_specs, out_specs=out_specs,
                              out_shape=out_shape, scratch_shapes=scratch, name=name)(*args)
    single = not isinstance(out_shape, list)
    res, carried = _call_with_carry(name, body, (m // tm, n // tn, nk), in_specs, [out_specs] if single else out_specs,
                                    [out_shape] if single else out_shape, scratch, args, carry)
    return (res[0] if single else res), carried


def final_loss(name, x, y, g, target):
    t, d = x.shape
    tile = min(ROW_TILE, t)

    def body(x_ref, y_ref, g_ref, t_ref, dx_ref, l_ref):
        i = pl.program_id(0)

        @pl.when(i == 0)
        def _():
            l_ref[...] = jnp.zeros_like(l_ref)

        err = x_ref[...] + _rms(y_ref[...], g_ref[...]) - t_ref[...]
        dx_ref[...] = err * (1.0 / d)
        l_ref[...] += 0.5 * jnp.sum(jnp.mean(err * err, axis=-1, keepdims=True), axis=0, keepdims=True)

    row = pl.BlockSpec((tile, d), lambda i: (i, 0))
    return pl.pallas_call(
        body, grid=(t // tile,), in_specs=[row, row, _full_spec(g), row],
        out_specs=[row, pl.BlockSpec((1, 1), lambda i: (0, 0))],
        out_shape=[jax.ShapeDtypeStruct((t, d), f32), jax.ShapeDtypeStruct((1, 1), f32)], name=name)(x, y, g, target)


def adamw(name, w, m, v, gslots, tr=128):
    nl, r, c = w.shape
    tr = min(tr, r)
    nr = r // tr
    ns = gslots[0].shape[0]
    c1 = 1.0 / (1.0 - ADAM_B1 ** ADAM_STEP)
    c2 = 1.0 / (1.0 - ADAM_B2 ** ADAM_STEP)

    def body(*refs):
        w_ref, m_ref, v_ref = refs[:3]
        g_refs = refs[3:3 + nl]
        go_ref, d_ref, mo_ref, vo_ref = refs[3 + nl:]
        l = pl.program_id(0)
        g = None
        for li in range(nl):
            s = g_refs[li][0].astype(f32)
            for k in range(1, ns):
                s = s + g_refs[li][k].astype(f32)
            g = s if g is None else jnp.where(l == li, s, g)
        mn = ADAM_B1 * m_ref[...] + (1.0 - ADAM_B1) * g
        vn = ADAM_B2 * v_ref[...] + (1.0 - ADAM_B2) * jnp.square(g)
        go_ref[...] = g
        mo_ref[...] = mn
        vo_ref[...] = vn
        d_ref[...] = -ADAM_LR * ((mn * c1) / (jnp.sqrt(vn * c2) + ADAM_EPS) + ADAM_WD * w_ref[...])

    blk = pl.BlockSpec((None, tr, c), lambda l, i: (l, i, 0))

    def gspec(li):
        return pl.BlockSpec((ns, tr, c), lambda l, i: (0, jnp.where(l == li, i, jnp.where(l < li, 0, nr - 1)), 0))

    return pl.pallas_call(
        body, grid=(nl, nr), in_specs=[blk, blk, blk] + [gspec(li) for li in range(nl)],
        out_specs=[blk] * 4, out_shape=[jax.ShapeDtypeStruct(w.shape, f32)] * 4, name=name)(w, m, v, *gslots)


def exchange(name, arrays, gather, group):
    n = len(arrays)
    ns = {'all': 8, 'chips': 4, 'core': 2}[group]

    def body(*refs):
        ins, outs = refs[:n], refs[n:2 * n]
        send_sems, recv_sems, loc_sems = refs[2 * n:]
        x, y, c = lax.axis_index("x"), lax.axis_index("y"), lax.axis_index("c")

        def member(k):
            if group == 'all':
                px, py, pc = x ^ ((k >> 2) & 1), y ^ ((k >> 1) & 1), c ^ (k & 1)
                return (px, py, pc), 4 * px + 2 * py + pc
            if group == 'chips':
                px, py = x ^ ((k >> 1) & 1), y ^ (k & 1)
                return (px, py, c), 2 * px + py
            return (x, y, c ^ k), c ^ k

        _, me = member(0)
        sends, recvs, locs = [], [], []
        for a in range(n):
            lc = pltpu.make_async_copy(ins[a] if gather else ins[a].at[me], outs[a].at[me], loc_sems.at[a])
            lc.start()
            locs.append(lc)
            for k in range(1, ns):
                dev, peer = member(k)
                src = ins[a] if gather else ins[a].at[peer]
                cp = pltpu.make_async_remote_copy(src_ref=src, dst_ref=outs[a].at[me], send_sem=send_sems.at[a, k],
                                                  recv_sem=recv_sems.at[a, k], device_id=dev, device_id_type=MESH)
                cp.start()
                sends.append(cp)
                recvs.append(pltpu.make_async_remote_copy(src_ref=src, dst_ref=outs[a].at[peer], send_sem=send_sems.at[a, k],
                                                          recv_sem=recv_sems.at[a, k], device_id=dev, device_id_type=MESH))
        for cp in recvs:
            cp.wait_recv()
        for cp in sends:
            cp.wait_send()
        for lc in locs:
            lc.wait()

    anyspec = pl.BlockSpec(memory_space=pl.ANY)
    out_shape = [jax.ShapeDtypeStruct(((ns,) + a.shape) if gather else a.shape, a.dtype) for a in arrays]
    return pl.pallas_call(
        body, in_specs=[anyspec] * n, out_specs=[anyspec] * n, out_shape=out_shape,
        scratch_shapes=[pltpu.SemaphoreType.DMA((n, ns)), pltpu.SemaphoreType.DMA((n, ns)),
                        pltpu.SemaphoreType.DMA((n,))], name=name)(*arrays)


def gather_two_level(name, arrays):
    n = len(arrays)

    def body(*refs):
        start, finish = _gather_parts(refs[:n], refs[n:2 * n], *refs[2 * n:])
        start()
        finish()

    anyspec = pl.BlockSpec(memory_space=pl.ANY)
    return pl.pallas_call(
        body, in_specs=[anyspec] * n, out_specs=[anyspec] * n, out_shape=_carry_out_shape('gather', arrays),
        scratch_shapes=_carry_sems('gather', n), name=name)(*arrays)


def _gather_parts(ins, outs, send_sems, recv_sems, loc_sems):
    n = len(ins)
    x, y, c = lax.axis_index("x"), lax.axis_index("y"), lax.axis_index("c")
    sib = (x, y, 1 - c)

    def chip(k):
        px, py = x ^ ((k >> 1) & 1), y ^ (k & 1)
        return (px, py), 2 * px + py

    _, mine = chip(0)

    def copy(a, sem, src, slot, to):
        return pltpu.make_async_remote_copy(src_ref=src, dst_ref=outs[a].at[slot], send_sem=send_sems.at[a, sem],
                                            recv_sem=recv_sems.at[a, sem], device_id=to, device_id_type=MESH)

    def local(a):
        return pltpu.make_async_copy(ins[a], outs[a].at[2 * mine + c], loc_sems.at[a])

    def own_sends(a):
        cps = [copy(a, 0, ins[a], 2 * mine + c, sib)]
        for k in range(1, 4):
            (px, py), _ = chip(k)
            cps.append(copy(a, k, ins[a], 2 * mine + c, (px, py, c)))
        return cps

    def start():
        for a in range(n):
            local(a).start()
            for cp in own_sends(a):
                cp.start()

    def finish():
        passed = []
        for a in range(n):
            for k in range(1, 4):
                _, other = chip(k)
                slot = 2 * other + c
                copy(a, k, outs[a].at[slot], slot, sib).wait_recv()
                fw = copy(a, 3 + k, outs[a].at[slot], slot, sib)
                fw.start()
                passed.append(fw)
        for a in range(n):
            copy(a, 0, ins[a], 2 * mine + 1 - c, sib).wait_recv()
            for k in range(1, 4):
                _, other = chip(k)
                slot = 2 * other + 1 - c
                copy(a, 3 + k, outs[a].at[slot], slot, sib).wait_recv()
        for a in range(n):
            for cp in own_sends(a):
                cp.wait_send()
        for cp in passed:
            cp.wait_send()
        for a in range(n):
            local(a).wait()

    return start, finish


def _scatter_parts(ins, outs, send_sems, recv_sems, loc_sems):
    n = len(ins)
    x, y, c = lax.axis_index("x"), lax.axis_index("y"), lax.axis_index("c")
    mine = 2 * x + y

    def local(a):
        return pltpu.make_async_copy(ins[a].at[mine], outs[a].at[mine], loc_sems.at[a])

    def remote(a, k, slot):
        px, py = x ^ ((k >> 1) & 1), y ^ (k & 1)
        return pltpu.make_async_remote_copy(src_ref=ins[a].at[2 * px + py], dst_ref=outs[a].at[slot],
                                            send_sem=send_sems.at[a, k], recv_sem=recv_sems.at[a, k],
                                            device_id=(px, py, c), device_id_type=MESH)

    def start():
        for a in range(n):
            local(a).start()
            for k in range(1, 4):
                remote(a, k, mine).start()

    def finish():
        for a in range(n):
            for k in range(1, 4):
                remote(a, k, 2 * (x ^ ((k >> 1) & 1)) + (y ^ (k & 1))).wait_recv()
        for a in range(n):
            for k in range(1, 4):
                remote(a, k, mine).wait_send()
            local(a).wait()

    return start, finish


def _carry_out_shape(kind, arrays):
    if kind == 'gather':
        return [jax.ShapeDtypeStruct((N_DEV,) + a.shape, a.dtype) for a in arrays]
    return [jax.ShapeDtypeStruct(a.shape, a.dtype) for a in arrays]


def _carry_sems(kind, n):
    k = 7 if kind == 'gather' else 4
    return [pltpu.SemaphoreType.DMA((n, k)), pltpu.SemaphoreType.DMA((n, k)), pltpu.SemaphoreType.DMA((n,))]


def _carry_parts(kind, ins, outs, sems):
    return (_gather_parts if kind == 'gather' else _scatter_parts)(ins, outs, *sems)


def chip_partials(tag, slots):
    from_sibling = send_to_sibling(f"to_sibling_{tag}", slots)
    return [chip_sum(f"chip_sum_{tag}{i}", b, o) for i, (b, o) in enumerate(zip(slots, from_sibling))]


def send_to_sibling(name, arrays):
    n = len(arrays)

    def body(*refs):
        ins, outs = refs[:n], refs[n:2 * n]
        send_sems, recv_sems = refs[2 * n:]
        x, y, c = lax.axis_index("x"), lax.axis_index("y"), lax.axis_index("c")
        cps = [pltpu.make_async_remote_copy(src_ref=ins[a].at[1 - c], dst_ref=outs[a], send_sem=send_sems.at[a],
                                            recv_sem=recv_sems.at[a], device_id=(x, y, 1 - c), device_id_type=MESH)
               for a in range(n)]
        for cp in cps:
            cp.start()
        for cp in cps:
            cp.wait()

    anyspec = pl.BlockSpec(memory_space=pl.ANY)
    out_shape = [jax.ShapeDtypeStruct(a.shape[1:], a.dtype) for a in arrays]
    return pl.pallas_call(
        body, in_specs=[anyspec] * n, out_specs=[anyspec] * n, out_shape=out_shape,
        scratch_shapes=[pltpu.SemaphoreType.DMA((n,)), pltpu.SemaphoreType.DMA((n,))], name=name)(*arrays)


def chip_sum(name, both, other):
    _, nc, r, c = both.shape
    tr = ROW_TILE if r % ROW_TILE == 0 else r

    def body(b_ref, o_ref, s_ref):
        core = lax.axis_index("c")
        own = jnp.where(core == 0, b_ref[0], b_ref[1]).astype(f32)
        s_ref[...] = (own + o_ref[...].astype(f32)).astype(s_ref.dtype)

    return pl.pallas_call(
        body, grid=(nc, r // tr),
        in_specs=[pl.BlockSpec((2, None, tr, c), lambda s, i: (0, s, i, 0)),
                  pl.BlockSpec((None, tr, c), lambda s, i: (s, i, 0))],
        out_specs=pl.BlockSpec((None, tr, c), lambda s, i: (s, i, 0)),
        out_shape=jax.ShapeDtypeStruct((nc, r, c), both.dtype), name=name)(both, other)


REPLICATED = ('lower_bounds', 'norm_mix_pre', 'norm_mix_post', 'norm_ff_pre', 'norm_ff_post', 'hgrn_norm_w',
              'gdn_a_log', 'gdn_dt_bias', 'gdn_norm_w', 'gmlp_ln_w', 'gmlp_ln_b', 'gmlp_w_s', 'gmlp_b_s',
              'conv_dw_b', 'conv_ln_w', 'conv_ln_b')
WEIGHTS = ('lower_bounds', 'norm_mix_pre', 'norm_mix_post', 'norm_ff_pre', 'norm_ff_post', 'w_in', 'w_out',
           'hgrn_norm_w', 'gdn_conv_w', 'gdn_a_log', 'gdn_dt_bias', 'gdn_norm_w', 'gmlp_ln_w', 'gmlp_ln_b',
           'gmlp_w_s', 'gmlp_b_s', 'conv_dw_w', 'conv_dw_b', 'conv_ln_w', 'conv_ln_b', 'w_ff1', 'w_ff2')


def _row(v):
    return v.reshape(1, -1)


def _pad_lanes(v, offset):
    return jnp.pad(v, (offset, 128 - offset - v.shape[0])).reshape(1, 128)


def _relayout_w_in(g):
    full = jnp.moveaxis(g[:, 0], 0, 1).reshape(D_MODEL, D_IN)
    return jnp.concatenate([full[:, :8 * GW], full[:, 8 * GW + 2 * NH:], full[:, 8 * GW:8 * GW + 2 * NH],
                            jnp.zeros((D_MODEL, 128 - 2 * NH), bf16)], axis=1)


def _layer_fwd(l, x0, params, lb_all, shards):
    p = params[l]
    sv = {'x0': x0}
    (h,) = rowwise(f"norm_mix_pre{l}", fn_norm, [(x0, D_MODEL, 0)], [p['g_mix_pre']], [(D_MODEL, bf16)], ROW_TILE)
    proj = matmul(f"proj{l}", h, p['w_in'], 'nn', f32, tn=896)
    sv.update(h=h, proj=proj)
    lb = lb_all[l:l + 1]
    o_a, st_a, (p['w_ff1'],) = scan_fwd(f"hgrn{l}", fn_hgrn, [(proj, 0), (proj, 4), (proj, 8), (proj, 12)], [lb],
                                        [p['hgrn_norm_w']], bf16, NH, carry=('gather', [shards['w_ff1', l]]))
    q, k, v, beta, g = halo_fwd(
        f"gdn_pre{l}", fn_gdn_pre,
        [(proj, GW, 4, True), (proj, GW, 5, True), (proj, GW, 6, True), (proj, 128, 48, False)],
        [p['gdn_conv_w'], p['alog'], p['dtb']], [(GW, f32)] * 5, ROW_TILE, 8)
    wanted = [shards['w_ff2', l]] + ([shards['w_out']] if l == 0 else []) + ([shards['w_in', l + 1]] if l + 1 < DEPTH else [])
    o_b, st_b, tinv_b, got = scan_fwd(f"gdn{l}", fn_gdn, [(q, 0), (k, 0), (v, 0), (beta, 0), (g, 0), (proj, 28)], [],
                                      [p['gdn_norm_w']], bf16, NH, n_extra=1, carry=('gather', wanted))
    p['w_ff2'] = got.pop(0)
    if l == 0:
        w_out_full = jnp.moveaxis(got.pop(0), 0, 1).reshape(DEPTH, D_MODEL, D_MODEL)
        for ll in range(DEPTH):
            params[ll]['w_out'] = w_out_full[ll]
    if l + 1 < DEPTH:
        params[l + 1]['w_in'] = _relayout_w_in(got.pop(0))
    (o_c,) = rowwise(f"gmlp{l}", fn_gmlp, [(proj, GW, 8), (proj, GW, 9)],
                     [p['gmlp_ln_w'], p['gmlp_ln_b'], p['gmlp_w_s'], p['gmlp_b_s']], [(GW, bf16)], MIX_CHUNK)
    (o_d,) = halo_fwd(f"conv{l}", fn_conv, [(proj, GW, 10, True), (proj, GW, 11, True)],
                      [p['conv_dw_w'], p['conv_dw_b'], p['conv_ln_w'], p['conv_ln_b']], [(GW, bf16)], ROW_TILE, 32)
    mix = jnp.concatenate([o_a, o_b, o_c, o_d], axis=1)
    y1 = matmul(f"out_proj{l}", mix, p['w_out'], 'nn', f32)
    (x1,) = rowwise(f"res_mix{l}", lambda x, y, gg: (x + _rms(y, gg),), [(x0, D_MODEL, 0), (y1, D_MODEL, 0)],
                    [p['g_mix_post']], [(D_MODEL, f32)], ROW_TILE)
    (h2,) = rowwise(f"norm_ff_pre{l}", fn_norm, [(x1, D_MODEL, 0)], [p['g_ff_pre']], [(D_MODEL, bf16)], ROW_TILE)
    u, a = matmul(f"ff1_{l}", h2, p['w_ff1'], 'nn', epilogue='relu2', b_gathered=('cols', 0))
    y2 = matmul(f"ff2_{l}", a, p['w_ff2'], 'nn', f32, b_gathered=('rows', 0))
    sv.update(st_a=st_a, q=q, k=k, v=v, beta=beta, g=g, st_b=st_b, tinv_b=tinv_b, mix=mix, y1=y1, x1=x1, h2=h2, u=u, a=a, y2=y2)
    return sv


def _slots_w_in(gt):
    gl = jnp.concatenate([gt[:8 * GW], gt[12 * GW:12 * GW + 2 * NH], gt[8 * GW:12 * GW]], axis=0)
    return jnp.transpose(gl.reshape(N_DEV // 2, 2, D_IN // N_DEV, D_MODEL), (1, 0, 2, 3))


def sum_slots(name, slots):
    ns, r, c = slots.shape
    tc = 512

    def body(s_ref, o_ref):
        tot = s_ref[0].astype(f32)
        for k in range(1, ns):
            tot = tot + s_ref[k].astype(f32)
        o_ref[...] = tot

    return pl.pallas_call(
        body, grid=(c // tc,), in_specs=[pl.BlockSpec((ns, r, tc), lambda j: (0, 0, j))],
        out_specs=pl.BlockSpec((r, tc), lambda j: (0, j)), out_shape=jax.ShapeDtypeStruct((r, c), f32), name=name)(slots)


def _layer_bwd(l, dx, sv, p, lb_all, received):
    gr = {}
    (dy2,), (gr['norm_ff_post'],) = rowwise_bwd(f"res_ff_bwd{l}", fn_norm, [(sv['y2'], D_MODEL, 0)], [p['g_ff_post']],
                                                [(dx, D_MODEL, 0)], [bf16], ROW_TILE)
    du = matmul(f"ff2_dx{l}", dy2, p['w_ff2'], 'nt', bf16, epilogue='relu2_bwd', extra=sv['u'], b_gathered=('rows', 0))
    g_ff2 = matmul(f"ff2_dw{l}", sv['a'], dy2, 'tn', bf16, slots='rows')
    dh2 = matmul(f"ff1_dx{l}", du, p['w_ff1'], 'nt', f32, b_gathered=('cols', 0))
    g_ff1 = matmul(f"ff1_dw{l}", sv['h2'], du, 'tn', bf16, slots='cols')
    (dx1,), (gr['norm_ff_pre'],) = rowwise_bwd(f"norm_ff_pre_bwd{l}", fn_norm, [(sv['x1'], D_MODEL, 0)], [p['g_ff_pre']],
                                               [(dh2, D_MODEL, 0)], [f32], ROW_TILE, addto=(dx, D_MODEL, 0))
    (dy1,), (gr['norm_mix_post'],) = rowwise_bwd(f"res_mix_bwd{l}", fn_norm, [(sv['y1'], D_MODEL, 0)], [p['g_mix_post']],
                                                 [(dx1, D_MODEL, 0)], [bf16], ROW_TILE)
    dmix = matmul(f"out_proj_dx{l}", dy1, p['w_out'], 'nt', f32)
    g_out = matmul(f"out_proj_dw{l}", sv['mix'], dy1, 'tn', bf16, slots='rows')
    part_ff2, part_ff1, part_out = chip_partials(f"l{l}_", [g_ff2, g_ff1, g_out])
    proj = sv['proj']
    lb = lb_all[l:l + 1]
    going = [(('w_ff2', l), part_ff2)]
    d_a, (dlb,), (gr['hgrn_norm_w'],), got = scan_bwd(
        f"hgrn_bwd{l}", fn_hgrn, [(proj, 0), (proj, 4), (proj, 8), (proj, 12)], [lb], [p['hgrn_norm_w']],
        sv['st_a'], (dmix, 0), [bf16] * 4, NH, carry=('scatter', [a for _, a in going]))
    received.update({key: r for (key, _), r in zip(going, got)})
    going = [(('w_ff1', l), part_ff1), (('w_out', l), part_out)]
    d_b, _, (gr['gdn_norm_w'],), got = scan_bwd(
        f"gdn_bwd{l}", fn_gdn, [(sv['q'], 0), (sv['k'], 0), (sv['v'], 0), (sv['beta'], 0), (sv['g'], 0), (proj, 28)],
        [], [p['gdn_norm_w']], sv['st_b'], (dmix, 4), [f32] * 5 + [bf16], NH, extras=[sv['tinv_b']],
        carry=('scatter', [a for _, a in going]))
    received.update({key: r for (key, _), r in zip(going, got)})
    d_bp, (gr['gdn_conv_w'], dalog, ddtb) = halo_bwd(
        f"gdn_pre_bwd{l}", fn_gdn_pre,
        [(proj, GW, 4, True), (proj, GW, 5, True), (proj, GW, 6, True), (proj, 128, 48, False)],
        [p['gdn_conv_w'], p['alog'], p['dtb']], [(d_b[j], GW, 0) for j in range(5)], bf16, ROW_TILE, 8)
    gr['gdn_a_log'] = dalog[0, NH:2 * NH]
    gr['gdn_dt_bias'] = ddtb[0, NH:2 * NH]
    d_c, (gr['gmlp_ln_w'], gr['gmlp_ln_b'], gr['gmlp_w_s'], gr['gmlp_b_s']) = rowwise_bwd(
        f"gmlp_bwd{l}", fn_gmlp, [(proj, GW, 8), (proj, GW, 9)],
        [p['gmlp_ln_w'], p['gmlp_ln_b'], p['gmlp_w_s'], p['gmlp_b_s']], [(dmix, GW, 2)], [bf16, bf16], MIX_CHUNK)
    d_d, (gr['conv_dw_w'], gr['conv_dw_b'], gr['conv_ln_w'], gr['conv_ln_b']) = halo_bwd(
        f"conv_bwd{l}", fn_conv, [(proj, GW, 10, True), (proj, GW, 11, True)],
        [p['conv_dw_w'], p['conv_dw_b'], p['conv_ln_w'], p['conv_ln_b']], [(dmix, GW, 3)], bf16, ROW_TILE, 32)
    dproj = jnp.concatenate(list(d_a) + [d_bp[0], d_bp[1], d_bp[2], d_b[5]] + list(d_c) + list(d_d) + [d_bp[3]], axis=1)
    g_in_t = matmul(f"proj_dw{l}", dproj, sv['h'], 'tn', bf16, tm=896)
    (part_in,) = chip_partials(f"l{l}_in", [_slots_w_in(g_in_t)])
    dh, (received['w_in', l],) = matmul(f"proj_dx{l}", dproj, p['w_in'], 'nt', f32, tm=512, tn=512, tk=D_IN_PAD,
                                        carry=('scatter', [part_in]))
    (dx0,), (gr['norm_mix_pre'],) = rowwise_bwd(f"norm_mix_pre_bwd{l}", fn_norm, [(sv['x0'], D_MODEL, 0)], [p['g_mix_pre']],
                                                [(dh, D_MODEL, 0)], [f32], ROW_TILE, addto=(dx1, D_MODEL, 0))
    return dx0, gr, dlb


def kernel(x, lower_bounds, norm_mix_pre, norm_mix_post, norm_ff_pre, norm_ff_post, w_in, w_out, hgrn_norm_w, gdn_conv_w, gdn_a_log, gdn_dt_bias, gdn_norm_w, gmlp_ln_w, gmlp_ln_b, gmlp_w_s, gmlp_b_s, conv_dw_w, conv_dw_b, conv_ln_w, conv_ln_b, w_ff1, w_ff2, loss_target, m_lower_bounds, m_norm_mix_pre, m_norm_mix_post, m_norm_ff_pre, m_norm_ff_post, m_w_in, m_w_out, m_hgrn_norm_w, m_gdn_conv_w, m_gdn_a_log, m_gdn_dt_bias, m_gdn_norm_w, m_gmlp_ln_w, m_gmlp_ln_b, m_gmlp_w_s, m_gmlp_b_s, m_conv_dw_w, m_conv_dw_b, m_conv_ln_w, m_conv_ln_b, m_w_ff1, m_w_ff2, v_lower_bounds, v_norm_mix_pre, v_norm_mix_post, v_norm_ff_pre, v_norm_ff_post, v_w_in, v_w_out, v_hgrn_norm_w, v_gdn_conv_w, v_gdn_a_log, v_gdn_dt_bias, v_gdn_norm_w, v_gmlp_ln_w, v_gmlp_ln_b, v_gmlp_w_s, v_gmlp_b_s, v_conv_dw_w, v_conv_dw_b, v_conv_ln_w, v_conv_ln_b, v_w_ff1, v_w_ff2):
    loc = dict(locals())
    W = {n: loc[n] for n in WEIGHTS}
    M = {n: loc['m_' + n] for n in WEIGHTS}
    V = {n: loc['v_' + n] for n in WEIGHTS}
    t = x.shape[1]
    me = 4 * lax.axis_index("x") + 2 * lax.axis_index("y") + lax.axis_index("c")

    shards = {'w_out': w_out.astype(bf16)}
    for l in range(DEPTH):
        shards['w_in', l] = w_in[l:l + 1].astype(bf16)
        shards['w_ff1', l] = w_ff1[l:l + 1].astype(bf16)
        shards['w_ff2', l] = w_ff2[l:l + 1].astype(bf16)
    g_in0, g_gconv, g_dconv = gather_two_level("gather_weights", [shards['w_in', 0], gdn_conv_w, conv_dw_w])
    gconv_full = jnp.moveaxis(g_gconv, 0, 2).reshape(DEPTH, SHORT_CONV, 3 * GW)
    dconv_full = jnp.moveaxis(g_dconv, 0, 2).reshape(DEPTH, CONV_WIDTH, GW)

    (lb_all,) = rowwise("lower_bounds", fn_lb, [(lower_bounds, GW, 0)], [], [(GW, f32)], DEPTH)

    P = []
    for l in range(DEPTH):
        P.append(dict(
            g_mix_pre=_row(norm_mix_pre[l]), g_mix_post=_row(norm_mix_post[l]), g_ff_pre=_row(norm_ff_pre[l]),
            g_ff_post=_row(norm_ff_post[l]), hgrn_norm_w=_row(hgrn_norm_w[l]), gdn_conv_w=gconv_full[l],
            alog=_pad_lanes(gdn_a_log[l], NH), dtb=_pad_lanes(gdn_dt_bias[l], NH), gdn_norm_w=_row(gdn_norm_w[l]),
            gmlp_ln_w=_row(gmlp_ln_w[l]), gmlp_ln_b=_row(gmlp_ln_b[l]), gmlp_w_s=gmlp_w_s[l].reshape(NH * MIX_CHUNK, MIX_CHUNK),
            gmlp_b_s=gmlp_b_s[l], conv_dw_w=dconv_full[l], conv_dw_b=_row(conv_dw_b[l]), conv_ln_w=_row(conv_ln_w[l]),
            conv_ln_b=_row(conv_ln_b[l])))

    P[0]['w_in'] = _relayout_w_in(g_in0)
    xs = x[0]
    saved = []
    for l in range(DEPTH):
        sv = _layer_fwd(l, xs, P, lb_all, shards)
        saved.append(sv)
        if l < DEPTH - 1:
            (xs,) = rowwise(f"res_ff{l}", lambda a, y, gg: (a + _rms(y, gg),), [(sv['x1'], D_MODEL, 0), (sv['y2'], D_MODEL, 0)],
                            [P[l]['g_ff_post']], [(D_MODEL, f32)], ROW_TILE)
    sv = saved[-1]
    dx, loss_loc = final_loss("final_loss", sv['x1'], sv['y2'], P[-1]['g_ff_post'], loss_target[0])
    loss = lax.psum(loss_loc[0, 0], ("x", "y", "c"))

    G = {}
    dlb_rows = []
    received = {}
    for l in reversed(range(DEPTH)):
        dx, gr, dlb = _layer_bwd(l, dx, saved[l], P[l], lb_all, received)
        G[l] = gr
        dlb_rows.append(dlb)
    dlb_all = jnp.concatenate(dlb_rows[::-1], axis=0)
    (g_lower_bounds,), _ = rowwise_bwd("lower_bounds_bwd", fn_lb, [(lower_bounds, GW, 0)], [], [(dlb_all, GW, 0)],
                                       [f32], DEPTH)
    grad_x = dx[None]

    def stack(name, f=lambda a: a):
        return jnp.stack([f(G[l][name]) for l in range(DEPTH)], axis=0)

    full = {
        'lower_bounds': g_lower_bounds,
        'norm_mix_pre': stack('norm_mix_pre', lambda a: a[0]), 'norm_mix_post': stack('norm_mix_post', lambda a: a[0]),
        'norm_ff_pre': stack('norm_ff_pre', lambda a: a[0]), 'norm_ff_post': stack('norm_ff_post', lambda a: a[0]),
        'hgrn_norm_w': stack('hgrn_norm_w', lambda a: a[0]), 'gdn_a_log': stack('gdn_a_log'), 'gdn_dt_bias': stack('gdn_dt_bias'),
        'gdn_norm_w': stack('gdn_norm_w', lambda a: a[0]), 'gmlp_ln_w': stack('gmlp_ln_w', lambda a: a[0]),
        'gmlp_ln_b': stack('gmlp_ln_b', lambda a: a[0]),
        'gmlp_w_s': stack('gmlp_w_s', lambda a: a.reshape(NH, MIX_CHUNK, MIX_CHUNK)), 'gmlp_b_s': stack('gmlp_b_s'),
        'conv_dw_b': stack('conv_dw_b', lambda a: a[0]), 'conv_ln_w': stack('conv_ln_w', lambda a: a[0]),
        'conv_ln_b': stack('conv_ln_b', lambda a: a[0]),
        'gdn_conv_w': stack('gdn_conv_w'), 'conv_dw_w': stack('conv_dw_w'),
    }

    small_names = list(REPLICATED) + ['gdn_conv_w', 'conv_dw_w']
    flat = jnp.concatenate([full[n].reshape(-1) for n in small_names])
    n_small = flat.shape[0]
    n_pad = -(-n_small // 1024) * 1024
    packed = jnp.pad(flat, (0, n_pad - n_small)).reshape(n_pad // 128, 128)

    (small_slots,) = exchange("gather_small_grads", [packed], True, 'all')

    out = {}
    for l in range(DEPTH):
        received['w_in', l] = sum_slots(f"sum_w_in{l}", received['w_in', l]).T[None]
    for name in ('w_in', 'w_out', 'w_ff1', 'w_ff2'):
        out[name] = adamw(f"adamw_{name}", W[name], M[name], V[name], [received[name, l] for l in range(DEPTH)])

    def pack(d, fill):
        parts = [d[n].reshape(-1) for n in REPLICATED]
        parts.append(jnp.full((n_pad - sum(a.shape[0] for a in parts),), fill, f32))
        return jnp.concatenate(parts).reshape(1, n_pad // 128, 128)

    sm = adamw("adamw_small", pack(W, 0.0), pack(M, 0.0), pack(V, 1.0), [small_slots], tr=n_pad // 128)
    off = 0
    for n in REPLICATED:
        sz = W[n].size
        out[n] = tuple(a.reshape(-1)[off:off + sz].reshape(W[n].shape) for a in sm)
        off += sz
    gsum = sm[0].reshape(-1)
    for n, full_shape in (('gdn_conv_w', (DEPTH, SHORT_CONV, 3 * GW)), ('conv_dw_w', (DEPTH, CONV_WIDTH, GW))):
        sz = math.prod(full_shape)
        gfull = gsum[off:off + sz].reshape(full_shape)
        off += sz
        sh = W[n].shape
        gmine = lax.dynamic_slice_in_dim(gfull, me * sh[2], sh[2], axis=2)
        r = adamw(f"adamw_{n}", W[n].reshape(1, sh[0] * sh[1], sh[2]), M[n].reshape(1, sh[0] * sh[1], sh[2]),
                  V[n].reshape(1, sh[0] * sh[1], sh[2]), [gmine.reshape(1, sh[0] * sh[1], sh[2])], tr=sh[0] * sh[1])
        out[n] = tuple(a.reshape(sh) for a in r)

    return (loss, grad_x, *[out[n][0] for n in WEIGHTS], *[out[n][1] for n in WEIGHTS],
            *[out[n][2] for n in WEIGHTS], *[out[n][3] for n in WEIGHTS])
```
